```python
import jax
import jax.numpy as jnp
from jax import lax
import numpy as np

D_MODEL = 1024
BATCH = 1
SEQ = 16384
DEPTH = 1

CTX_LEN = 256
GRID_W = 64
D_CONV = D_MODEL // 2
CONV_WIDTH = 31
CONV_PAD = CONV_WIDTH // 2
N_GLA_HEADS = 4
D_GLA_V = D_MODEL // 2
GLA_HEAD_V = D_GLA_V // N_GLA_HEADS
GLA_HEAD_K = GLA_HEAD_V // 2
D_GLA_K = N_GLA_HEADS * GLA_HEAD_K
GLA_RANK = 16
GLA_TAU = 16.0
CHUNK = 64
D_MIX = D_CONV + D_GLA_V
D_IN = 2 * D_CONV + 2 * D_GLA_K + 2 * D_GLA_V + 2 * GLA_RANK
SPLITS = (2 * D_CONV,
          2 * D_CONV + D_GLA_K,
          2 * D_CONV + 2 * D_GLA_K,
          2 * D_CONV + 2 * D_GLA_K + D_GLA_V,
          2 * D_CONV + 2 * D_GLA_K + 2 * D_GLA_V,
          2 * D_CONV + 2 * D_GLA_K + 2 * D_GLA_V + GLA_RANK)
D_FF = -(-8 * D_MODEL // 768) * 256
EPS = 1e-6

kernel_name = 'hybrid_conformer_gla_dit_block'


def rms_norm(x, g):
    xf = x.astype(jnp.float32)
    y = xf * lax.rsqrt(jnp.mean(xf * xf, axis=-1, keepdims=True) + EPS)
    return (y * g.astype(jnp.float32)).astype(x.dtype)


def layer_norm(x, g, b):
    xf = x.astype(jnp.float32)
    mu = jnp.mean(xf, axis=-1, keepdims=True)
    var = jnp.mean(jnp.square(xf - mu), axis=-1, keepdims=True)
    y = (xf - mu) * lax.rsqrt(var + EPS)
    return (y * g.astype(jnp.float32) + b.astype(jnp.float32)).astype(x.dtype)


def modulate(x, shift, scale):
    return x * (1 + scale) + shift


def conformer_conv(u, conv_w, conv_b, ln_g, ln_b, n_seg, seg_len):
    a, gate = jnp.split(u, 2, axis=-1)
    v = a * jax.nn.sigmoid(gate)
    bsz, L, C = v.shape
    vs = v.reshape(bsz * n_seg, seg_len, C)
    y = lax.conv_general_dilated(vs, conv_w[:, None, :].astype(vs.dtype), (1,),
                                 [(CONV_PAD, CONV_PAD)],
                                 dimension_numbers=('NWC', 'WIO', 'NWC'),
                                 feature_group_count=C)
    y = y.reshape(bsz, L, C) + conv_b
    return jax.nn.silu(layer_norm(y, ln_g, ln_b))


def gla_chunked(q, k, v, log_a, s0, with_output):
    f32 = jnp.float32
    bsz, nh, L, dk = k.shape
    dv = v.shape[-1]
    n = L // CHUNK
    kc = k.astype(f32).reshape(bsz, nh, n, CHUNK, dk)
    vc = v.astype(f32).reshape(bsz, nh, n, CHUNK, dv)
    b = jnp.cumsum(log_a.astype(f32).reshape(bsz, nh, n, CHUNK, dk), axis=3)
    b_last = b[:, :, :, -1:, :]
    chunk_kv = jnp.einsum('bhncd,bhncv->bhndv', kc * jnp.exp(b_last - b), vc)
    decay = jnp.exp(b_last[:, :, :, 0, :])

    def step(s, inp):
        dec, kv = inp
        return dec[..., None] * s + kv, s

    s_final, s_enter = lax.scan(step, s0.astype(f32),
                                (jnp.moveaxis(decay, 2, 0), jnp.moveaxis(chunk_kv, 2, 0)))
    if not with_output:
        return None, s_final
    s_enter = jnp.moveaxis(s_enter, 0, 2)
    qc = q.astype(f32).reshape(bsz, nh, n, CHUNK, dk) * (dk ** -0.5)
    q_t = qc * jnp.exp(b)
    k_t = kc * jnp.exp(-b)
    attn = jnp.einsum('bhncd,bhnsd->bhncs', q_t, k_t)
    mask = jnp.tril(jnp.ones((CHUNK, CHUNK), dtype=bool))
    attn = jnp.where(mask, attn, 0.0)
    o = (jnp.einsum('bhncs,bhnsv->bhncv', attn, vc)
         + jnp.einsum('bhncd,bhndv->bhncv', q_t, s_enter))
    return o.reshape(bsz, nh, L, dv).astype(v.dtype), s_final


def gla_bidir(q, k, v, la_f, la_b, s0_f, s0_b, with_output):
    flip = lambda t: jnp.flip(t, axis=2)
    o_f, s_f = gla_chunked(q, k, v, la_f, s0_f, with_output)
    o_b, s_b = gla_chunked(flip(q), flip(k), flip(v), flip(la_b), s0_b, with_output)
    o = o_f + flip(o_b) if with_output else None
    return o, s_f, s_b


def project(h, w_in, w_a2_f, b_a_f, w_a2_b, b_a_b):
    p = h @ w_in
    u_conv, q, k, v, g, r_f, r_b = jnp.split(p, SPLITS, axis=-1)
    bsz, L, _ = h.shape
    heads = lambda t, d: t.reshape(bsz, L, N_GLA_HEADS, d).transpose(0, 2, 1, 3)
    la_f = jax.nn.log_sigmoid((r_f @ w_a2_f + b_a_f).astype(jnp.float32)) / GLA_TAU
    la_b = jax.nn.log_sigmoid((r_b @ w_a2_b + b_a_b).astype(jnp.float32)) / GLA_TAU
    return (u_conv, heads(q, GLA_HEAD_K), heads(k, GLA_HEAD_K), heads(v, GLA_HEAD_V), g,
            heads(la_f, GLA_HEAD_K), heads(la_b, GLA_HEAD_K))


def merge(u_conv, o, g, n_seg, seg_len, conv_w, conv_b, conv_ln_g, conv_ln_b, gla_norm_g, w_out):
    conv_o = conformer_conv(u_conv, conv_w, conv_b, conv_ln_g, conv_ln_b, n_seg, seg_len)
    bsz, nh, L, dv = o.shape
    o = rms_norm(o, gla_norm_g).transpose(0, 2, 1, 3).reshape(bsz, L, nh * dv).astype(g.dtype)
    o = o * jax.nn.silu(g)
    return jnp.concatenate([conv_o.astype(g.dtype), o], axis=-1) @ w_out


def swiglu(h, w_gate, w_up, w_down):
    return (jax.nn.silu(h @ w_gate) * (h @ w_up)) @ w_down


def setup_inputs(seed: int = 0) -> dict:
    key = jax.random.key(seed)
    ks = jax.random.split(key, 23)
    nrm = lambda k, s, sc: jax.random.normal(k, s, jnp.float32) * sc
    L = DEPTH
    return {
        'x': nrm(ks[0], (BATCH, SEQ, D_MODEL), 1.0),
        'c': nrm(ks[1], (BATCH, D_MODEL), 1.0),
        'ctx': nrm(ks[2], (BATCH, CTX_LEN, D_MODEL), 1.0),
        'c_ctx': nrm(ks[3], (D_MODEL,), 1.0),
        'w_mod': nrm(ks[4], (L, D_MODEL, 6 * D_MODEL), 0.5 * D_MODEL ** -0.5),
        'b_mod': nrm(ks[5], (L, 6 * D_MODEL), 0.02),
        'norm1_g': 1.0 + nrm(ks[6], (L, D_MODEL), 0.02),
        'norm2_g': 1.0 + nrm(ks[7], (L, D_MODEL), 0.02),
        'w_in': nrm(ks[8], (L, D_MODEL, D_IN), D_MODEL ** -0.5),
        'conv_w': nrm(ks[9], (L, CONV_WIDTH, D_CONV), CONV_WIDTH ** -0.5),
        'conv_b': nrm(ks[10], (L, D_CONV), 0.02),
        'conv_ln_g': 1.0 + nrm(ks[11], (L, D_CONV), 0.02),
        'conv_ln_b': nrm(ks[12], (L, D_CONV), 0.02),
        'w_a2_f': nrm(ks[13], (L, GLA_RANK, D_GLA_K), GLA_RANK ** -0.5),
        'b_a_f': nrm(ks[14], (L, D_GLA_K), 0.1),
        'w_a2_b': nrm(ks[15], (L, GLA_RANK, D_GLA_K), GLA_RANK ** -0.5),
        'b_a_b': nrm(ks[16], (L, D_GLA_K), 0.1),
        'gla_norm_g': 1.0 + nrm(ks[17], (L, GLA_HEAD_V), 0.02),
        'w_out': nrm(ks[18], (L, D_MIX, D_MODEL), D_MIX ** -0.5),
        'w_gate': nrm(ks[19], (L, D_MODEL, D_FF), D_MODEL ** -0.5),
        'w_up': nrm(ks[20], (L, D_MODEL, D_FF), D_MODEL ** -0.5),
        'w_down': nrm(ks[21], (L, D_FF, D_MODEL), D_FF ** -0.5),
        'final_g': 1.0 + nrm(ks[22], (D_MODEL,), 0.02),
    }


def reference(x, c, ctx, c_ctx, w_mod, b_mod, norm1_g, norm2_g, w_in, conv_w, conv_b,
              conv_ln_g, conv_ln_b, w_a2_f, b_a_f, w_a2_b, b_a_b, gla_norm_g, w_out,
              w_gate, w_up, w_down, final_g):
    bsz, n_lat, _ = x.shape
    rows = n_lat // GRID_W
    n_ctx = ctx.shape[1]
    for l in range(DEPTH):
        last = l == DEPTH - 1
        mod = jax.nn.silu(c) @ w_mod[l] + b_mod[l]
        sh1, sc1, g1, sh2, sc2, g2 = jnp.split(mod[:, None, :], 6, axis=-1)
        mod_c = jax.nn.silu(c_ctx) @ w_mod[l] + b_mod[l]
        csh1, csc1, cg1, csh2, csc2, cg2 = jnp.split(mod_c, 6, axis=-1)

        hc = modulate(rms_norm(ctx, norm1_g[l]), csh1, csc1)
        uc, qc, kc, vc, gc, laf_c, lab_c = project(hc, w_in[l], w_a2_f[l], b_a_f[l], w_a2_b[l], b_a_b[l])
        s_zero = jnp.zeros((bsz, N_GLA_HEADS, GLA_HEAD_K, GLA_HEAD_V), jnp.float32)
        oc, sf_c, sb_c = gla_bidir(qc, kc, vc, laf_c, lab_c, s_zero, s_zero, not last)

        h = modulate(rms_norm(x, norm1_g[l]), sh1, sc1)
        u, q, k, v, g, laf, lab = project(h, w_in[l], w_a2_f[l], b_a_f[l], w_a2_b[l], b_a_b[l])
        o, _, _ = gla_bidir(q, k, v, laf, lab, sf_c, sb_c, True)
        x = x + g1 * merge(u, o, g, rows, GRID_W, conv_w[l], conv_b[l], conv_ln_g[l],
                           conv_ln_b[l], gla_norm_g[l], w_out[l])
        h2 = modulate(rms_norm(x, norm2_g[l]), sh2, sc2)
        x = x + g2 * swiglu(h2, w_gate[l], w_up[l], w_down[l])

        if not last:
            ctx = ctx + cg1 * merge(uc, oc, gc, 1, n_ctx, conv_w[l], conv_b[l], conv_ln_g[l],
                                    conv_ln_b[l], gla_norm_g[l], w_out[l])
            hc2 = modulate(rms_norm(ctx, norm2_g[l]), csh2, csc2)
            ctx = ctx + cg2 * swiglu(hc2, w_gate[l], w_up[l], w_down[l])
    return rms_norm(x, final_g)
```

```python
import functools

import jax
import jax.numpy as jnp
from jax import lax
from jax.experimental import pallas as pl
from jax.experimental.pallas import tpu as pltpu

D_MODEL = 1024
D_CONV = 512
CONV_WIDTH = 31
CONV_PAD = 15
N_HEADS = 4
HEAD_K = 64
HEAD_V = 128
D_GLA_K = N_HEADS * HEAD_K
D_GLA_V = N_HEADS * HEAD_V
GLA_RANK = 16
GLA_TAU = 16.0
CHUNK = 64
EPS = 1e-6
D_FF = 2816

C_Q = 2 * D_CONV
C_K = C_Q + D_GLA_K
C_V = C_K + D_GLA_K
C_G = C_V + D_GLA_V
C_R = C_G + D_GLA_V
D_IN = C_R + 2 * GLA_RANK

LANES = 128
SEG_PITCH = 96
SEG_LEAD = 16
TILE = 256
VMEM_LIMIT = 56 * 1024 * 1024

F32 = jnp.float32
BF16 = jnp.bfloat16


def _dot(a, b):
    return jnp.dot(a, b, preferred_element_type=F32)


def _dot_nt(a, b):
    return lax.dot_general(a, b, (((1,), (1,)), ((), ())), preferred_element_type=F32)


def _sigmoid(x):
    return 1.0 / (1.0 + jnp.exp(-x))


def _silu(x):
    return x * _sigmoid(x)


def _log_sigmoid(z):
    return jnp.minimum(z, 0.0) - jnp.log(1.0 + jnp.exp(-jnp.abs(z)))


def _rms_rows(x, g):
    ms = jnp.mean(x * x, axis=-1, keepdims=True)
    return x * lax.rsqrt(ms + EPS) * g


def _gla_consts(forward):
    r = lax.broadcasted_iota(jnp.int32, (CHUNK, CHUNK), 0)
    c = lax.broadcasted_iota(jnp.int32, (CHUNK, CHUNK), 1)
    tri = (c <= r) if forward else (c >= r)
    tri_bf = jnp.where(tri, 1.0, 0.0).astype(BF16)
    rr = lax.broadcasted_iota(jnp.int32, (N_HEADS * CHUNK, LANES), 0) % CHUNK
    cc = lax.broadcasted_iota(jnp.int32, (N_HEADS * CHUNK, LANES), 1)
    keep = ((cc <= rr) if forward else (cc >= rr)) & (cc < CHUNK)
    lane = lax.broadcasted_iota(jnp.int32, (CHUNK, D_GLA_K), 1)
    head_masks = [(lane >= h * HEAD_K) & (lane < (h + 1) * HEAD_K) for h in range(N_HEADS)]
    return tri_bf, keep, head_masks


def _gla_chunk(q, k, v_bf, la, s_all, consts, forward, with_output):
    tri_bf, keep, head_masks = consts
    la_hi = la.astype(BF16)
    la_lo = (la - la_hi.astype(F32)).astype(BF16)
    b = _dot(tri_bf, la_hi) + _dot(tri_bf, la_lo)
    b_last = b[CHUNK - 1:CHUNK, :] if forward else b[0:1, :]
    k_w = k * jnp.exp(b_last - b)
    e_last = jnp.exp(b_last)

    tb = jnp.concatenate(
        [k_w, jnp.broadcast_to(e_last, (8, D_GLA_K)), jnp.zeros((CHUNK - 8, D_GLA_K), F32)], axis=0)
    tbt = tb.T
    dec_col = tbt[:, CHUNK:CHUNK + 1]
    kwt_bf = tbt.astype(BF16)
    zeros_v = jnp.zeros((CHUNK, HEAD_V), BF16)

    if with_output:
        q_t = (q * jnp.exp(b) * (HEAD_K ** -0.5)).astype(BF16)
        k_t = (k * jnp.exp(-b)).astype(BF16)
        qm = jnp.concatenate([jnp.where(m, q_t, jnp.zeros_like(q_t)) for m in head_masks], axis=0)
        kt_pad = jnp.concatenate([k_t, jnp.zeros_like(k_t)], axis=0)
        attn = _dot_nt(qm, kt_pad)
        p_bf = jnp.where(keep, attn, 0.0).astype(BF16)
        o_inter = _dot(qm, s_all.astype(BF16))

    outs, kvs = [], []
    for h in range(N_HEADS):
        rows = slice(h * CHUNK, (h + 1) * CHUNK)
        v_pad = jnp.concatenate([v_bf[:, h * HEAD_V:(h + 1) * HEAD_V], zeros_v], axis=0)
        if with_output:
            lhs = jnp.concatenate([p_bf[rows], kwt_bf[rows]], axis=0)
            res = _dot(lhs, v_pad)
            outs.append(res[:CHUNK] + o_inter[rows])
            kvs.append(res[CHUNK:])
        else:
            kvs.append(_dot(kwt_bf[rows], v_pad))
    s_new = dec_col * s_all + jnp.concatenate(kvs, axis=0)
    o = jnp.concatenate(outs, axis=1) if with_output else None
    return o, s_new


def _log_decay(r, w_a2, b_a):
    z = _dot(r.astype(BF16), w_a2.astype(BF16)) + b_a
    return _log_sigmoid(z) * (1.0 / GLA_TAU)


def _mod_kernel(ct_ref, w_ref, b_ref, o_ref):
    s = _silu(ct_ref[...])
    w = w_ref[...]
    for j in range(2):
        o_ref[j:j + 1, :] = jnp.sum(w * s[:, j:j + 1], axis=0, keepdims=True) + b_ref[...]


def _mod_call(ct, w_mod, b_mod):
    tn = 1536
    n = w_mod.shape[1]
    return pl.pallas_call(
        _mod_kernel,
        grid=(n // tn,),
        in_specs=[pl.BlockSpec((D_MODEL, 2), lambda j: (0, 0)),
                  pl.BlockSpec((D_MODEL, tn), lambda j: (0, j)),
                  pl.BlockSpec((1, tn), lambda j: (0, j))],
        out_specs=pl.BlockSpec((2, tn), lambda j: (0, j)),
        out_shape=jax.ShapeDtypeStruct((2, n), F32),
        compiler_params=pltpu.CompilerParams(dimension_semantics=("arbitrary",),
                                             vmem_limit_bytes=VMEM_LIMIT),
        name="adaln_mod",
    )(ct, w_mod, b_mod)


def _ctx_kernel(ctx_ref, mod_ref, n1g_ref, w_in_ref, wa2f_ref, baf_ref, wa2b_ref, bab_ref,
                sf_ref, sb_ref):
    n_ctx = ctx_ref.shape[0]
    sh = mod_ref[1:2, 0:D_MODEL]
    sc = mod_ref[1:2, D_MODEL:2 * D_MODEL]
    h = (_rms_rows(ctx_ref[...], n1g_ref[...]) * (1.0 + sc) + sh).astype(BF16)
    kv = _dot(h, w_in_ref[:, C_K:C_G])
    r = _dot(h, w_in_ref[:, C_R:D_IN])
    k = kv[:, :D_GLA_K]
    v_bf = kv[:, D_GLA_K:].astype(BF16)
    la_f = _log_decay(r[:, :GLA_RANK], wa2f_ref[...], baf_ref[...])
    la_b = _log_decay(r[:, GLA_RANK:], wa2b_ref[...], bab_ref[...])
    n_chunks = n_ctx // CHUNK
    for forward, la, out_ref in ((True, la_f, sf_ref), (False, la_b, sb_ref)):
        consts = _gla_consts(forward)
        s = jnp.zeros((N_HEADS * HEAD_K, HEAD_V), F32)
        order = range(n_chunks) if forward else range(n_chunks - 1, -1, -1)
        for ci in order:
            rows = slice(ci * CHUNK, (ci + 1) * CHUNK)
            _, s = _gla_chunk(None, k[rows], v_bf[rows], la[rows], s, consts, forward, False)
        out_ref[...] = s


def _ctx_call(ctx2d, mod, n1g, w_in_bf, wa2f, baf, wa2b, bab):
    st = jax.ShapeDtypeStruct((N_HEADS * HEAD_K, HEAD_V), F32)
    return pl.pallas_call(
        _ctx_kernel,
        out_shape=(st, st),
        compiler_params=pltpu.CompilerParams(vmem_limit_bytes=VMEM_LIMIT),
        name="context_states",
    )(ctx2d, mod, n1g, w_in_bf, wa2f, baf, wa2b, bab)


def _pass1_kernel(x_ref, mod_ref, n1g_ref, w_in_ref, cw_ref, cb_ref, lng_ref, lnb_ref,
                  wa2f_ref, baf_ref, wa2b_ref, bab_ref, sb0_ref,
                  qk_ref, laf_ref, v_ref, g_ref, convo_ref, ob_ref,
                  s_ref, cbuf_ref):
    tile = x_ref.shape[0]
    n_seg = tile // CHUNK

    @pl.when(pl.program_id(0) == 0)
    def _():
        s_ref[...] = sb0_ref[...]
        cbuf_ref[...] = jnp.zeros(cbuf_ref.shape, F32)

    sh1 = mod_ref[0:1, 0:D_MODEL]
    sc1 = mod_ref[0:1, D_MODEL:2 * D_MODEL]
    h = (_rms_rows(x_ref[...], n1g_ref[...]) * (1.0 + sc1) + sh1).astype(BF16)

    u = _dot(h, w_in_ref[:, 0:C_Q])
    glu = u[:, :D_CONV] * _sigmoid(u[:, D_CONV:])
    n_lt = D_CONV // LANES
    for j in range(n_lt):
        for s in range(n_seg):
            base = s * SEG_PITCH + SEG_LEAD
            cbuf_ref[j, base:base + CHUNK, :] = glu[s * CHUNK:(s + 1) * CHUNK, j * LANES:(j + 1) * LANES]
    conv_rows = []
    for s in range(n_seg):
        cols = []
        for j in range(n_lt):
            acc = jnp.zeros((CHUNK, LANES), F32)
            for t in range(CONV_WIDTH):
                start = s * SEG_PITCH + SEG_LEAD - CONV_PAD + t
                acc = acc + cbuf_ref[j, start:start + CHUNK, :] * cw_ref[t:t + 1, j * LANES:(j + 1) * LANES]
            cols.append(acc)
        conv_rows.append(jnp.concatenate(cols, axis=1))
    y = jnp.concatenate(conv_rows, axis=0) + cb_ref[...]
    mu = jnp.mean(y, axis=-1, keepdims=True)
    yc = y - mu
    var = jnp.mean(yc * yc, axis=-1, keepdims=True)
    convo_ref[...] = _silu(yc * lax.rsqrt(var + EPS) * lng_ref[...] + lnb_ref[...]).astype(BF16)

    qkv = _dot(h, w_in_ref[:, C_Q:C_G])
    gr = _dot(h, w_in_ref[:, C_G:D_IN])
    q = qkv[:, 0:D_GLA_K]
    k = qkv[:, D_GLA_K:2 * D_GLA_K]
    v_bf = qkv[:, 2 * D_GLA_K:].astype(BF16)
    qk_ref[...] = qkv[:, 0:2 * D_GLA_K]
    v_ref[...] = v_bf
    g_ref[...] = gr[:, 0:D_GLA_V].astype(BF16)
    r = gr[:, D_GLA_V:]
    laf_ref[...] = _log_decay(r[:, :GLA_RANK], wa2f_ref[...], baf_ref[...])
    la_b = _log_decay(r[:, GLA_RANK:], wa2b_ref[...], bab_ref[...])

    consts = _gla_consts(False)
    s_all = s_ref[...]
    for ci in range(n_seg - 1, -1, -1):
        rows = slice(ci * CHUNK, (ci + 1) * CHUNK)
        o, s_all = _gla_chunk(q[rows], k[rows], v_bf[rows], la_b[rows], s_all, consts, False, True)
        ob_ref[rows, :] = o.astype(BF16)
    s_ref[...] = s_all


def _pass1_call(x2d, mod, n1g, w_in_bf, cw, cb, lng, lnb, wa2f, baf, wa2b, bab, sb0):
    n_tok = x2d.shape[0]
    n_tiles = n_tok // TILE
    rev = lambda i: (n_tiles - 1 - i, 0)
    const = lambda i: (0, 0)
    full = lambda a: pl.BlockSpec(a.shape, const)
    tile_spec = lambda w: pl.BlockSpec((TILE, w), rev)
    out_shape = (
        jax.ShapeDtypeStruct((n_tok, 2 * D_GLA_K), F32),
        jax.ShapeDtypeStruct((n_tok, D_GLA_K), F32),
        jax.ShapeDtypeStruct((n_tok, D_GLA_V), BF16),
        jax.ShapeDtypeStruct((n_tok, D_GLA_V), BF16),
        jax.ShapeDtypeStruct((n_tok, D_CONV), BF16),
        jax.ShapeDtypeStruct((n_tok, D_GLA_V), BF16),
    )
    return pl.pallas_call(
        _pass1_kernel,
        grid=(n_tiles,),
        in_specs=[tile_spec(D_MODEL), full(mod), full(n1g), full(w_in_bf), full(cw), full(cb),
                  full(lng), full(lnb), full(wa2f), full(baf), full(wa2b), full(bab), full(sb0)],
        out_specs=(tile_spec(2 * D_GLA_K), tile_spec(D_GLA_K), tile_spec(D_GLA_V),
                   tile_spec(D_GLA_V), tile_spec(D_CONV), tile_spec(D_GLA_V)),
        out_shape=out_shape,
        scratch_shapes=[pltpu.VMEM((N_HEADS * HEAD_K, HEAD_V), F32),
                        pltpu.VMEM((D_CONV // LANES, (TILE // CHUNK) * SEG_PITCH, LANES), F32)],
        compiler_params=pltpu.CompilerParams(dimension_semantics=("arbitrary",),
                                             vmem_limit_bytes=VMEM_LIMIT),
        name="pass1_project_conv_gla_bwd",
    )(x2d, mod, n1g, w_in_bf, cw, cb, lng, lnb, wa2f, baf, wa2b, bab, sb0)


def _pass2_kernel(x_ref, qk_ref, laf_ref, v_ref, g_ref, convo_ref, ob_ref, mod_ref, n2g_ref,
                  gng_ref, fg_ref, w_out_ref, wg_ref, wu_ref, wd_ref, sf0_ref,
                  out_ref, s_ref, mix_ref):
    tile = x_ref.shape[0]
    n_seg = tile // CHUNK

    @pl.when(pl.program_id(0) == 0)
    def _():
        s_ref[...] = sf0_ref[...]

    consts = _gla_consts(True)
    s_all = s_ref[...]
    gng = gng_ref[...]
    for ci in range(n_seg):
        rows = slice(ci * CHUNK, (ci + 1) * CHUNK)
        qk = qk_ref[rows, :]
        o, s_all = _gla_chunk(qk[:, :D_GLA_K], qk[:, D_GLA_K:], v_ref[rows, :], laf_ref[rows, :],
                              s_all, consts, True, True)
        o = o + ob_ref[rows, :].astype(F32)
        gate = _silu(g_ref[rows, :].astype(F32))
        for h in range(N_HEADS):
            cols = slice(h * HEAD_V, (h + 1) * HEAD_V)
            mix_ref[rows, D_CONV + h * HEAD_V:D_CONV + (h + 1) * HEAD_V] = (
                _rms_rows(o[:, cols], gng) * gate[:, cols]).astype(BF16)
    s_ref[...] = s_all
    mix_ref[:, 0:D_CONV] = convo_ref[...]

    g1 = mod_ref[0:1, 2 * D_MODEL:3 * D_MODEL]
    sh2 = mod_ref[0:1, 3 * D_MODEL:4 * D_MODEL]
    sc2 = mod_ref[0:1, 4 * D_MODEL:5 * D_MODEL]
    g2 = mod_ref[0:1, 5 * D_MODEL:6 * D_MODEL]

    x1 = x_ref[...] + g1 * _dot(mix_ref[...], w_out_ref[...])
    h2 = (_rms_rows(x1, n2g_ref[...]) * (1.0 + sc2) + sh2).astype(BF16)
    act = (_silu(_dot(h2, wg_ref[...])) * _dot(h2, wu_ref[...])).astype(BF16)
    x2 = x1 + g2 * _dot(act, wd_ref[...])
    out_ref[...] = _rms_rows(x2, fg_ref[...])


def _pass2_call(x2d, qk, laf, v, g, convo, ob, mod, n2g, gng, fg, w_out_bf, wg_bf, wu_bf, wd_bf, sf0):
    n_tok = x2d.shape[0]
    n_tiles = n_tok // TILE
    fwd = lambda i: (i, 0)
    const = lambda i: (0, 0)
    full = lambda a: pl.BlockSpec(a.shape, const)
    tile_spec = lambda w: pl.BlockSpec((TILE, w), fwd)
    return pl.pallas_call(
        _pass2_kernel,
        grid=(n_tiles,),
        in_specs=[tile_spec(D_MODEL), tile_spec(2 * D_GLA_K), tile_spec(D_GLA_K), tile_spec(D_GLA_V),
                  tile_spec(D_GLA_V), tile_spec(D_CONV), tile_spec(D_GLA_V),
                  full(mod), full(n2g), full(gng), full(fg),
                  full(w_out_bf), full(wg_bf), full(wu_bf), full(wd_bf), full(sf0)],
        out_specs=tile_spec(D_MODEL),
        out_shape=jax.ShapeDtypeStruct((n_tok, D_MODEL), F32),
        scratch_shapes=[pltpu.VMEM((N_HEADS * HEAD_K, HEAD_V), F32),
                        pltpu.VMEM((TILE, D_CONV + D_GLA_V), BF16)],
        compiler_params=pltpu.CompilerParams(dimension_semantics=("arbitrary",),
                                             vmem_limit_bytes=VMEM_LIMIT),
        name="pass2_gla_fwd_merge_ffn",
    )(x2d, qk, laf, v, g, convo, ob, mod, n2g, gng, fg, w_out_bf, wg_bf, wu_bf, wd_bf, sf0)


def kernel(x, c, ctx, c_ctx, w_mod, b_mod, norm1_g, norm2_g, w_in, conv_w, conv_b, conv_ln_g,
           conv_ln_b, w_a2_f, b_a_f, w_a2_b, b_a_b, gla_norm_g, w_out, w_gate, w_up, w_down, final_g):
    bsz, n_lat, d = x.shape
    assert bsz == 1 and d == D_MODEL and n_lat % TILE == 0 and ctx.shape[1] % CHUNK == 0
    assert w_mod.shape[0] == 1, "single layer"
    row = lambda a: a.reshape(1, -1)

    ct = jnp.concatenate([c.reshape(D_MODEL, 1), c_ctx.reshape(D_MODEL, 1)], axis=1)
    mod = _mod_call(ct, w_mod[0], row(b_mod[0]))

    w_in_bf = w_in[0].astype(BF16)
    wa2f, baf = w_a2_f[0], row(b_a_f[0])
    wa2b, bab = w_a2_b[0], row(b_a_b[0])
    n1g = row(norm1_g[0])

    sf0, sb0 = _ctx_call(ctx[0], mod, n1g, w_in_bf, wa2f, baf, wa2b, bab)

    x2d = x[0]
    qk, laf, v, g, convo, ob = _pass1_call(
        x2d, mod, n1g, w_in_bf, conv_w[0], row(conv_b[0]), row(conv_ln_g[0]), row(conv_ln_b[0]),
        wa2f, baf, wa2b, bab, sb0)

    out = _pass2_call(
        x2d, qk, laf, v, g, convo, ob, mod, row(norm2_g[0]), row(gla_norm_g[0]), row(final_g),
        w_out[0].astype(BF16), w_gate[0].astype(BF16), w_up[0].astype(BF16), w_down[0].astype(BF16), sf0)
    return out.reshape(bsz, n_lat, d)
```

```python
import functools

import jax
import jax.numpy as jnp
from jax import lax
from jax.experimental import pallas as pl
from jax.experimental.pallas import tpu as pltpu

D_MODEL = 1024
D_CONV = 512
CONV_WIDTH = 31
CONV_PAD = 15
N_HEADS = 4
HEAD_K = 64
HEAD_V = 128
D_GLA_K = N_HEADS * HEAD_K
D_GLA_V = N_HEADS * HEAD_V
GLA_RANK = 16
GLA_TAU = 16.0
CHUNK = 64
EPS = 1e-6
D_FF = 2816

C_Q = 2 * D_CONV
C_K = C_Q + D_GLA_K
C_V = C_K + D_GLA_K
C_G = C_V + D_GLA_V
C_R = C_G + D_GLA_V
D_IN = C_R + 2 * GLA_RANK

LANES = 128
MXU_N = 256
SEG_PITCH = 96
SEG_LEAD = 16
TILE = 256
VMEM_LIMIT = 56 * 1024 * 1024

F32 = jnp.float32
BF16 = jnp.bfloat16


def _dot(a, b):
    return jnp.dot(a, b, preferred_element_type=F32)


def _dot_nt(a, b):
    return lax.dot_general(a, b, (((1,), (1,)), ((), ())), preferred_element_type=F32)


def _sigmoid(x):
    return 1.0 / (1.0 + jnp.exp(-x))


def _silu(x):
    return x * _sigmoid(x)


def _log_sigmoid(z):
    return jnp.minimum(z, 0.0) - jnp.log(1.0 + jnp.exp(-jnp.abs(z)))


def _rms_rows(x, g):
    ms = jnp.mean(x * x, axis=-1, keepdims=True)
    return x * lax.rsqrt(ms + EPS) * g


def _interleave(main, side):
    n_main, n_side = len(main), len(side)
    done = 0
    for i, thunk in enumerate(main):
        thunk()
        want = ((i + 1) * n_side) // n_main
        while done < want:
            side[done]()
            done += 1


def _gla_consts(forward):
    r = lax.broadcasted_iota(jnp.int32, (CHUNK, CHUNK), 0)
    c = lax.broadcasted_iota(jnp.int32, (CHUNK, CHUNK), 1)
    tri = (c <= r) if forward else (c >= r)
    tri_bf = jnp.where(tri, 1.0, 0.0).astype(BF16)
    rr = lax.broadcasted_iota(jnp.int32, (N_HEADS * CHUNK, LANES), 0) % CHUNK
    cc = lax.broadcasted_iota(jnp.int32, (N_HEADS * CHUNK, LANES), 1)
    keep = ((cc <= rr) if forward else (cc >= rr)) & (cc < CHUNK)
    lane = lax.broadcasted_iota(jnp.int32, (CHUNK, D_GLA_K), 1)
    head_masks = [(lane >= h * HEAD_K) & (lane < (h + 1) * HEAD_K) for h in range(N_HEADS)]
    return tri_bf, keep, head_masks


def _gla_cumsum(la, consts):
    tri_bf = consts[0]
    la_hi = la.astype(BF16)
    la_lo = (la - la_hi.astype(F32)).astype(BF16)
    return _dot(tri_bf, la_hi) + _dot(tri_bf, la_lo)


def _gla_prep(q, k, v_bf, b, consts, forward, with_output):
    _, keep, head_masks = consts
    b_last = b[CHUNK - 1:CHUNK, :] if forward else b[0:1, :]
    k_w = k * jnp.exp(b_last - b)
    e_last = jnp.exp(b_last)

    tb = jnp.concatenate(
        [k_w, jnp.broadcast_to(e_last, (8, D_GLA_K)), jnp.zeros((CHUNK - 8, D_GLA_K), F32)], axis=0)
    tbt = tb.T
    dec_col = tbt[:, CHUNK:CHUNK + 1]
    kwt_bf = tbt.astype(BF16)
    zeros_v = jnp.zeros((CHUNK, HEAD_V), BF16)
    v_pads = [jnp.concatenate([v_bf[:, h * HEAD_V:(h + 1) * HEAD_V], zeros_v], axis=0)
              for h in range(N_HEADS)]
    kv = jnp.concatenate([_dot(kwt_bf[h * CHUNK:(h + 1) * CHUNK], v_pads[h])
                          for h in range(N_HEADS)], axis=0)
    if not with_output:
        return dec_col, kv, None
    q_t = (q * jnp.exp(b) * (HEAD_K ** -0.5)).astype(BF16)
    k_t = (k * jnp.exp(-b)).astype(BF16)
    qm = jnp.concatenate([jnp.where(m, q_t, jnp.zeros_like(q_t)) for m in head_masks], axis=0)
    kt_pad = jnp.concatenate([k_t, jnp.zeros_like(k_t)], axis=0)
    attn = _dot_nt(qm, kt_pad)
    p_bf = jnp.where(keep, attn, 0.0).astype(BF16)
    return dec_col, kv, (qm, p_bf, v_pads)


def _gla_out(ops, s_enter):
    qm, p_bf, v_pads = ops
    o_inter = _dot(qm, s_enter.astype(BF16))
    outs = []
    for h in range(N_HEADS):
        rows = slice(h * CHUNK, (h + 1) * CHUNK)
        outs.append(_dot(p_bf[rows], v_pads[h]) + o_inter[rows])
    return jnp.concatenate(outs, axis=1)


def _log_decay(r, w_a2, b_a):
    z = _dot(r.astype(BF16), w_a2.astype(BF16)) + b_a
    return _log_sigmoid(z) * (1.0 / GLA_TAU)


def _chunk_rows(ci):
    return slice(ci * CHUNK, (ci + 1) * CHUNK)


def _mod_kernel(ct_ref, w_ref, b_ref, o_ref):
    s = _silu(ct_ref[...])
    w = w_ref[...]
    for j in range(2):
        o_ref[j:j + 1, :] = jnp.sum(w * s[:, j:j + 1], axis=0, keepdims=True) + b_ref[...]


def _mod_call(ct, w_mod, b_mod):
    tn = 1536
    n = w_mod.shape[1]
    return pl.pallas_call(
        _mod_kernel,
        grid=(n // tn,),
        in_specs=[pl.BlockSpec((D_MODEL, 2), lambda j: (0, 0)),
                  pl.BlockSpec((D_MODEL, tn), lambda j: (0, j)),
                  pl.BlockSpec((1, tn), lambda j: (0, j))],
        out_specs=pl.BlockSpec((2, tn), lambda j: (0, j)),
        out_shape=jax.ShapeDtypeStruct((2, n), F32),
        compiler_params=pltpu.CompilerParams(dimension_semantics=("arbitrary",),
                                             vmem_limit_bytes=VMEM_LIMIT),
        name="adaln_mod",
    )(ct, w_mod, b_mod)


def _ctx_kernel(ctx_ref, mod_ref, n1g_ref, w_in_ref, wa2f_ref, baf_ref, wa2b_ref, bab_ref,
                sf_ref, sb_ref):
    n_chunks = ctx_ref.shape[0] // CHUNK
    sh = mod_ref[1:2, 0:D_MODEL]
    sc = mod_ref[1:2, D_MODEL:2 * D_MODEL]
    h = (_rms_rows(ctx_ref[...], n1g_ref[...]) * (1.0 + sc) + sh).astype(BF16)
    kv = _dot(h, w_in_ref[:, C_K:C_G])
    r = _dot(h, w_in_ref[:, C_R:D_IN])
    k = kv[:, :D_GLA_K]
    v_bf = kv[:, D_GLA_K:].astype(BF16)
    la_f = _log_decay(r[:, :GLA_RANK], wa2f_ref[...], baf_ref[...])
    la_b = _log_decay(r[:, GLA_RANK:], wa2b_ref[...], bab_ref[...])
    for forward, la, out_ref in ((True, la_f, sf_ref), (False, la_b, sb_ref)):
        consts = _gla_consts(forward)
        order = list(range(n_chunks)) if forward else list(range(n_chunks - 1, -1, -1))
        b = {ci: _gla_cumsum(la[_chunk_rows(ci)], consts) for ci in order}
        s = jnp.zeros((N_HEADS * HEAD_K, HEAD_V), F32)
        for ci in order:
            rows = _chunk_rows(ci)
            dec_col, kv_c, _ = _gla_prep(None, k[rows], v_bf[rows], b[ci], consts, forward, False)
            s = dec_col * s + kv_c
        out_ref[...] = s


def _ctx_call(ctx2d, mod, n1g, w_in_bf, wa2f, baf, wa2b, bab):
    st = jax.ShapeDtypeStruct((N_HEADS * HEAD_K, HEAD_V), F32)
    return pl.pallas_call(
        _ctx_kernel,
        out_shape=(st, st),
        compiler_params=pltpu.CompilerParams(vmem_limit_bytes=VMEM_LIMIT),
        name="context_states",
    )(ctx2d, mod, n1g, w_in_bf, wa2f, baf, wa2b, bab)


def _pass1_step(x_ref, mod_ref, n1g_ref, w_in_ref, wa2f_ref, baf_ref, wa2b_ref, bab_ref,
                qk_ref, laf_ref, v_ref, g_ref, glu_ref, ob_ref, s_ref, carry_w, carry_r):
    n_seg = x_ref.shape[0] // CHUNK
    qk_w, v_w, lab_w = carry_w
    qk_r, v_r, lab_r = carry_r
    order = list(range(n_seg - 1, -1, -1))
    consts = _gla_consts(False)

    b = {ci: _gla_cumsum(lab_r[_chunk_rows(ci), :], consts) for ci in order}

    sh1 = mod_ref[0:1, 0:D_MODEL]
    sc1 = mod_ref[0:1, D_MODEL:2 * D_MODEL]
    h = (_rms_rows(x_ref[...], n1g_ref[...]) * (1.0 + sc1) + sh1).astype(BF16)

    def proj(lo, hi):
        return _dot(h, w_in_ref[:, lo:hi])

    def glu_block(j):
        lo, hi = j * MXU_N, (j + 1) * MXU_N
        glu_ref[:, lo:hi] = proj(lo, hi) * _sigmoid(proj(D_CONV + lo, D_CONV + hi))

    def qk_block(j):
        lo, hi = j * MXU_N, (j + 1) * MXU_N
        blk = proj(C_Q + lo, C_Q + hi)
        qk_ref[:, lo:hi] = blk
        qk_w[:, lo:hi] = blk

    def v_block(j):
        lo, hi = j * MXU_N, (j + 1) * MXU_N
        blk = proj(C_V + lo, C_V + hi).astype(BF16)
        v_ref[:, lo:hi] = blk
        v_w[:, lo:hi] = blk

    def g_block(j):
        lo, hi = j * MXU_N, (j + 1) * MXU_N
        g_ref[:, lo:hi] = proj(C_G + lo, C_G + hi).astype(BF16)

    def decay_block():
        r = proj(C_R, D_IN)
        laf_ref[...] = _log_decay(r[:, :GLA_RANK], wa2f_ref[...], baf_ref[...])
        lab_w[...] = _log_decay(r[:, GLA_RANK:], wa2b_ref[...], bab_ref[...])

    proj_blocks = ([functools.partial(glu_block, j) for j in range(D_CONV // MXU_N)]
                   + [functools.partial(qk_block, j) for j in range(2 * D_GLA_K // MXU_N)]
                   + [functools.partial(v_block, j) for j in range(D_GLA_V // MXU_N)]
                   + [functools.partial(g_block, j) for j in range(D_GLA_V // MXU_N)]
                   + [decay_block])

    state = {"s": s_ref[...]}
    prep, s_enter = {}, {}

    def gla_prep(ci):
        rows = _chunk_rows(ci)
        qk_c = qk_r[rows, :]
        dec_col, kv, ops = _gla_prep(qk_c[:, :D_GLA_K], qk_c[:, D_GLA_K:], v_r[rows, :], b[ci],
                                     consts, False, True)
        prep[ci] = ops
        s_enter[ci] = state["s"]
        state["s"] = dec_col * state["s"] + kv

    def gla_out(ci):
        ob_ref[_chunk_rows(ci), :] = _gla_out(prep[ci], s_enter[ci]).astype(BF16)

    gla_stages = ([functools.partial(gla_prep, ci) for ci in order]
                  + [functools.partial(gla_out, ci) for ci in order])
    _interleave(proj_blocks, gla_stages)
    s_ref[...] = state["s"]


def _pass1_kernel(x_ref, mod_ref, n1g_ref, w_in_ref, wa2f_ref, baf_ref, wa2b_ref, bab_ref, sb0_ref,
                  qk_ref, laf_ref, v_ref, g_ref, glu_ref, ob_ref,
                  s_ref, qk_a, v_a, lab_a, qk_b, v_b, lab_b):
    step = pl.program_id(0)
    buf_a = (qk_a, v_a, lab_a)
    buf_b = (qk_b, v_b, lab_b)

    @pl.when(step == 0)
    def _():
        s_ref[...] = jnp.zeros(s_ref.shape, F32)
        for ref in buf_b:
            ref[...] = jnp.zeros(ref.shape, ref.dtype)

    run = functools.partial(_pass1_step, x_ref, mod_ref, n1g_ref, w_in_ref, wa2f_ref, baf_ref,
                            wa2b_ref, bab_ref, qk_ref, laf_ref, v_ref, g_ref, glu_ref, ob_ref, s_ref)

    @pl.when(step % 2 == 0)
    def _():
        run(buf_a, buf_b)

    @pl.when(step % 2 == 1)
    def _():
        run(buf_b, buf_a)

    @pl.when(step == 0)
    def _():
        s_ref[...] = sb0_ref[...]


def _pass1_call(x2d, mod, n1g, w_in_bf, wa2f, baf, wa2b, bab, sb0):
    n_tok = x2d.shape[0]
    n_tiles = n_tok // TILE
    proj = lambda s: (n_tiles - 1 - jnp.minimum(s, n_tiles - 1), 0)
    gla = lambda s: (n_tiles - 1 - jnp.maximum(s - 1, 0), 0)
    const = lambda s: (0, 0)
    full = lambda a: pl.BlockSpec(a.shape, const)
    out_shape = (
        jax.ShapeDtypeStruct((n_tok, 2 * D_GLA_K), F32),
        jax.ShapeDtypeStruct((n_tok, D_GLA_K), F32),
        jax.ShapeDtypeStruct((n_tok, D_GLA_V), BF16),
        jax.ShapeDtypeStruct((n_tok, D_GLA_V), BF16),
        jax.ShapeDtypeStruct((n_tok, D_CONV), F32),
        jax.ShapeDtypeStruct((n_tok, D_GLA_V), BF16),
    )
    carry = [pltpu.VMEM((TILE, 2 * D_GLA_K), F32), pltpu.VMEM((TILE, D_GLA_V), BF16),
             pltpu.VMEM((TILE, D_GLA_K), F32)]
    return pl.pallas_call(
        _pass1_kernel,
        grid=(n_tiles + 1,),
        in_specs=[pl.BlockSpec((TILE, D_MODEL), proj), full(mod), full(n1g), full(w_in_bf),
                  full(wa2f), full(baf), full(wa2b), full(bab), full(sb0)],
        out_specs=(pl.BlockSpec((TILE, 2 * D_GLA_K), proj), pl.BlockSpec((TILE, D_GLA_K), proj),
                   pl.BlockSpec((TILE, D_GLA_V), proj), pl.BlockSpec((TILE, D_GLA_V), proj),
                   pl.BlockSpec((TILE, D_CONV), proj), pl.BlockSpec((TILE, D_GLA_V), gla)),
        out_shape=out_shape,
        scratch_shapes=[pltpu.VMEM((N_HEADS * HEAD_K, HEAD_V), F32)] + carry + carry,
        compiler_params=pltpu.CompilerParams(dimension_semantics=("arbitrary",),
                                             vmem_limit_bytes=VMEM_LIMIT),
        name="pass1_project_gla_bwd",
    )(x2d, mod, n1g, w_in_bf, wa2f, baf, wa2b, bab, sb0)


def _conv_stage_in(glu_ref, cbuf_ref):
    n_seg = glu_ref.shape[0] // CHUNK
    for j in range(D_CONV // LANES):
        for s in range(n_seg):
            base = s * SEG_PITCH + SEG_LEAD
            cbuf_ref[j, base:base + CHUNK, :] = glu_ref[_chunk_rows(s), j * LANES:(j + 1) * LANES]


def _conv_taps(s, j, cbuf_ref, cw_ref, acc):
    for t in range(CONV_WIDTH):
        start = s * SEG_PITCH + SEG_LEAD - CONV_PAD + t
        acc = acc + cbuf_ref[j, start:start + CHUNK, :] * cw_ref[t:t + 1, j * LANES:(j + 1) * LANES]
    return acc


def _conv_norm(cols, cb_ref, lng_ref, lnb_ref):
    y = jnp.concatenate(cols, axis=1) + cb_ref[...]
    mu = jnp.mean(y, axis=-1, keepdims=True)
    yc = y - mu
    var = jnp.mean(yc * yc, axis=-1, keepdims=True)
    return _silu(yc * lax.rsqrt(var + EPS) * lng_ref[...] + lnb_ref[...])


def _pass2_kernel(x_ref, qk_ref, laf_ref, v_ref, g_ref, glu_ref, ob_ref, mod_ref, n2g_ref,
                  gng_ref, fg_ref, cw_ref, cb_ref, lng_ref, lnb_ref,
                  w_out_ref, wg_ref, wu_ref, wd_ref, sf0_ref, never_ref,
                  out_ref, s_ref, mix_ref, cbuf_ref):
    n_seg = x_ref.shape[0] // CHUNK
    n_ff = D_FF // MXU_N

    @pl.when(pl.program_id(0) == 0)
    def _():
        s_ref[...] = sf0_ref[...]
        mix_ref[...] = jnp.zeros(mix_ref.shape, BF16)
        cbuf_ref[...] = jnp.zeros(cbuf_ref.shape, F32)

    chunks = list(range(n_seg))
    consts = _gla_consts(True)
    gng = gng_ref[...]

    _conv_stage_in(glu_ref, cbuf_ref)
    b = {ci: _gla_cumsum(laf_ref[_chunk_rows(ci), :], consts) for ci in chunks}

    g1 = mod_ref[0:1, 2 * D_MODEL:3 * D_MODEL]
    sh2 = mod_ref[0:1, 3 * D_MODEL:4 * D_MODEL]
    sc2 = mod_ref[0:1, 4 * D_MODEL:5 * D_MODEL]
    g2 = mod_ref[0:1, 5 * D_MODEL:6 * D_MODEL]
    x1 = x_ref[...] + g1 * _dot(mix_ref[...], w_out_ref[...])

    state = {"s": s_ref[...]}
    prep, s_enter = {}, {}

    def gla_prep(ci):
        rows = _chunk_rows(ci)
        qk_c = qk_ref[rows, :]
        dec_col, kv, ops = _gla_prep(qk_c[:, :D_GLA_K], qk_c[:, D_GLA_K:], v_ref[rows, :], b[ci],
                                     consts, True, True)
        prep[ci] = ops
        s_enter[ci] = state["s"]
        state["s"] = dec_col * state["s"] + kv

    def gla_out(ci):
        rows = _chunk_rows(ci)
        o = _gla_out(prep[ci], s_enter[ci]) + ob_ref[rows, :].astype(F32)
        gate = _silu(g_ref[rows, :].astype(F32))
        for h in range(N_HEADS):
            cols = slice(h * HEAD_V, (h + 1) * HEAD_V)
            mix_ref[rows, D_CONV + h * HEAD_V:D_CONV + (h + 1) * HEAD_V] = (
                _rms_rows(o[:, cols], gng) * gate[:, cols]).astype(BF16)

    conv_cols = {}

    never = never_ref[...] != 0
    latest = {}

    def conv_taps(s, j):
        zero = jnp.where(never, latest["ffn"], 0.0)
        conv_cols[s, j] = _conv_taps(s, j, cbuf_ref, cw_ref, zero)

    def conv_out(s):
        cols = [conv_cols.pop((s, j)) for j in range(D_CONV // LANES)]
        mix_ref[_chunk_rows(s), 0:D_CONV] = _conv_norm(cols, cb_ref, lng_ref, lnb_ref).astype(BF16)

    side = [functools.partial(gla_prep, ci) for ci in chunks]
    for ci in chunks:
        side.append(functools.partial(gla_out, ci))
        side += [functools.partial(conv_taps, ci, j) for j in range(D_CONV // LANES)]
        side.append(functools.partial(conv_out, ci))

    n_early = min(2, n_seg)
    for thunk in side[:n_early]:
        thunk()

    h2 = (_rms_rows(x1, n2g_ref[...]) * (1.0 + sc2) + sh2).astype(BF16)
    act, acc = {}, {}

    def gate_up(j):
        cols = slice(j * MXU_N, (j + 1) * MXU_N)
        gate = _dot(h2, wg_ref[:, cols])
        latest["ffn"] = gate[0:CHUNK, 0:LANES]
        act[j] = (_silu(gate) * _dot(h2, wu_ref[:, cols])).astype(BF16)

    def down(j):
        part = _dot(act.pop(j), wd_ref[j * MXU_N:(j + 1) * MXU_N, :])
        acc["v"] = part if j == 0 else acc["v"] + part

    ffn = [functools.partial(gate_up, 0)]
    for j in range(1, n_ff):
        ffn += [functools.partial(gate_up, j), functools.partial(down, j - 1)]
    ffn.append(functools.partial(down, n_ff - 1))
    _interleave(ffn, side[n_early:])
    s_ref[...] = state["s"]

    x2 = x1 + g2 * acc["v"]
    out_ref[...] = _rms_rows(x2, fg_ref[...])


def _pass2_call(x2d, qk, laf, v, g, glu, ob, mod, n2g, gng, fg, cw, cb, lng, lnb,
                w_out_bf, wg_bf, wu_bf, wd_bf, sf0, never):
    n_tok = x2d.shape[0]
    n_tiles = n_tok // TILE
    mixer = lambda s: (jnp.minimum(s, n_tiles - 1), 0)
    ffn = lambda s: (jnp.maximum(s - 1, 0), 0)
    const = lambda s: (0, 0)
    full = lambda a: pl.BlockSpec(a.shape, const)
    mixer_spec = lambda w: pl.BlockSpec((TILE, w), mixer)
    return pl.pallas_call(
        _pass2_kernel,
        grid=(n_tiles + 1,),
        in_specs=[pl.BlockSpec((TILE, D_MODEL), ffn), mixer_spec(2 * D_GLA_K), mixer_spec(D_GLA_K),
                  mixer_spec(D_GLA_V), mixer_spec(D_GLA_V), mixer_spec(D_CONV), mixer_spec(D_GLA_V),
                  full(mod), full(n2g), full(gng), full(fg), full(cw), full(cb), full(lng), full(lnb),
                  full(w_out_bf), full(wg_bf), full(wu_bf), full(wd_bf), full(sf0),
                  full(never)],
        out_specs=pl.BlockSpec((TILE, D_MODEL), ffn),
        out_shape=jax.ShapeDtypeStruct((n_tok, D_MODEL), F32),
        scratch_shapes=[pltpu.VMEM((N_HEADS * HEAD_K, HEAD_V), F32),
                        pltpu.VMEM((TILE, D_CONV + D_GLA_V), BF16),
                        pltpu.VMEM((D_CONV // LANES, (TILE // CHUNK) * SEG_PITCH, LANES), F32)],
        compiler_params=pltpu.CompilerParams(dimension_semantics=("arbitrary",),
                                             vmem_limit_bytes=VMEM_LIMIT),
        name="pass2_gla_fwd_conv_merge_ffn",
    )(x2d, qk, laf, v, g, glu, ob, mod, n2g, gng, fg, cw, cb, lng, lnb,
      w_out_bf, wg_bf, wu_bf, wd_bf, sf0, never)


def kernel(x, c, ctx, c_ctx, w_mod, b_mod, norm1_g, norm2_g, w_in, conv_w, conv_b, conv_ln_g,
           conv_ln_b, w_a2_f, b_a_f, w_a2_b, b_a_b, gla_norm_g, w_out, w_gate, w_up, w_down, final_g):
    bsz, n_lat, d = x.shape
    assert bsz == 1 and d == D_MODEL and n_lat % TILE == 0 and ctx.shape[1] % CHUNK == 0
    assert w_mod.shape[0] == 1, "single layer"
    row = lambda a: a.reshape(1, -1)

    ct = jnp.concatenate([c.reshape(D_MODEL, 1), c_ctx.reshape(D_MODEL, 1)], axis=1)
    mod = _mod_call(ct, w_mod[0], row(b_mod[0]))

    w_in_bf = w_in[0].astype(BF16)
    wa2f, baf = w_a2_f[0], row(b_a_f[0])
    wa2b, bab = w_a2_b[0], row(b_a_b[0])
    n1g = row(norm1_g[0])

    sf0, sb0 = _ctx_call(ctx[0], mod, n1g, w_in_bf, wa2f, baf, wa2b, bab)

    x2d = x[0]
    qk, laf, v, g, glu, ob = _pass1_call(x2d, mod, n1g, w_in_bf, wa2f, baf, wa2b, bab, sb0)

    out = _pass2_call(
        x2d, qk, laf, v, g, glu, ob, mod, row(norm2_g[0]), row(gla_norm_g[0]), row(final_g),
        conv_w[0], row(conv_b[0]), row(conv_ln_g[0]), row(conv_ln_b[0]),
        w_out[0].astype(BF16), w_gate[0].astype(BF16), w_up[0].astype(BF16), w_down[0].astype(BF16), sf0,
        jnp.zeros((CHUNK, LANES), jnp.int32))
    return out.reshape(bsz, n_lat, d)
```

```python
import functools

import jax
import jax.numpy as jnp
from jax import lax
from jax.experimental import pallas as pl
from jax.experimental.pallas import tpu as pltpu

D_MODEL = 1024
D_CONV = 512
CONV_WIDTH = 31
CONV_PAD = 15
N_HEADS = 4
HEAD_K = 64
HEAD_V = 128
D_GLA_K = N_HEADS * HEAD_K
D_GLA_V = N_HEADS * HEAD_V
GLA_RANK = 16
GLA_TAU = 16.0
CHUNK = 64
EPS = 1e-6
D_FF = 2816

C_Q = 2 * D_CONV
C_K = C_Q + D_GLA_K
C_V = C_K + D_GLA_K
C_G = C_V + D_GLA_V
C_R = C_G + D_GLA_V
D_IN = C_R + 2 * GLA_RANK

LANES = 128
MXU_N = 256
SEG_PITCH = 96
SEG_LEAD = 16
SUB = 256
TILE = 512
VMEM_LIMIT = 56 * 1024 * 1024

F32 = jnp.float32
BF16 = jnp.bfloat16


def _dot(a, b):
    return jnp.dot(a, b, preferred_element_type=F32)


def _dot_nt(a, b):
    return lax.dot_general(a, b, (((1,), (1,)), ((), ())), preferred_element_type=F32)


def _sigmoid(x):
    return 1.0 / (1.0 + jnp.exp(-x))


def _silu(x):
    return x * _sigmoid(x)


def _log_sigmoid(z):
    return jnp.minimum(z, 0.0) - jnp.log(1.0 + jnp.exp(-jnp.abs(z)))


def _rms_rows(x, g):
    ms = jnp.mean(x * x, axis=-1, keepdims=True)
    return x * lax.rsqrt(ms + EPS) * g


def _interleave(main, side):
    n_main, n_side = len(main), len(side)
    done = 0
    for i, thunk in enumerate(main):
        thunk()
        want = ((i + 1) * n_side) // n_main
        while done < want:
            side[done]()
            done += 1


def _gla_consts(forward):
    r = lax.broadcasted_iota(jnp.int32, (CHUNK, CHUNK), 0)
    c = lax.broadcasted_iota(jnp.int32, (CHUNK, CHUNK), 1)
    tri = (c <= r) if forward else (c >= r)
    tri_bf = jnp.where(tri, 1.0, 0.0).astype(BF16)
    rr = lax.broadcasted_iota(jnp.int32, (N_HEADS * CHUNK, LANES), 0) % CHUNK
    cc = lax.broadcasted_iota(jnp.int32, (N_HEADS * CHUNK, LANES), 1)
    keep = ((cc <= rr) if forward else (cc >= rr)) & (cc < CHUNK)
    lane = lax.broadcasted_iota(jnp.int32, (CHUNK, D_GLA_K), 1)
    head_masks = [(lane >= h * HEAD_K) & (lane < (h + 1) * HEAD_K) for h in range(N_HEADS)]
    return tri_bf, keep, head_masks


def _gla_cumsum(la, consts):
    tri_bf = consts[0]
    la_hi = la.astype(BF16)
    la_lo = (la - la_hi.astype(F32)).astype(BF16)
    return _dot(tri_bf, la_hi) + _dot(tri_bf, la_lo)


def _gla_prep(q, k, v_bf, b, consts, forward, with_output):
    _, keep, head_masks = consts
    b_last = b[CHUNK - 1:CHUNK, :] if forward else b[0:1, :]
    k_w = k * jnp.exp(b_last - b)
    e_last = jnp.exp(b_last)

    tb = jnp.concatenate(
        [k_w, jnp.broadcast_to(e_last, (8, D_GLA_K)), jnp.zeros((CHUNK - 8, D_GLA_K), F32)], axis=0)
    tbt = tb.T
    dec_col = tbt[:, CHUNK:CHUNK + 1]
    kwt_bf = tbt.astype(BF16)
    zeros_v = jnp.zeros((CHUNK, HEAD_V), BF16)
    v_pads = [jnp.concatenate([v_bf[:, h * HEAD_V:(h + 1) * HEAD_V], zeros_v], axis=0)
              for h in range(N_HEADS)]
    kv = jnp.concatenate([_dot(kwt_bf[h * CHUNK:(h + 1) * CHUNK], v_pads[h])
                          for h in range(N_HEADS)], axis=0)
    if not with_output:
        return dec_col, kv, None
    q_t = (q * jnp.exp(b) * (HEAD_K ** -0.5)).astype(BF16)
    k_t = (k * jnp.exp(-b)).astype(BF16)
    qm = jnp.concatenate([jnp.where(m, q_t, jnp.zeros_like(q_t)) for m in head_masks], axis=0)
    kt_pad = jnp.concatenate([k_t, jnp.zeros_like(k_t)], axis=0)
    attn = _dot_nt(qm, kt_pad)
    p_bf = jnp.where(keep, attn, 0.0).astype(BF16)
    return dec_col, kv, (qm, p_bf, v_pads)


def _gla_out(ops, s_enter):
    qm, p_bf, v_pads = ops
    o_inter = _dot(qm, s_enter.astype(BF16))
    outs = []
    for h in range(N_HEADS):
        rows = slice(h * CHUNK, (h + 1) * CHUNK)
        outs.append(_dot(p_bf[rows], v_pads[h]) + o_inter[rows])
    return jnp.concatenate(outs, axis=1)


def _log_decay(r, w_a2, b_a):
    z = _dot(r.astype(BF16), w_a2.astype(BF16)) + b_a
    return _log_sigmoid(z) * (1.0 / GLA_TAU)


def _chunk_rows(ci):
    return slice(ci * CHUNK, (ci + 1) * CHUNK)


def _mod_kernel(ct_ref, w_ref, b_ref, o_ref):
    s = _silu(ct_ref[...])
    w = w_ref[...]
    for j in range(2):
        o_ref[j:j + 1, :] = jnp.sum(w * s[:, j:j + 1], axis=0, keepdims=True) + b_ref[...]


def _mod_call(ct, w_mod, b_mod):
    tn = 1536
    n = w_mod.shape[1]
    return pl.pallas_call(
        _mod_kernel,
        grid=(n // tn,),
        in_specs=[pl.BlockSpec((D_MODEL, 2), lambda j: (0, 0)),
                  pl.BlockSpec((D_MODEL, tn), lambda j: (0, j)),
                  pl.BlockSpec((1, tn), lambda j: (0, j))],
        out_specs=pl.BlockSpec((2, tn), lambda j: (0, j)),
        out_shape=jax.ShapeDtypeStruct((2, n), F32),
        compiler_params=pltpu.CompilerParams(dimension_semantics=("arbitrary",),
                                             vmem_limit_bytes=VMEM_LIMIT),
        name="adaln_mod",
    )(ct, w_mod, b_mod)


def _ctx_kernel(ctx_ref, mod_ref, n1g_ref, w_in_f32_ref, wa2f_ref, baf_ref, wa2b_ref, bab_ref,
                sf_ref, sb_ref, w_in_ref):
    w_in_ref[...] = w_in_f32_ref[...].astype(BF16)
    n_chunks = ctx_ref.shape[0] // CHUNK
    sh = mod_ref[1:2, 0:D_MODEL]
    sc = mod_ref[1:2, D_MODEL:2 * D_MODEL]
    h = (_rms_rows(ctx_ref[...], n1g_ref[...]) * (1.0 + sc) + sh).astype(BF16)
    kv = _dot(h, w_in_ref[:, C_K:C_G])
    r = _dot(h, w_in_ref[:, C_R:D_IN])
    k = kv[:, :D_GLA_K]
    v_bf = kv[:, D_GLA_K:].astype(BF16)
    la_f = _log_decay(r[:, :GLA_RANK], wa2f_ref[...], baf_ref[...])
    la_b = _log_decay(r[:, GLA_RANK:], wa2b_ref[...], bab_ref[...])
    for forward, la, out_ref in ((True, la_f, sf_ref), (False, la_b, sb_ref)):
        consts = _gla_consts(forward)
        order = list(range(n_chunks)) if forward else list(range(n_chunks - 1, -1, -1))
        b = {ci: _gla_cumsum(la[_chunk_rows(ci)], consts) for ci in order}
        s = jnp.zeros((N_HEADS * HEAD_K, HEAD_V), F32)
        for ci in order:
            rows = _chunk_rows(ci)
            dec_col, kv_c, _ = _gla_prep(None, k[rows], v_bf[rows], b[ci], consts, forward, False)
            s = dec_col * s + kv_c
        out_ref[...] = s


def _ctx_call(ctx2d, mod, n1g, w_in, wa2f, baf, wa2b, bab):
    st = jax.ShapeDtypeStruct((N_HEADS * HEAD_K, HEAD_V), F32)
    return pl.pallas_call(
        _ctx_kernel,
        out_shape=(st, st, jax.ShapeDtypeStruct(w_in.shape, BF16)),
        compiler_params=pltpu.CompilerParams(vmem_limit_bytes=VMEM_LIMIT),
        name="context_states",
    )(ctx2d, mod, n1g, w_in, wa2f, baf, wa2b, bab)


def _pass1_sub(r0, state, x_ref, mod_ref, n1g_ref, w_in_ref, wa2f_ref, baf_ref, wa2b_ref, bab_ref,
               qk_ref, laf_ref, v_ref, g_ref, glu_ref, ob_ref, carry_w, carry_r):
    rows = slice(r0, r0 + SUB)
    qk_w, v_w, lab_w = carry_w
    qk_r, v_r, lab_r = carry_r
    first = r0 // CHUNK
    order = list(range(first + SUB // CHUNK - 1, first - 1, -1))
    consts = _gla_consts(False)

    b = {ci: _gla_cumsum(lab_r[_chunk_rows(ci), :], consts) for ci in order}

    sh1 = mod_ref[0:1, 0:D_MODEL]
    sc1 = mod_ref[0:1, D_MODEL:2 * D_MODEL]
    h = (_rms_rows(x_ref[rows, :], n1g_ref[...]) * (1.0 + sc1) + sh1).astype(BF16)

    def proj(lo, hi):
        return _dot(h, w_in_ref[:, lo:hi])

    def glu_block(j):
        lo, hi = j * MXU_N, (j + 1) * MXU_N
        glu_ref[rows, lo:hi] = proj(lo, hi) * _sigmoid(proj(D_CONV + lo, D_CONV + hi))

    def qk_block(j):
        lo, hi = j * MXU_N, (j + 1) * MXU_N
        blk = proj(C_Q + lo, C_Q + hi)
        qk_ref[rows, lo:hi] = blk
        qk_w[rows, lo:hi] = blk

    def v_block(j):
        lo, hi = j * MXU_N, (j + 1) * MXU_N
        blk = proj(C_V + lo, C_V + hi).astype(BF16)
        v_ref[rows, lo:hi] = blk
        v_w[rows, lo:hi] = blk

    def g_block(j):
        lo, hi = j * MXU_N, (j + 1) * MXU_N
        g_ref[rows, lo:hi] = proj(C_G + lo, C_G + hi).astype(BF16)

    def decay_block():
        r = proj(C_R, D_IN)
        laf_ref[rows, :] = _log_decay(r[:, :GLA_RANK], wa2f_ref[...], baf_ref[...])
        lab_w[rows, :] = _log_decay(r[:, GLA_RANK:], wa2b_ref[...], bab_ref[...])

    proj_blocks = ([functools.partial(glu_block, j) for j in range(D_CONV // MXU_N)]
                   + [functools.partial(qk_block, j) for j in range(2 * D_GLA_K // MXU_N)]
                   + [functools.partial(v_block, j) for j in range(D_GLA_V // MXU_N)]
                   + [functools.partial(g_block, j) for j in range(D_GLA_V // MXU_N)]
                   + [decay_block])

    prep, s_enter = {}, {}

    def gla_prep(ci):
        crows = _chunk_rows(ci)
        qk_c = qk_r[crows, :]
        dec_col, kv, ops = _gla_prep(qk_c[:, :D_GLA_K], qk_c[:, D_GLA_K:], v_r[crows, :], b[ci],
                                     consts, False, True)
        prep[ci] = ops
        s_enter[ci] = state["s"]
        state["s"] = dec_col * state["s"] + kv

    def gla_out(ci):
        ob_ref[_chunk_rows(ci), :] = _gla_out(prep[ci], s_enter[ci]).astype(BF16)

    gla_stages = ([functools.partial(gla_prep, ci) for ci in order]
                  + [functools.partial(gla_out, ci) for ci in order])
    _interleave(proj_blocks, gla_stages)


def _pass1_step(x_ref, mod_ref, n1g_ref, w_in_ref, wa2f_ref, baf_ref, wa2b_ref, bab_ref,
                qk_ref, laf_ref, v_ref, g_ref, glu_ref, ob_ref, s_ref, carry_w, carry_r):
    state = {"s": s_ref[...]}
    for r0 in range(x_ref.shape[0] - SUB, -1, -SUB):
        _pass1_sub(r0, state, x_ref, mod_ref, n1g_ref, w_in_ref, wa2f_ref, baf_ref, wa2b_ref, bab_ref,
                   qk_ref, laf_ref, v_ref, g_ref, glu_ref, ob_ref, carry_w, carry_r)
    s_ref[...] = state["s"]


def _pass1_kernel(x_ref, mod_ref, n1g_ref, w_in_ref, wa2f_ref, baf_ref, wa2b_ref, bab_ref, sb0_ref,
                  wo_f32_ref, wg_f32_ref, wu_f32_ref, wd_f32_ref,
                  qk_ref, laf_ref, v_ref, g_ref, glu_ref, ob_ref,
                  wo_bf_ref, wg_bf_ref, wu_bf_ref, wd_bf_ref,
                  s_ref, qk_a, v_a, lab_a, qk_b, v_b, lab_b):
    step = pl.program_id(0)

    for src, dst in ((wo_f32_ref, wo_bf_ref), (wg_f32_ref, wg_bf_ref), (wu_f32_ref, wu_bf_ref),
                     (wd_f32_ref, wd_bf_ref)):
        dst[...] = src[...].astype(BF16)
    buf_a = (qk_a, v_a, lab_a)
    buf_b = (qk_b, v_b, lab_b)

    @pl.when(step == 0)
    def _():
        s_ref[...] = jnp.zeros(s_ref.shape, F32)
        for ref in buf_b:
            ref[...] = jnp.zeros(ref.shape, ref.dtype)

    run = functools.partial(_pass1_step, x_ref, mod_ref, n1g_ref, w_in_ref, wa2f_ref, baf_ref,
                            wa2b_ref, bab_ref, qk_ref, laf_ref, v_ref, g_ref, glu_ref, ob_ref, s_ref)

    @pl.when(step % 2 == 0)
    def _():
        run(buf_a, buf_b)

    @pl.when(step % 2 == 1)
    def _():
        run(buf_b, buf_a)

    @pl.when(step == 0)
    def _():
        s_ref[...] = sb0_ref[...]


def _slabs(w, n_slabs):
    rows, cols = w.shape
    assert rows % (n_slabs * 16) == 0
    return w.reshape(n_slabs, rows // n_slabs, cols)


def _pass1_call(x2d, mod, n1g, w_in_bf, wa2f, baf, wa2b, bab, sb0, w_out, w_gate, w_up, w_down):
    n_tok = x2d.shape[0]
    n_tiles = n_tok // TILE
    cast_in = [_slabs(w, n) for w, n in ((w_out, 32), (w_gate, 32), (w_up, 32), (w_down, 22))]
    assert all(w.shape[0] <= n_tiles for w in cast_in)
    cast_specs = [pl.BlockSpec((1,) + w.shape[1:],
                               functools.partial(lambda n, s: (jnp.minimum(s, n - 1), 0, 0), w.shape[0]))
                  for w in cast_in]
    proj = lambda s: (n_tiles - 1 - jnp.minimum(s, n_tiles - 1), 0)
    gla = lambda s: (n_tiles - 1 - jnp.maximum(s - 1, 0), 0)
    const = lambda s: (0, 0)
    full = lambda a: pl.BlockSpec(a.shape, const)
    out_shape = (
        jax.ShapeDtypeStruct((n_tok, 2 * D_GLA_K), F32),
        jax.ShapeDtypeStruct((n_tok, D_GLA_K), F32),
        jax.ShapeDtypeStruct((n_tok, D_GLA_V), BF16),
        jax.ShapeDtypeStruct((n_tok, D_GLA_V), BF16),
        jax.ShapeDtypeStruct((n_tok, D_CONV), F32),
        jax.ShapeDtypeStruct((n_tok, D_GLA_V), BF16),
    )
    carry = [pltpu.VMEM((TILE, 2 * D_GLA_K), F32), pltpu.VMEM((TILE, D_GLA_V), BF16),
             pltpu.VMEM((TILE, D_GLA_K), F32)]
    return pl.pallas_call(
        _pass1_kernel,
        grid=(n_tiles + 1,),
        in_specs=[pl.BlockSpec((TILE, D_MODEL), proj), full(mod), full(n1g), full(w_in_bf),
                  full(wa2f), full(baf), full(wa2b), full(bab), full(sb0)] + cast_specs,
        out_specs=(pl.BlockSpec((TILE, 2 * D_GLA_K), proj), pl.BlockSpec((TILE, D_GLA_K), proj),
                   pl.BlockSpec((TILE, D_GLA_V), proj), pl.BlockSpec((TILE, D_GLA_V), proj),
                   pl.BlockSpec((TILE, D_CONV), proj), pl.BlockSpec((TILE, D_GLA_V), gla))
        + tuple(cast_specs),
        out_shape=out_shape + tuple(jax.ShapeDtypeStruct(w.shape, BF16) for w in cast_in),
        scratch_shapes=[pltpu.VMEM((N_HEADS * HEAD_K, HEAD_V), F32)] + carry + carry,
        compiler_params=pltpu.CompilerParams(dimension_semantics=("arbitrary",),
                                             vmem_limit_bytes=VMEM_LIMIT),
        name="pass1_project_gla_bwd",
    )(x2d, mod, n1g, w_in_bf, wa2f, baf, wa2b, bab, sb0, *cast_in)


def _conv_stage_in(glu_ref, cbuf_ref, segments):
    for j in range(D_CONV // LANES):
        for s in segments:
            base = s * SEG_PITCH + SEG_LEAD
            cbuf_ref[j, base:base + CHUNK, :] = glu_ref[_chunk_rows(s), j * LANES:(j + 1) * LANES]


def _conv_taps(s, j, cbuf_ref, cw_ref, acc):
    for t in range(CONV_WIDTH):
        start = s * SEG_PITCH + SEG_LEAD - CONV_PAD + t
        acc = acc + cbuf_ref[j, start:start + CHUNK, :] * cw_ref[t:t + 1, j * LANES:(j + 1) * LANES]
    return acc


def _conv_norm(cols, cb_ref, lng_ref, lnb_ref):
    y = jnp.concatenate(cols, axis=1) + cb_ref[...]
    mu = jnp.mean(y, axis=-1, keepdims=True)
    yc = y - mu
    var = jnp.mean(yc * yc, axis=-1, keepdims=True)
    return _silu(yc * lax.rsqrt(var + EPS) * lng_ref[...] + lnb_ref[...])


def _pass2_sub(r0, state, x_ref, qk_ref, laf_ref, v_ref, g_ref, glu_ref, ob_ref, mod_ref, n2g_ref,
               gng_ref, fg_ref, cw_ref, cb_ref, lng_ref, lnb_ref,
               w_out_ref, wg_ref, wu_ref, wd_ref, never, out_ref, mix_ref, cbuf_ref):
    rows = slice(r0, r0 + SUB)
    n_ff = D_FF // MXU_N
    first = r0 // CHUNK
    chunks = list(range(first, first + SUB // CHUNK))
    consts = _gla_consts(True)
    gng = gng_ref[...]

    _conv_stage_in(glu_ref, cbuf_ref, chunks)
    b = {ci: _gla_cumsum(laf_ref[_chunk_rows(ci), :], consts) for ci in chunks}

    g1 = mod_ref[0:1, 2 * D_MODEL:3 * D_MODEL]
    sh2 = mod_ref[0:1, 3 * D_MODEL:4 * D_MODEL]
    sc2 = mod_ref[0:1, 4 * D_MODEL:5 * D_MODEL]
    g2 = mod_ref[0:1, 5 * D_MODEL:6 * D_MODEL]
    x1 = x_ref[rows, :] + g1 * _dot(mix_ref[rows, :], w_out_ref[...])

    prep, s_enter = {}, {}

    def gla_prep(ci):
        crows = _chunk_rows(ci)
        qk_c = qk_ref[crows, :]
        dec_col, kv, ops = _gla_prep(qk_c[:, :D_GLA_K], qk_c[:, D_GLA_K:], v_ref[crows, :], b[ci],
                                     consts, True, True)
        prep[ci] = ops
        s_enter[ci] = state["s"]
        state["s"] = dec_col * state["s"] + kv

    def gla_out(ci):
        crows = _chunk_rows(ci)
        o = _gla_out(prep[ci], s_enter[ci]) + ob_ref[crows, :].astype(F32)
        gate = _silu(g_ref[crows, :].astype(F32))
        for h in range(N_HEADS):
            cols = slice(h * HEAD_V, (h + 1) * HEAD_V)
            mix_ref[crows, D_CONV + h * HEAD_V:D_CONV + (h + 1) * HEAD_V] = (
                _rms_rows(o[:, cols], gng) * gate[:, cols]).astype(BF16)

    conv_cols = {}
    latest = {}

    def conv_taps(s, j):
        zero = jnp.where(never, latest["ffn"], 0.0)
        conv_cols[s, j] = _conv_taps(s, j, cbuf_ref, cw_ref, zero)

    def conv_out(s):
        cols = [conv_cols.pop((s, j)) for j in range(D_CONV // LANES)]
        mix_ref[_chunk_rows(s), 0:D_CONV] = _conv_norm(cols, cb_ref, lng_ref, lnb_ref).astype(BF16)

    side = [functools.partial(gla_prep, ci) for ci in chunks]
    for ci in chunks:
        side.append(functools.partial(gla_out, ci))
        side += [functools.partial(conv_taps, ci, j) for j in range(D_CONV // LANES)]
        side.append(functools.partial(conv_out, ci))

    n_early = 2
    for thunk in side[:n_early]:
        thunk()

    h2 = (_rms_rows(x1, n2g_ref[...]) * (1.0 + sc2) + sh2).astype(BF16)
    act, acc = {}, {}

    def gate_up(j):
        cols = slice(j * MXU_N, (j + 1) * MXU_N)
        gate = _dot(h2, wg_ref[:, cols])
        latest["ffn"] = gate[0:CHUNK, 0:LANES]
        act[j] = (_silu(gate) * _dot(h2, wu_ref[:, cols])).astype(BF16)

    def down(j):
        part = _dot(act.pop(j), wd_ref[j * MXU_N:(j + 1) * MXU_N, :])
        acc["v"] = part if j == 0 else acc["v"] + part

    ffn = [functools.partial(gate_up, 0)]
    for j in range(1, n_ff):
        ffn += [functools.partial(gate_up, j), functools.partial(down, j - 1)]
    ffn.append(functools.partial(down, n_ff - 1))
    _interleave(ffn, side[n_early:])

    x2 = x1 + g2 * acc["v"]
    out_ref[rows, :] = _rms_rows(x2, fg_ref[...])


def _pass2_kernel(x_ref, qk_ref, laf_ref, v_ref, g_ref, glu_ref, ob_ref, mod_ref, n2g_ref,
                  gng_ref, fg_ref, cw_ref, cb_ref, lng_ref, lnb_ref,
                  w_out_ref, wg_ref, wu_ref, wd_ref, sf0_ref, never_ref,
                  out_ref, s_ref, mix_ref, cbuf_ref):
    @pl.when(pl.program_id(0) == 0)
    def _():
        s_ref[...] = sf0_ref[...]
        mix_ref[...] = jnp.zeros(mix_ref.shape, BF16)
        cbuf_ref[...] = jnp.zeros(cbuf_ref.shape, F32)

    never = never_ref[...] != 0
    state = {"s": s_ref[...]}
    for r0 in range(0, x_ref.shape[0], SUB):
        _pass2_sub(r0, state, x_ref, qk_ref, laf_ref, v_ref, g_ref, glu_ref, ob_ref, mod_ref, n2g_ref,
                   gng_ref, fg_ref, cw_ref, cb_ref, lng_ref, lnb_ref,
                   w_out_ref, wg_ref, wu_ref, wd_ref, never, out_ref, mix_ref, cbuf_ref)
    s_ref[...] = state["s"]


def _pass2_call(x2d, qk, laf, v, g, glu, ob, mod, n2g, gng, fg, cw, cb, lng, lnb,
                w_out_bf, wg_bf, wu_bf, wd_bf, sf0, never):
    n_tok = x2d.shape[0]
    n_tiles = n_tok // TILE
    mixer = lambda s: (jnp.minimum(s, n_tiles - 1), 0)
    ffn = lambda s: (jnp.maximum(s - 1, 0), 0)
    const = lambda s: (0, 0)
    full = lambda a: pl.BlockSpec(a.shape, const)
    mixer_spec = lambda w: pl.BlockSpec((TILE, w), mixer)
    return pl.pallas_call(
        _pass2_kernel,
        grid=(n_tiles + 1,),
        in_specs=[pl.BlockSpec((TILE, D_MODEL), ffn), mixer_spec(2 * D_GLA_K), mixer_spec(D_GLA_K),
                  mixer_spec(D_GLA_V), mixer_spec(D_GLA_V), mixer_spec(D_CONV), mixer_spec(D_GLA_V),
                  full(mod), full(n2g), full(gng), full(fg), full(cw), full(cb), full(lng), full(lnb),
                  full(w_out_bf), full(wg_bf), full(wu_bf), full(wd_bf), full(sf0),
                  full(never)],
        out_specs=pl.BlockSpec((TILE, D_MODEL), ffn),
        out_shape=jax.ShapeDtypeStruct((n_tok, D_MODEL), F32),
        scratch_shapes=[pltpu.VMEM((N_HEADS * HEAD_K, HEAD_V), F32),
                        pltpu.VMEM((TILE, D_CONV + D_GLA_V), BF16),
                        pltpu.VMEM((D_CONV // LANES, (TILE // CHUNK) * SEG_PITCH, LANES), F32)],
        compiler_params=pltpu.CompilerParams(dimension_semantics=("arbitrary",),
                                             vmem_limit_bytes=VMEM_LIMIT),
        name="pass2_gla_fwd_conv_merge_ffn",
    )(x2d, qk, laf, v, g, glu, ob, mod, n2g, gng, fg, cw, cb, lng, lnb,
      w_out_bf, wg_bf, wu_bf, wd_bf, sf0, never)


def kernel(x, c, ctx, c_ctx, w_mod, b_mod, norm1_g, norm2_g, w_in, conv_w, conv_b, conv_ln_g,
           conv_ln_b, w_a2_f, b_a_f, w_a2_b, b_a_b, gla_norm_g, w_out, w_gate, w_up, w_down, final_g):
    bsz, n_lat, d = x.shape
    assert bsz == 1 and d == D_MODEL and n_lat % TILE == 0 and TILE % SUB == 0 and ctx.shape[1] % CHUNK == 0
    assert w_mod.shape[0] == 1, "single layer"
    row = lambda a: a.reshape(1, -1)

    ct = jnp.concatenate([c.reshape(D_MODEL, 1), c_ctx.reshape(D_MODEL, 1)], axis=1)
    mod = _mod_call(ct, w_mod[0], row(b_mod[0]))

    wa2f, baf = w_a2_f[0], row(b_a_f[0])
    wa2b, bab = w_a2_b[0], row(b_a_b[0])
    n1g = row(norm1_g[0])

    sf0, sb0, w_in_bf = _ctx_call(ctx[0], mod, n1g, w_in[0], wa2f, baf, wa2b, bab)

    x2d = x[0]
    qk, laf, v, g, glu, ob, wo_bf, wg_bf, wu_bf, wd_bf = _pass1_call(
        x2d, mod, n1g, w_in_bf, wa2f, baf, wa2b, bab, sb0, w_out[0], w_gate[0], w_up[0], w_down[0])
    unslab = lambda w: w.reshape(-1, w.shape[-1])

    out = _pass2_call(
        x2d, qk, laf, v, g, glu, ob, mod, row(norm2_g[0]), row(gla_norm_g[0]), row(final_g),
        conv_w[0], row(conv_b[0]), row(conv_ln_g[0]), row(conv_ln_b[0]),
        unslab(wo_bf), unslab(wg_bf), unslab(wu_bf), unslab(wd_bf), sf0,
        jnp.zeros((CHUNK, LANES), jnp.int32))
    return out.reshape(bsz, n_lat, d)
```

```python
import functools

import jax
import jax.numpy as jnp
from jax import lax
from jax.experimental import pallas as pl
from jax.experimental.pallas import tpu as pltpu

D_MODEL = 1024
D_CONV = 512
CONV_WIDTH = 31
CONV_PAD = 15
N_HEADS = 4
HEAD_K = 64
HEAD_V = 128
D_GLA_K = N_HEADS * HEAD_K
D_GLA_V = N_HEADS * HEAD_V
GLA_RANK = 16
GLA_TAU = 16.0
CHUNK = 64
EPS = 1e-6
D_FF = 2816

C_Q = 2 * D_CONV
C_K = C_Q + D_GLA_K
C_V = C_K + D_GLA_K
C_G = C_V + D_GLA_V
C_R = C_G + D_GLA_V
D_IN = C_R + 2 * GLA_RANK

LANES = 128
MXU_N = 256
SEG_PITCH = 96
SEG_LEAD = 16
CONV_GROUP = 2
TILE = 256
VMEM_LIMIT = 56 * 1024 * 1024

F32 = jnp.float32
BF16 = jnp.bfloat16


def _dot(a, b):
    return jnp.dot(a, b, preferred_element_type=F32)


def _dot_nt(a, b):
    return lax.dot_general(a, b, (((1,), (1,)), ((), ())), preferred_element_type=F32)


def _sigmoid(x):
    return 1.0 / (1.0 + jnp.exp(-x))


def _silu(x):
    return x * _sigmoid(x)


def _log_sigmoid(z):
    return jnp.minimum(z, 0.0) - jnp.log(1.0 + jnp.exp(-jnp.abs(z)))


def _rms_rows(x, g):
    ms = jnp.mean(x * x, axis=-1, keepdims=True)
    return x * lax.rsqrt(ms + EPS) * g


def _interleave(main, side):
    n_main, n_side = len(main), len(side)
    done = 0
    for i, thunk in enumerate(main):
        thunk()
        want = ((i + 1) * n_side) // n_main
        while done < want:
            side[done]()
            done += 1


def _gla_consts(forward):
    r = lax.broadcasted_iota(jnp.int32, (CHUNK, CHUNK), 0)
    c = lax.broadcasted_iota(jnp.int32, (CHUNK, CHUNK), 1)
    tri = (c <= r) if forward else (c >= r)
    tri_bf = jnp.where(tri, 1.0, 0.0).astype(BF16)
    rr = lax.broadcasted_iota(jnp.int32, (N_HEADS * CHUNK, LANES), 0) % CHUNK
    cc = lax.broadcasted_iota(jnp.int32, (N_HEADS * CHUNK, LANES), 1)
    keep = ((cc <= rr) if forward else (cc >= rr)) & (cc < CHUNK)
    lane = lax.broadcasted_iota(jnp.int32, (CHUNK, D_GLA_K), 1)
    head_masks = [(lane >= h * HEAD_K) & (lane < (h + 1) * HEAD_K) for h in range(N_HEADS)]
    return tri_bf, keep, head_masks


def _gla_cumsum(la, consts):
    tri_bf = consts[0]
    la_hi = la.astype(BF16)
    la_lo = (la - la_hi.astype(F32)).astype(BF16)
    return _dot(tri_bf, la_hi) + _dot(tri_bf, la_lo)


def _gla_prep(q, k, v_bf, b, consts, forward, with_output):
    _, keep, head_masks = consts
    b_last = b[CHUNK - 1:CHUNK, :] if forward else b[0:1, :]
    k_w = k * jnp.exp(b_last - b)
    e_last = jnp.exp(b_last)

    tb = jnp.concatenate(
        [k_w, jnp.broadcast_to(e_last, (8, D_GLA_K)), jnp.zeros((CHUNK - 8, D_GLA_K), F32)], axis=0)
    tbt = tb.T
    dec_col = tbt[:, CHUNK:CHUNK + 1]
    kwt_bf = tbt.astype(BF16)
    zeros_v = jnp.zeros((CHUNK, HEAD_V), BF16)
    v_pads = [jnp.concatenate([v_bf[:, h * HEAD_V:(h + 1) * HEAD_V], zeros_v], axis=0)
              for h in range(N_HEADS)]
    kv = jnp.concatenate([_dot(kwt_bf[h * CHUNK:(h + 1) * CHUNK], v_pads[h])
                          for h in range(N_HEADS)], axis=0)
    if not with_output:
        return dec_col, kv, None
    q_t = (q * jnp.exp(b) * (HEAD_K ** -0.5)).astype(BF16)
    k_t = (k * jnp.exp(-b)).astype(BF16)
    qm = jnp.concatenate([jnp.where(m, q_t, jnp.zeros_like(q_t)) for m in head_masks], axis=0)
    kt_pad = jnp.concatenate([k_t, jnp.zeros_like(k_t)], axis=0)
    attn = _dot_nt(qm, kt_pad)
    p_bf = jnp.where(keep, attn, 0.0).astype(BF16)
    return dec_col, kv, (qm, p_bf, v_pads)


def _gla_out(ops, s_enter):
    qm, p_bf, v_pads = ops
    o_inter = _dot(qm, s_enter.astype(BF16))
    outs = []
    for h in range(N_HEADS):
        rows = slice(h * CHUNK, (h + 1) * CHUNK)
        outs.append(_dot(p_bf[rows], v_pads[h]) + o_inter[rows])
    return jnp.concatenate(outs, axis=1)


def _log_decay(r, w_a2, b_a):
    z = _dot(r.astype(BF16), w_a2.astype(BF16)) + b_a
    return _log_sigmoid(z) * (1.0 / GLA_TAU)


def _chunk_rows(ci):
    return slice(ci * CHUNK, (ci + 1) * CHUNK)


def _mod_kernel(ct_ref, w_ref, b_ref, o_ref):
    s = _silu(ct_ref[...])
    w = w_ref[...]
    for j in range(2):
        o_ref[j:j + 1, :] = jnp.sum(w * s[:, j:j + 1], axis=0, keepdims=True) + b_ref[...]


def _mod_call(ct, w_mod, b_mod):
    tn = 3072
    n = w_mod.shape[1]
    return pl.pallas_call(
        _mod_kernel,
        grid=(n // tn,),
        in_specs=[pl.BlockSpec((D_MODEL, 2), lambda j: (0, 0)),
                  pl.BlockSpec((D_MODEL, tn), lambda j: (0, j)),
                  pl.BlockSpec((1, tn), lambda j: (0, j))],
        out_specs=pl.BlockSpec((2, tn), lambda j: (0, j)),
        out_shape=jax.ShapeDtypeStruct((2, n), F32),
        compiler_params=pltpu.CompilerParams(dimension_semantics=("arbitrary",),
                                             vmem_limit_bytes=VMEM_LIMIT),
        name="adaln_mod",
    )(ct, w_mod, b_mod)


def _ctx_kernel(ctx_ref, mod_ref, n1g_ref, w_in_f32_ref, wa2f_ref, baf_ref, wa2b_ref, bab_ref,
                sf_ref, sb_ref, w_in_ref):
    w_in_ref[...] = w_in_f32_ref[...].astype(BF16)
    n_chunks = ctx_ref.shape[0] // CHUNK
    sh = mod_ref[1:2, 0:D_MODEL]
    sc = mod_ref[1:2, D_MODEL:2 * D_MODEL]
    h = (_rms_rows(ctx_ref[...], n1g_ref[...]) * (1.0 + sc) + sh).astype(BF16)
    kv = _dot(h, w_in_ref[:, C_K:C_G])
    r = _dot(h, w_in_ref[:, C_R:D_IN])
    k = kv[:, :D_GLA_K]
    v_bf = kv[:, D_GLA_K:].astype(BF16)
    la_f = _log_decay(r[:, :GLA_RANK], wa2f_ref[...], baf_ref[...])
    la_b = _log_decay(r[:, GLA_RANK:], wa2b_ref[...], bab_ref[...])
    for forward, la, out_ref in ((True, la_f, sf_ref), (False, la_b, sb_ref)):
        consts = _gla_consts(forward)
        order = list(range(n_chunks)) if forward else list(range(n_chunks - 1, -1, -1))
        b = {ci: _gla_cumsum(la[_chunk_rows(ci)], consts) for ci in order}
        s = jnp.zeros((N_HEADS * HEAD_K, HEAD_V), F32)
        for ci in order:
            rows = _chunk_rows(ci)
            dec_col, kv_c, _ = _gla_prep(None, k[rows], v_bf[rows], b[ci], consts, forward, False)
            s = dec_col * s + kv_c
        out_ref[...] = s


def _ctx_call(ctx2d, mod, n1g, w_in, wa2f, baf, wa2b, bab):
    st = jax.ShapeDtypeStruct((N_HEADS * HEAD_K, HEAD_V), F32)
    return pl.pallas_call(
        _ctx_kernel,
        out_shape=(st, st, jax.ShapeDtypeStruct(w_in.shape, BF16)),
        compiler_params=pltpu.CompilerParams(vmem_limit_bytes=VMEM_LIMIT),
        name="context_states",
    )(ctx2d, mod, n1g, w_in, wa2f, baf, wa2b, bab)


def _pass1_step(x_ref, mod_ref, n1g_ref, w_in_ref, wa2f_ref, baf_ref, wa2b_ref, bab_ref,
                qk_ref, laf_ref, v_ref, g_ref, glu_ref, ob_ref, s_ref, carry_w, carry_r):
    n_seg = x_ref.shape[0] // CHUNK
    qk_w, v_w, lab_w = carry_w
    qk_r, v_r, lab_r = carry_r
    order = list(range(n_seg - 1, -1, -1))
    consts = _gla_consts(False)

    b = {ci: _gla_cumsum(lab_r[_chunk_rows(ci), :], consts) for ci in order}

    sh1 = mod_ref[0:1, 0:D_MODEL]
    sc1 = mod_ref[0:1, D_MODEL:2 * D_MODEL]
    h = (_rms_rows(x_ref[...], n1g_ref[...]) * (1.0 + sc1) + sh1).astype(BF16)

    def proj(lo, hi):
        return _dot(h, w_in_ref[:, lo:hi])

    def glu_block(j):
        lo, hi = j * MXU_N, (j + 1) * MXU_N
        glu_ref[:, lo:hi] = proj(lo, hi) * _sigmoid(proj(D_CONV + lo, D_CONV + hi))

    def qk_block(j):
        lo, hi = j * MXU_N, (j + 1) * MXU_N
        blk = proj(C_Q + lo, C_Q + hi)
        qk_ref[:, lo:hi] = blk
        qk_w[:, lo:hi] = blk

    def v_block(j):
        lo, hi = j * MXU_N, (j + 1) * MXU_N
        blk = proj(C_V + lo, C_V + hi).astype(BF16)
        v_ref[:, lo:hi] = blk
        v_w[:, lo:hi] = blk

    def g_block(j):
        lo, hi = j * MXU_N, (j + 1) * MXU_N
        g_ref[:, lo:hi] = proj(C_G + lo, C_G + hi).astype(BF16)

    def decay_block():
        r = proj(C_R, D_IN)
        laf_ref[...] = _log_decay(r[:, :GLA_RANK], wa2f_ref[...], baf_ref[...])
        lab_w[...] = _log_decay(r[:, GLA_RANK:], wa2b_ref[...], bab_ref[...])

    proj_blocks = ([functools.partial(glu_block, j) for j in range(D_CONV // MXU_N)]
                   + [functools.partial(qk_block, j) for j in range(2 * D_GLA_K // MXU_N)]
                   + [functools.partial(v_block, j) for j in range(D_GLA_V // MXU_N)]
                   + [functools.partial(g_block, j) for j in range(D_GLA_V // MXU_N)]
                   + [decay_block])

    state = {"s": s_ref[...]}
    prep, s_enter = {}, {}

    def gla_prep(ci):
        rows = _chunk_rows(ci)
        qk_c = qk_r[rows, :]
        dec_col, kv, ops = _gla_prep(qk_c[:, :D_GLA_K], qk_c[:, D_GLA_K:], v_r[rows, :], b[ci],
                                     consts, False, True)
        prep[ci] = ops
        s_enter[ci] = state["s"]
        state["s"] = dec_col * state["s"] + kv

    def gla_out(ci):
        ob_ref[_chunk_rows(ci), :] = _gla_out(prep[ci], s_enter[ci]).astype(BF16)

    gla_stages = ([functools.partial(gla_prep, ci) for ci in order]
                  + [functools.partial(gla_out, ci) for ci in order])
    _interleave(proj_blocks, gla_stages)
    s_ref[...] = state["s"]


def _pass1_kernel(x_ref, mod_ref, n1g_ref, w_in_ref, wa2f_ref, baf_ref, wa2b_ref, bab_ref, sb0_ref,
                  wo_f32_ref, wg_f32_ref, wu_f32_ref, wd_f32_ref,
                  qk_ref, laf_ref, v_ref, g_ref, glu_ref, ob_ref,
                  wo_bf_ref, wg_bf_ref, wu_bf_ref, wd_bf_ref,
                  s_ref, qk_a, v_a, lab_a, qk_b, v_b, lab_b):
    step = pl.program_id(0)

    for src, dst in ((wo_f32_ref, wo_bf_ref), (wg_f32_ref, wg_bf_ref), (wu_f32_ref, wu_bf_ref),
                     (wd_f32_ref, wd_bf_ref)):
        dst[...] = src[...].astype(BF16)
    buf_a = (qk_a, v_a, lab_a)
    buf_b = (qk_b, v_b, lab_b)

    @pl.when(step == 0)
    def _():
        s_ref[...] = jnp.zeros(s_ref.shape, F32)
        for ref in buf_b:
            ref[...] = jnp.zeros(ref.shape, ref.dtype)

    run = functools.partial(_pass1_step, x_ref, mod_ref, n1g_ref, w_in_ref, wa2f_ref, baf_ref,
                            wa2b_ref, bab_ref, qk_ref, laf_ref, v_ref, g_ref, glu_ref, ob_ref, s_ref)

    @pl.when(step % 2 == 0)
    def _():
        run(buf_a, buf_b)

    @pl.when(step % 2 == 1)
    def _():
        run(buf_b, buf_a)

    @pl.when(step == 0)
    def _():
        s_ref[...] = sb0_ref[...]


def _slabs(w, n_slabs):
    rows, cols = w.shape
    assert rows % (n_slabs * 16) == 0
    return w.reshape(n_slabs, rows // n_slabs, cols)


def _pass1_call(x2d, mod, n1g, w_in_bf, wa2f, baf, wa2b, bab, sb0, w_out, w_gate, w_up, w_down):
    n_tok = x2d.shape[0]
    n_tiles = n_tok // TILE
    cast_in = [_slabs(w, n) for w, n in ((w_out, 32), (w_gate, 32), (w_up, 32), (w_down, 44))]
    assert all(w.shape[0] <= n_tiles for w in cast_in)
    cast_specs = [pl.BlockSpec((1,) + w.shape[1:],
                               functools.partial(lambda n, s: (jnp.minimum(s, n - 1), 0, 0), w.shape[0]))
                  for w in cast_in]
    proj = lambda s: (n_tiles - 1 - jnp.minimum(s, n_tiles - 1), 0)
    gla = lambda s: (n_tiles - 1 - jnp.maximum(s - 1, 0), 0)
    const = lambda s: (0, 0)
    full = lambda a: pl.BlockSpec(a.shape, const)
    out_shape = (
        jax.ShapeDtypeStruct((n_tok, 2 * D_GLA_K), F32),
        jax.ShapeDtypeStruct((n_tok, D_GLA_K), F32),
        jax.ShapeDtypeStruct((n_tok, D_GLA_V), BF16),
        jax.ShapeDtypeStruct((n_tok, D_GLA_V), BF16),
        jax.ShapeDtypeStruct((n_tok, D_CONV), F32),
        jax.ShapeDtypeStruct((n_tok, D_GLA_V), BF16),
    )
    carry = [pltpu.VMEM((TILE, 2 * D_GLA_K), F32), pltpu.VMEM((TILE, D_GLA_V), BF16),
             pltpu.VMEM((TILE, D_GLA_K), F32)]
    return pl.pallas_call(
        _pass1_kernel,
        grid=(n_tiles + 1,),
        in_specs=[pl.BlockSpec((TILE, D_MODEL), proj), full(mod), full(n1g), full(w_in_bf),
                  full(wa2f), full(baf), full(wa2b), full(bab), full(sb0)] + cast_specs,
        out_specs=(pl.BlockSpec((TILE, 2 * D_GLA_K), proj), pl.BlockSpec((TILE, D_GLA_K), proj),
                   pl.BlockSpec((TILE, D_GLA_V), proj), pl.BlockSpec((TILE, D_GLA_V), proj),
                   pl.BlockSpec((TILE, D_CONV), proj), pl.BlockSpec((TILE, D_GLA_V), gla))
        + tuple(cast_specs),
        out_shape=out_shape + tuple(jax.ShapeDtypeStruct(w.shape, BF16) for w in cast_in),
        scratch_shapes=[pltpu.VMEM((N_HEADS * HEAD_K, HEAD_V), F32)] + carry + carry,
        compiler_params=pltpu.CompilerParams(dimension_semantics=("arbitrary",),
                                             vmem_limit_bytes=VMEM_LIMIT),
        name="pass1_project_gla_bwd",
    )(x2d, mod, n1g, w_in_bf, wa2f, baf, wa2b, bab, sb0, *cast_in)


def _conv_stage_in(glu_ref, cbuf_ref):
    n_seg = glu_ref.shape[0] // CHUNK
    for j in range(D_CONV // LANES):
        for s in range(n_seg):
            base = s * SEG_PITCH + SEG_LEAD
            cbuf_ref[j, base:base + CHUNK, :] = glu_ref[_chunk_rows(s), j * LANES:(j + 1) * LANES]


def _conv_taps(segs, j, cbuf_ref, cw_ref, zero):
    accs = [zero for _ in segs]
    for t in range(CONV_WIDTH):
        w = cw_ref[t:t + 1, j * LANES:(j + 1) * LANES]
        for i, s in enumerate(segs):
            start = s * SEG_PITCH + SEG_LEAD - CONV_PAD + t
            accs[i] = accs[i] + cbuf_ref[j, start:start + CHUNK, :] * w
    return accs


def _conv_norm(cols, cb_ref, lng_ref, lnb_ref):
    y = jnp.concatenate(cols, axis=1) + cb_ref[...]
    mu = jnp.mean(y, axis=-1, keepdims=True)
    yc = y - mu
    var = jnp.mean(yc * yc, axis=-1, keepdims=True)
    return _silu(yc * lax.rsqrt(var + EPS) * lng_ref[...] + lnb_ref[...])


def _pass2_kernel(x_ref, qk_ref, laf_ref, v_ref, g_ref, glu_ref, ob_ref, mod_ref, n2g_ref,
                  gng_ref, fg_ref, cw_ref, cb_ref, lng_ref, lnb_ref,
                  w_out_ref, wg_ref, wu_ref, wd_ref, sf0_ref, never_ref,
                  out_ref, s_ref, mix_ref, cbuf_ref):
    n_seg = x_ref.shape[0] // CHUNK
    n_ff = D_FF // MXU_N

    @pl.when(pl.program_id(0) == 0)
    def _():
        s_ref[...] = sf0_ref[...]
        mix_ref[...] = jnp.zeros(mix_ref.shape, BF16)
        cbuf_ref[...] = jnp.zeros(cbuf_ref.shape, F32)

    chunks = list(range(n_seg))
    consts = _gla_consts(True)
    gng = gng_ref[...]

    _conv_stage_in(glu_ref, cbuf_ref)
    b = {ci: _gla_cumsum(laf_ref[_chunk_rows(ci), :], consts) for ci in chunks}

    g1 = mod_ref[0:1, 2 * D_MODEL:3 * D_MODEL]
    sh2 = mod_ref[0:1, 3 * D_MODEL:4 * D_MODEL]
    sc2 = mod_ref[0:1, 4 * D_MODEL:5 * D_MODEL]
    g2 = mod_ref[0:1, 5 * D_MODEL:6 * D_MODEL]
    x1 = x_ref[...] + g1 * _dot(mix_ref[...], w_out_ref[...])

    state = {"s": s_ref[...]}
    prep, s_enter = {}, {}

    def gla_prep(ci):
        rows = _chunk_rows(ci)
        qk_c = qk_ref[rows, :]
        dec_col, kv, ops = _gla_prep(qk_c[:, :D_GLA_K], qk_c[:, D_GLA_K:], v_ref[rows, :], b[ci],
                                     consts, True, True)
        prep[ci] = ops
        s_enter[ci] = state["s"]
        state["s"] = dec_col * state["s"] + kv

    def gla_out(ci):
        rows = _chunk_rows(ci)
        o = _gla_out(prep[ci], s_enter[ci]) + ob_ref[rows, :].astype(F32)
        gate = _silu(g_ref[rows, :].astype(F32))
        for h in range(N_HEADS):
            cols = slice(h * HEAD_V, (h + 1) * HEAD_V)
            mix_ref[rows, D_CONV + h * HEAD_V:D_CONV + (h + 1) * HEAD_V] = (
                _rms_rows(o[:, cols], gng) * gate[:, cols]).astype(BF16)

    conv_cols = {}

    never = never_ref[...] != 0
    latest = {}

    def conv_taps(segs, j):
        zero = jnp.where(never, latest["ffn"], 0.0)
        for s, acc in zip(segs, _conv_taps(segs, j, cbuf_ref, cw_ref, zero)):
            conv_cols[s, j] = acc

    def conv_out(s):
        cols = [conv_cols.pop((s, j)) for j in range(D_CONV // LANES)]
        mix_ref[_chunk_rows(s), 0:D_CONV] = _conv_norm(cols, cb_ref, lng_ref, lnb_ref).astype(BF16)

    side = [functools.partial(gla_prep, ci) for ci in chunks]
    for g0 in range(0, n_seg, CONV_GROUP):
        segs = chunks[g0:g0 + CONV_GROUP]
        side += [functools.partial(gla_out, ci) for ci in segs]
        side += [functools.partial(conv_taps, segs, j) for j in range(D_CONV // LANES)]
        side += [functools.partial(conv_out, ci) for ci in segs]

    n_early = min(2, n_seg)
    for thunk in side[:n_early]:
        thunk()

    h2 = (_rms_rows(x1, n2g_ref[...]) * (1.0 + sc2) + sh2).astype(BF16)
    act, acc = {}, {}

    def gate_up(j):
        cols = slice(j * MXU_N, (j + 1) * MXU_N)
        gate = _dot(h2, wg_ref[:, cols])
        latest["ffn"] = gate[0:CHUNK, 0:LANES]
        act[j] = (_silu(gate) * _dot(h2, wu_ref[:, cols])).astype(BF16)

    def down(j):
        part = _dot(act.pop(j), wd_ref[j * MXU_N:(j + 1) * MXU_N, :])
        acc["v"] = part if j == 0 else acc["v"] + part

    ffn = [functools.partial(gate_up, 0)]
    for j in range(1, n_ff):
        ffn += [functools.partial(gate_up, j), functools.partial(down, j - 1)]
    ffn.append(functools.partial(down, n_ff - 1))
    _interleave(ffn, side[n_early:])
    s_ref[...] = state["s"]

    x2 = x1 + g2 * acc["v"]
    out_ref[...] = _rms_rows(x2, fg_ref[...])


def _pass2_call(x2d, qk, laf, v, g, glu, ob, mod, n2g, gng, fg, cw, cb, lng, lnb,
                w_out_bf, wg_bf, wu_bf, wd_bf, sf0, never):
    n_tok = x2d.shape[0]
    n_tiles = n_tok // TILE
    mixer = lambda s: (jnp.minimum(s, n_tiles - 1), 0)
    ffn = lambda s: (jnp.maximum(s - 1, 0), 0)
    const = lambda s: (0, 0)
    full = lambda a: pl.BlockSpec(a.shape, const)
    mixer_spec = lambda w: pl.BlockSpec((TILE, w), mixer)
    return pl.pallas_call(
        _pass2_kernel,
        grid=(n_tiles + 1,),
        in_specs=[pl.BlockSpec((TILE, D_MODEL), ffn), mixer_spec(2 * D_GLA_K), mixer_spec(D_GLA_K),
                  mixer_spec(D_GLA_V), mixer_spec(D_GLA_V), mixer_spec(D_CONV), mixer_spec(D_GLA_V),
                  full(mod), full(n2g), full(gng), full(fg), full(cw), full(cb), full(lng), full(lnb),
                  full(w_out_bf), full(wg_bf), full(wu_bf), full(wd_bf), full(sf0),
                  full(never)],
        out_specs=pl.BlockSpec((TILE, D_MODEL), ffn),
        out_shape=jax.ShapeDtypeStruct((n_tok, D_MODEL), F32),
        scratch_shapes=[pltpu.VMEM((N_HEADS * HEAD_K, HEAD_V), F32),
                        pltpu.VMEM((TILE, D_CONV + D_GLA_V), BF16),
                        pltpu.VMEM((D_CONV // LANES, (TILE // CHUNK) * SEG_PITCH, LANES), F32)],
        compiler_params=pltpu.CompilerParams(dimension_semantics=("arbitrary",),
                                             vmem_limit_bytes=VMEM_LIMIT),
        name="pass2_gla_fwd_conv_merge_ffn",
    )(x2d, qk, laf, v, g, glu, ob, mod, n2g, gng, fg, cw, cb, lng, lnb,
      w_out_bf, wg_bf, wu_bf, wd_bf, sf0, never)


def kernel(x, c, ctx, c_ctx, w_mod, b_mod, norm1_g, norm2_g, w_in, conv_w, conv_b, conv_ln_g,
           conv_ln_b, w_a2_f, b_a_f, w_a2_b, b_a_b, gla_norm_g, w_out, w_gate, w_up, w_down, final_g):
    bsz, n_lat, d = x.shape
    assert bsz == 1 and d == D_MODEL and n_lat % TILE == 0 and ctx.shape[1] % CHUNK == 0
    assert (TILE // CHUNK) % CONV_GROUP == 0
    assert w_mod.shape[0] == 1, "single layer"
    row = lambda a: a.reshape(1, -1)

    ct = jnp.concatenate([c.reshape(D_MODEL, 1), c_ctx.reshape(D_MODEL, 1)], axis=1)
    mod = _mod_call(ct, w_mod[0], row(b_mod[0]))

    wa2f, baf = w_a2_f[0], row(b_a_f[0])
    wa2b, bab = w_a2_b[0], row(b_a_b[0])
    n1g = row(norm1_g[0])

    sf0, sb0, w_in_bf = _ctx_call(ctx[0], mod, n1g, w_in[0], wa2f, baf, wa2b, bab)

    x2d = x[0]
    qk, laf, v, g, glu, ob, wo_bf, wg_bf, wu_bf, wd_bf = _pass1_call(
        x2d, mod, n1g, w_in_bf, wa2f, baf, wa2b, bab, sb0, w_out[0], w_gate[0], w_up[0], w_down[0])
    unslab = lambda w: w.reshape(-1, w.shape[-1])

    out = _pass2_call(
        x2d, qk, laf, v, g, glu, ob, mod, row(norm2_g[0]), row(gla_norm_g[0]), row(final_g),
        conv_w[0], row(conv_b[0]), row(conv_ln_g[0]), row(conv_ln_b[0]),
        unslab(wo_bf), unslab(wg_bf), unslab(wu_bf), unslab(wd_bf), sf0,
        jnp.zeros((CHUNK, LANES), jnp.int32))
    return out.reshape(bsz, n_lat, d)
```

```python
import functools

import jax
import jax.numpy as jnp
from jax import lax
from jax.experimental import pallas as pl
from jax.experimental.pallas import tpu as pltpu

D_MODEL = 1024
D_CONV = 512
CONV_WIDTH = 31
CONV_PAD = 15
N_HEADS = 4
HEAD_K = 64
HEAD_V = 128
D_GLA_K = N_HEADS * HEAD_K
D_GLA_V = N_HEADS * HEAD_V
GLA_RANK = 16
GLA_TAU = 16.0
CHUNK = 64
EPS = 1e-6
D_FF = 2816

C_Q = 2 * D_CONV
C_K = C_Q + D_GLA_K
C_V = C_K + D_GLA_K
C_G = C_V + D_GLA_V
C_R = C_G + D_GLA_V
D_IN = C_R + 2 * GLA_RANK

LANES = 128
MXU_N = 256
SEG_PITCH = 96
SEG_LEAD = 16
CONV_GROUP = 2
TILE = 256
VMEM_LIMIT = 56 * 1024 * 1024

F32 = jnp.float32
BF16 = jnp.bfloat16


def _dot(a, b):
    return jnp.dot(a, b, preferred_element_type=F32)


def _dot_nt(a, b):
    return lax.dot_general(a, b, (((1,), (1,)), ((), ())), preferred_element_type=F32)


def _sigmoid(x):
    return 1.0 / (1.0 + jnp.exp(-x))


def _silu(x):
    return x * _sigmoid(x)


def _log_sigmoid(z):
    return jnp.minimum(z, 0.0) - jnp.log(1.0 + jnp.exp(-jnp.abs(z)))


def _rms_rows(x, g):
    ms = jnp.mean(x * x, axis=-1, keepdims=True)
    return x * lax.rsqrt(ms + EPS) * g


def _interleave(main, side):
    n_main, n_side = len(main), len(side)
    done = 0
    for i, thunk in enumerate(main):
        thunk()
        want = ((i + 1) * n_side) // n_main
        while done < want:
            side[done]()
            done += 1


def _gla_consts(forward):
    r = lax.broadcasted_iota(jnp.int32, (CHUNK, CHUNK), 0)
    c = lax.broadcasted_iota(jnp.int32, (CHUNK, CHUNK), 1)
    tri = (c <= r) if forward else (c >= r)
    tri_bf = jnp.where(tri, 1.0, 0.0).astype(BF16)
    rr = lax.broadcasted_iota(jnp.int32, (N_HEADS * CHUNK, LANES), 0) % CHUNK
    cc = lax.broadcasted_iota(jnp.int32, (N_HEADS * CHUNK, LANES), 1)
    keep = ((cc <= rr) if forward else (cc >= rr)) & (cc < CHUNK)
    lane = lax.broadcasted_iota(jnp.int32, (CHUNK, D_GLA_K), 1)
    head_masks = [(lane >= h * HEAD_K) & (lane < (h + 1) * HEAD_K) for h in range(N_HEADS)]
    return tri_bf, keep, head_masks


def _gla_cumsum(la, consts):
    tri_bf = consts[0]
    la_hi = la.astype(BF16)
    la_lo = (la - la_hi.astype(F32)).astype(BF16)
    return _dot(tri_bf, la_hi) + _dot(tri_bf, la_lo)


def _gla_prep(q, k, v_bf, b, consts, forward, with_output):
    _, keep, head_masks = consts
    b_last = b[CHUNK - 1:CHUNK, :] if forward else b[0:1, :]
    k_w = k * jnp.exp(b_last - b)
    e_last = jnp.exp(b_last)

    tb = jnp.concatenate(
        [k_w, jnp.broadcast_to(e_last, (8, D_GLA_K)), jnp.zeros((CHUNK - 8, D_GLA_K), F32)], axis=0)
    tbt = tb.T
    dec_col = tbt[:, CHUNK:CHUNK + 1]
    kwt_bf = tbt.astype(BF16)
    zeros_v = jnp.zeros((CHUNK, HEAD_V), BF16)
    v_pads = [jnp.concatenate([v_bf[:, h * HEAD_V:(h + 1) * HEAD_V], zeros_v], axis=0)
              for h in range(N_HEADS)]
    kv = jnp.concatenate([_dot(kwt_bf[h * CHUNK:(h + 1) * CHUNK], v_pads[h])
                          for h in range(N_HEADS)], axis=0)
    if not with_output:
        return dec_col, kv, None
    q_t = (q * jnp.exp(b) * (HEAD_K ** -0.5)).astype(BF16)
    k_t = (k * jnp.exp(-b)).astype(BF16)
    qm = jnp.concatenate([jnp.where(m, q_t, jnp.zeros_like(q_t)) for m in head_masks], axis=0)
    kt_pad = jnp.concatenate([k_t, jnp.zeros_like(k_t)], axis=0)
    attn = _dot_nt(qm, kt_pad)
    p_bf = jnp.where(keep, attn, 0.0).astype(BF16)
    return dec_col, kv, (qm, p_bf, v_pads)


def _gla_out(ops, s_enter):
    qm, p_bf, v_pads = ops
    o_inter = _dot(qm, s_enter.astype(BF16))
    outs = []
    for h in range(N_HEADS):
        rows = slice(h * CHUNK, (h + 1) * CHUNK)
        outs.append(_dot(p_bf[rows], v_pads[h]) + o_inter[rows])
    return jnp.concatenate(outs, axis=1)


def _log_decay(r, w_a2, b_a):
    z = _dot(r.astype(BF16), w_a2.astype(BF16)) + b_a
    return _log_sigmoid(z) * (1.0 / GLA_TAU)


def _chunk_rows(ci):
    return slice(ci * CHUNK, (ci + 1) * CHUNK)


def _mod_kernel(ct_ref, w_ref, b_ref, o_ref):
    s = _silu(ct_ref[...])
    w = w_ref[...]
    for j in range(2):
        o_ref[j:j + 1, :] = jnp.sum(w * s[:, j:j + 1], axis=0, keepdims=True) + b_ref[...]


def _mod_call(ct, w_mod, b_mod):
    tn = 768
    n = w_mod.shape[1]
    return pl.pallas_call(
        _mod_kernel,
        grid=(n // tn,),
        in_specs=[pl.BlockSpec((D_MODEL, 2), lambda j: (0, 0)),
                  pl.BlockSpec((D_MODEL, tn), lambda j: (0, j)),
                  pl.BlockSpec((1, tn), lambda j: (0, j))],
        out_specs=pl.BlockSpec((2, tn), lambda j: (0, j)),
        out_shape=jax.ShapeDtypeStruct((2, n), F32),
        compiler_params=pltpu.CompilerParams(dimension_semantics=("arbitrary",),
                                             vmem_limit_bytes=VMEM_LIMIT),
        name="adaln_mod",
    )(ct, w_mod, b_mod)


def _ctx_kernel(ctx_ref, mod_ref, n1g_ref, w_in_t_ref, wa2f_ref, baf_ref, wa2b_ref, bab_ref,
                sf_ref, sb_ref, w_in_ref):
    n_full = D_IN // MXU_N
    for j in range(n_full):
        cols = slice(j * MXU_N, (j + 1) * MXU_N)
        w_in_ref[:, cols] = w_in_t_ref[cols, :].T.astype(BF16)
    tail = w_in_t_ref[D_IN - LANES:D_IN, :].T
    w_in_ref[:, n_full * MXU_N:D_IN] = tail[:, LANES - (D_IN - n_full * MXU_N):].astype(BF16)
    n_chunks = ctx_ref.shape[0] // CHUNK
    sh = mod_ref[1:2, 0:D_MODEL]
    sc = mod_ref[1:2, D_MODEL:2 * D_MODEL]
    h = (_rms_rows(ctx_ref[...], n1g_ref[...]) * (1.0 + sc) + sh).astype(BF16)
    kv = _dot(h, w_in_ref[:, C_K:C_G])
    r = _dot(h, w_in_ref[:, C_R:D_IN])
    k = kv[:, :D_GLA_K]
    v_bf = kv[:, D_GLA_K:].astype(BF16)
    la_f = _log_decay(r[:, :GLA_RANK], wa2f_ref[...], baf_ref[...])
    la_b = _log_decay(r[:, GLA_RANK:], wa2b_ref[...], bab_ref[...])
    for forward, la, out_ref in ((True, la_f, sf_ref), (False, la_b, sb_ref)):
        consts = _gla_consts(forward)
        order = list(range(n_chunks)) if forward else list(range(n_chunks - 1, -1, -1))
        b = {ci: _gla_cumsum(la[_chunk_rows(ci)], consts) for ci in order}
        s = jnp.zeros((N_HEADS * HEAD_K, HEAD_V), F32)
        for ci in order:
            rows = _chunk_rows(ci)
            dec_col, kv_c, _ = _gla_prep(None, k[rows], v_bf[rows], b[ci], consts, forward, False)
            s = dec_col * s + kv_c
        out_ref[...] = s


def _ctx_call(ctx2d, mod, n1g, w_in_t, wa2f, baf, wa2b, bab):
    st = jax.ShapeDtypeStruct((N_HEADS * HEAD_K, HEAD_V), F32)
    return pl.pallas_call(
        _ctx_kernel,
        out_shape=(st, st, jax.ShapeDtypeStruct(w_in_t.shape[::-1], BF16)),
        compiler_params=pltpu.CompilerParams(vmem_limit_bytes=VMEM_LIMIT),
        name="context_states",
    )(ctx2d, mod, n1g, w_in_t, wa2f, baf, wa2b, bab)


def _pass1_step(x_ref, mod_ref, n1g_ref, w_in_ref, wa2f_ref, baf_ref, wa2b_ref, bab_ref,
                qk_ref, laf_ref, v_ref, g_ref, glu_ref, ob_ref, s_ref, carry_w, carry_r):
    n_seg = x_ref.shape[0] // CHUNK
    qk_w, v_w, lab_w = carry_w
    qk_r, v_r, lab_r = carry_r
    order = list(range(n_seg - 1, -1, -1))
    consts = _gla_consts(False)

    b = {ci: _gla_cumsum(lab_r[_chunk_rows(ci), :], consts) for ci in order}

    sh1 = mod_ref[0:1, 0:D_MODEL]
    sc1 = mod_ref[0:1, D_MODEL:2 * D_MODEL]
    h = (_rms_rows(x_ref[...], n1g_ref[...]) * (1.0 + sc1) + sh1).astype(BF16)

    def proj(lo, hi):
        return _dot(h, w_in_ref[:, lo:hi])

    def glu_block(j):
        lo, hi = j * MXU_N, (j + 1) * MXU_N
        glu_ref[:, lo:hi] = proj(lo, hi) * _sigmoid(proj(D_CONV + lo, D_CONV + hi))

    def qk_block(j):
        lo, hi = j * MXU_N, (j + 1) * MXU_N
        blk = proj(C_Q + lo, C_Q + hi)
        qk_ref[:, lo:hi] = blk
        qk_w[:, lo:hi] = blk

    def v_block(j):
        lo, hi = j * MXU_N, (j + 1) * MXU_N
        blk = proj(C_V + lo, C_V + hi).astype(BF16)
        v_ref[:, lo:hi] = blk
        v_w[:, lo:hi] = blk

    def g_block(j):
        lo, hi = j * MXU_N, (j + 1) * MXU_N
        g_ref[:, lo:hi] = proj(C_G + lo, C_G + hi).astype(BF16)

    def decay_block():
        r = proj(C_R, D_IN)
        laf_ref[...] = _log_decay(r[:, :GLA_RANK], wa2f_ref[...], baf_ref[...])
        lab_w[...] = _log_decay(r[:, GLA_RANK:], wa2b_ref[...], bab_ref[...])

    proj_blocks = ([functools.partial(glu_block, j) for j in range(D_CONV // MXU_N)]
                   + [functools.partial(qk_block, j) for j in range(2 * D_GLA_K // MXU_N)]
                   + [functools.partial(v_block, j) for j in range(D_GLA_V // MXU_N)]
                   + [functools.partial(g_block, j) for j in range(D_GLA_V // MXU_N)]
                   + [decay_block])

    state = {"s": s_ref[...]}
    prep, s_enter = {}, {}

    def gla_prep(ci):
        rows = _chunk_rows(ci)
        qk_c = qk_r[rows, :]
        dec_col, kv, ops = _gla_prep(qk_c[:, :D_GLA_K], qk_c[:, D_GLA_K:], v_r[rows, :], b[ci],
                                     consts, False, True)
        prep[ci] = ops
        s_enter[ci] = state["s"]
        state["s"] = dec_col * state["s"] + kv

    def gla_out(ci):
        ob_ref[_chunk_rows(ci), :] = _gla_out(prep[ci], s_enter[ci]).astype(BF16)

    gla_stages = ([functools.partial(gla_prep, ci) for ci in order]
                  + [functools.partial(gla_out, ci) for ci in order])
    _interleave(proj_blocks, gla_stages)
    s_ref[...] = state["s"]


def _pass1_kernel(x_ref, mod_ref, n1g_ref, w_in_ref, wa2f_ref, baf_ref, wa2b_ref, bab_ref, sb0_ref,
                  wo_f32_ref, wg_f32_ref, wu_f32_ref, wd_f32_ref,
                  qk_ref, laf_ref, v_ref, g_ref, glu_ref, ob_ref,
                  wo_bf_ref, wg_bf_ref, wu_bf_ref, wd_bf_ref,
                  s_ref, qk_a, v_a, lab_a, qk_b, v_b, lab_b):
    step = pl.program_id(0)

    for src, dst in ((wo_f32_ref, wo_bf_ref), (wg_f32_ref, wg_bf_ref), (wu_f32_ref, wu_bf_ref),
                     (wd_f32_ref, wd_bf_ref)):
        dst[...] = src[...].astype(BF16)
    buf_a = (qk_a, v_a, lab_a)
    buf_b = (qk_b, v_b, lab_b)

    @pl.when(step == 0)
    def _():
        s_ref[...] = jnp.zeros(s_ref.shape, F32)
        for ref in buf_b:
            ref[...] = jnp.zeros(ref.shape, ref.dtype)

    run = functools.partial(_pass1_step, x_ref, mod_ref, n1g_ref, w_in_ref, wa2f_ref, baf_ref,
                            wa2b_ref, bab_ref, qk_ref, laf_ref, v_ref, g_ref, glu_ref, ob_ref, s_ref)

    @pl.when(step % 2 == 0)
    def _():
        run(buf_a, buf_b)

    @pl.when(step % 2 == 1)
    def _():
        run(buf_b, buf_a)

    @pl.when(step == 0)
    def _():
        s_ref[...] = sb0_ref[...]


def _slabs(w, n_slabs):
    rows, cols = w.shape
    assert rows % (n_slabs * 16) == 0
    return w.reshape(n_slabs, rows // n_slabs, cols)


def _pass1_call(x2d, mod, n1g, w_in_bf, wa2f, baf, wa2b, bab, sb0, w_out, w_gate, w_up, w_down):
    n_tok = x2d.shape[0]
    n_tiles = n_tok // TILE
    cast_in = [_slabs(w, n) for w, n in ((w_out, 32), (w_gate, 32), (w_up, 32), (w_down, 44))]
    assert all(w.shape[0] <= n_tiles for w in cast_in)
    cast_specs = [pl.BlockSpec((1,) + w.shape[1:],
                               functools.partial(lambda n, s: (jnp.minimum(s, n - 1), 0, 0), w.shape[0]))
                  for w in cast_in]
    proj = lambda s: (n_tiles - 1 - jnp.minimum(s, n_tiles - 1), 0)
    gla = lambda s: (n_tiles - 1 - jnp.maximum(s - 1, 0), 0)
    const = lambda s: (0, 0)
    full = lambda a: pl.BlockSpec(a.shape, const)
    out_shape = (
        jax.ShapeDtypeStruct((n_tok, 2 * D_GLA_K), F32),
        jax.ShapeDtypeStruct((n_tok, D_GLA_K), F32),
        jax.ShapeDtypeStruct((n_tok, D_GLA_V), BF16),
        jax.ShapeDtypeStruct((n_tok, D_GLA_V), BF16),
        jax.ShapeDtypeStruct((n_tok, D_CONV), F32),
        jax.ShapeDtypeStruct((n_tok, D_GLA_V), BF16),
    )
    carry = [pltpu.VMEM((TILE, 2 * D_GLA_K), F32), pltpu.VMEM((TILE, D_GLA_V), BF16),
             pltpu.VMEM((TILE, D_GLA_K), F32)]
    return pl.pallas_call(
        _pass1_kernel,
        grid=(n_tiles + 1,),
        in_specs=[pl.BlockSpec((TILE, D_MODEL), proj), full(mod), full(n1g), full(w_in_bf),
                  full(wa2f), full(baf), full(wa2b), full(bab), full(sb0)] + cast_specs,
        out_specs=(pl.BlockSpec((TILE, 2 * D_GLA_K), proj), pl.BlockSpec((TILE, D_GLA_K), proj),
                   pl.BlockSpec((TILE, D_GLA_V), proj), pl.BlockSpec((TILE, D_GLA_V), proj),
                   pl.BlockSpec((TILE, D_CONV), proj), pl.BlockSpec((TILE, D_GLA_V), gla))
        + tuple(cast_specs),
        out_shape=out_shape + tuple(jax.ShapeDtypeStruct(w.shape, BF16) for w in cast_in),
        scratch_shapes=[pltpu.VMEM((N_HEADS * HEAD_K, HEAD_V), F32)] + carry + carry,
        compiler_params=pltpu.CompilerParams(dimension_semantics=("arbitrary",),
                                             vmem_limit_bytes=VMEM_LIMIT),
        name="pass1_project_gla_bwd",
    )(x2d, mod, n1g, w_in_bf, wa2f, baf, wa2b, bab, sb0, *cast_in)


def _conv_stage_in(glu_ref, cbuf_ref):
    n_seg = glu_ref.shape[0] // CHUNK
    for j in range(D_CONV // LANES):
        for s in range(n_seg):
            base = s * SEG_PITCH + SEG_LEAD
            cbuf_ref[j, base:base + CHUNK, :] = glu_ref[_chunk_rows(s), j * LANES:(j + 1) * LANES]


def _conv_taps(segs, j, cbuf_ref, cw_ref, zero):
    accs = [zero for _ in segs]
    for t in range(CONV_WIDTH):
        w = cw_ref[t:t + 1, j * LANES:(j + 1) * LANES]
        for i, s in enumerate(segs):
            start = s * SEG_PITCH + SEG_LEAD - CONV_PAD + t
            accs[i] = accs[i] + cbuf_ref[j, start:start + CHUNK, :] * w
    return accs


def _conv_norm(cols, cb_ref, lng_ref, lnb_ref):
    y = jnp.concatenate(cols, axis=1) + cb_ref[...]
    mu = jnp.mean(y, axis=-1, keepdims=True)
    yc = y - mu
    var = jnp.mean(yc * yc, axis=-1, keepdims=True)
    return _silu(yc * lax.rsqrt(var + EPS) * lng_ref[...] + lnb_ref[...])


def _pass2_kernel(x_ref, qk_ref, laf_ref, v_ref, g_ref, glu_ref, ob_ref, mod_ref, n2g_ref,
                  gng_ref, fg_ref, cw_ref, cb_ref, lng_ref, lnb_ref,
                  w_out_ref, wg_ref, wu_ref, wd_ref, sf0_ref, never_ref,
                  out_ref, s_ref, mix_ref, cbuf_ref):
    n_seg = x_ref.shape[0] // CHUNK
    n_ff = D_FF // MXU_N

    @pl.when(pl.program_id(0) == 0)
    def _():
        s_ref[...] = sf0_ref[...]
        mix_ref[...] = jnp.zeros(mix_ref.shape, BF16)
        cbuf_ref[...] = jnp.zeros(cbuf_ref.shape, F32)

    chunks = list(range(n_seg))
    consts = _gla_consts(True)
    gng = gng_ref[...]

    _conv_stage_in(glu_ref, cbuf_ref)
    b = {ci: _gla_cumsum(laf_ref[_chunk_rows(ci), :], consts) for ci in chunks}

    g1 = mod_ref[0:1, 2 * D_MODEL:3 * D_MODEL]
    sh2 = mod_ref[0:1, 3 * D_MODEL:4 * D_MODEL]
    sc2 = mod_ref[0:1, 4 * D_MODEL:5 * D_MODEL]
    g2 = mod_ref[0:1, 5 * D_MODEL:6 * D_MODEL]
    x1 = x_ref[...] + g1 * _dot(mix_ref[...], w_out_ref[...])

    state = {"s": s_ref[...]}
    prep, s_enter = {}, {}

    def gla_prep(ci):
        rows = _chunk_rows(ci)
        qk_c = qk_ref[rows, :]
        dec_col, kv, ops = _gla_prep(qk_c[:, :D_GLA_K], qk_c[:, D_GLA_K:], v_ref[rows, :], b[ci],
                                     consts, True, True)
        prep[ci] = ops
        s_enter[ci] = state["s"]
        state["s"] = dec_col * state["s"] + kv

    def gla_out(ci):
        rows = _chunk_rows(ci)
        o = _gla_out(prep[ci], s_enter[ci]) + ob_ref[rows, :].astype(F32)
        gate = _silu(g_ref[rows, :].astype(F32))
        for h in range(N_HEADS):
            cols = slice(h * HEAD_V, (h + 1) * HEAD_V)
            mix_ref[rows, D_CONV + h * HEAD_V:D_CONV + (h + 1) * HEAD_V] = (
                _rms_rows(o[:, cols], gng) * gate[:, cols]).astype(BF16)

    conv_cols = {}

    never = never_ref[...] != 0
    latest = {}

    def conv_taps(segs, j):
        zero = jnp.where(never, latest["ffn"], 0.0)
        for s, acc in zip(segs, _conv_taps(segs, j, cbuf_ref, cw_ref, zero)):
            conv_cols[s, j] = acc

    def conv_out(s):
        cols = [conv_cols.pop((s, j)) for j in range(D_CONV // LANES)]
        mix_ref[_chunk_rows(s), 0:D_CONV] = _conv_norm(cols, cb_ref, lng_ref, lnb_ref).astype(BF16)

    side = [functools.partial(gla_prep, ci) for ci in chunks]
    for g0 in range(0, n_seg, CONV_GROUP):
        segs = chunks[g0:g0 + CONV_GROUP]
        side += [functools.partial(gla_out, ci) for ci in segs]
        side += [functools.partial(conv_taps, segs, j) for j in range(D_CONV // LANES)]
        side += [functools.partial(conv_out, ci) for ci in segs]

    n_early = min(2, n_seg)
    for thunk in side[:n_early]:
        thunk()

    h2 = (_rms_rows(x1, n2g_ref[...]) * (1.0 + sc2) + sh2).astype(BF16)
    act, acc = {}, {}

    def gate_up(j):
        cols = slice(j * MXU_N, (j + 1) * MXU_N)
        gate = _dot(h2, wg_ref[:, cols])
        latest["ffn"] = gate[0:CHUNK, 0:LANES]
        act[j] = (_silu(gate) * _dot(h2, wu_ref[:, cols])).astype(BF16)

    def down(j):
        part = _dot(act.pop(j), wd_ref[j * MXU_N:(j + 1) * MXU_N, :])
        acc["v"] = part if j == 0 else acc["v"] + part

    ffn = [functools.partial(gate_up, 0)]
    for j in range(1, n_ff):
        ffn += [functools.partial(gate_up, j), functools.partial(down, j - 1)]
    ffn.append(functools.partial(down, n_ff - 1))
    _interleave(ffn, side[n_early:])
    s_ref[...] = state["s"]

    x2 = x1 + g2 * acc["v"]
    out_ref[...] = _rms_rows(x2, fg_ref[...])


def _pass2_call(x2d, qk, laf, v, g, glu, ob, mod, n2g, gng, fg, cw, cb, lng, lnb,
                w_out_bf, wg_bf, wu_bf, wd_bf, sf0, never):
    n_tok = x2d.shape[0]
    n_tiles = n_tok // TILE
    mixer = lambda s: (jnp.minimum(s, n_tiles - 1), 0)
    ffn = lambda s: (jnp.maximum(s - 1, 0), 0)
    const = lambda s: (0, 0)
    full = lambda a: pl.BlockSpec(a.shape, const)
    mixer_spec = lambda w: pl.BlockSpec((TILE, w), mixer)
    return pl.pallas_call(
        _pass2_kernel,
        grid=(n_tiles + 1,),
        in_specs=[pl.BlockSpec((TILE, D_MODEL), ffn), mixer_spec(2 * D_GLA_K), mixer_spec(D_GLA_K),
                  mixer_spec(D_GLA_V), mixer_spec(D_GLA_V), mixer_spec(D_CONV), mixer_spec(D_GLA_V),
                  full(mod), full(n2g), full(gng), full(fg), full(cw), full(cb), full(lng), full(lnb),
                  full(w_out_bf), full(wg_bf), full(wu_bf), full(wd_bf), full(sf0),
                  full(never)],
        out_specs=pl.BlockSpec((TILE, D_MODEL), ffn),
        out_shape=jax.ShapeDtypeStruct((n_tok, D_MODEL), F32),
        scratch_shapes=[pltpu.VMEM((N_HEADS * HEAD_K, HEAD_V), F32),
                        pltpu.VMEM((TILE, D_CONV + D_GLA_V), BF16),
                        pltpu.VMEM((D_CONV // LANES, (TILE // CHUNK) * SEG_PITCH, LANES), F32)],
        compiler_params=pltpu.CompilerParams(dimension_semantics=("arbitrary",),
                                             vmem_limit_bytes=VMEM_LIMIT),
        name="pass2_gla_fwd_conv_merge_ffn",
    )(x2d, qk, laf, v, g, glu, ob, mod, n2g, gng, fg, cw, cb, lng, lnb,
      w_out_bf, wg_bf, wu_bf, wd_bf, sf0, never)


def kernel(x, c, ctx, c_ctx, w_mod, b_mod, norm1_g, norm2_g, w_in, conv_w, conv_b, conv_ln_g,
           conv_ln_b, w_a2_f, b_a_f, w_a2_b, b_a_b, gla_norm_g, w_out, w_gate, w_up, w_down, final_g):
    bsz, n_lat, d = x.shape
    assert bsz == 1 and d == D_MODEL and n_lat % TILE == 0 and ctx.shape[1] % CHUNK == 0
    assert (TILE // CHUNK) % CONV_GROUP == 0
    assert w_mod.shape[0] == 1, "single layer"
    row = lambda a: a.reshape(1, -1)

    ct = jnp.concatenate([c.reshape(D_MODEL, 1), c_ctx.reshape(D_MODEL, 1)], axis=1)
    mod = _mod_call(ct, w_mod[0], row(b_mod[0]))

    wa2f, baf = w_a2_f[0], row(b_a_f[0])
    wa2b, bab = w_a2_b[0], row(b_a_b[0])
    n1g = row(norm1_g[0])

    sf0, sb0, w_in_bf = _ctx_call(ctx[0], mod, n1g, w_in[0].T, wa2f, baf, wa2b, bab)

    x2d = x[0]
    qk, laf, v, g, glu, ob, wo_bf, wg_bf, wu_bf, wd_bf = _pass1_call(
        x2d, mod, n1g, w_in_bf, wa2f, baf, wa2b, bab, sb0, w_out[0], w_gate[0], w_up[0], w_down[0])
    unslab = lambda w: w.reshape(-1, w.shape[-1])

    out = _pass2_call(
        x2d, qk, laf, v, g, glu, ob, mod, row(norm2_g[0]), row(gla_norm_g[0]), row(final_g),
        conv_w[0], row(conv_b[0]), row(conv_ln_g[0]), row(conv_ln_b[0]),
        unslab(wo_bf), unslab(wg_bf), unslab(wu_bf), unslab(wd_bf), sf0,
        jnp.zeros((CHUNK, LANES), jnp.int32))
    return out.reshape(bsz, n_lat, d)
```

```python
import functools

import jax
import jax.numpy as jnp
from jax import lax
from jax.experimental import pallas as pl
from jax.experimental.pallas import tpu as pltpu

D_MODEL = 1024
D_CONV = 512
CONV_WIDTH = 31
CONV_PAD = 15
N_HEADS = 4
HEAD_K = 64
HEAD_V = 128
D_GLA_K = N_HEADS * HEAD_K
D_GLA_V = N_HEADS * HEAD_V
GLA_RANK = 16
GLA_TAU = 16.0
CHUNK = 64
EPS = 1e-6
D_FF = 2816

C_Q = 2 * D_CONV
C_K = C_Q + D_GLA_K
C_V = C_K + D_GLA_K
C_G = C_V + D_GLA_V
C_R = C_G + D_GLA_V
D_IN = C_R + 2 * GLA_RANK

LANES = 128
MXU_N = 256
SEG_PITCH = 96
SEG_LEAD = 16
CONV_GROUP = 2
TILE = 256
VMEM_LIMIT = 56 * 1024 * 1024

F32 = jnp.float32
BF16 = jnp.bfloat16


def _dot(a, b):
    return jnp.dot(a, b, preferred_element_type=F32)


def _dot_nt(a, b):
    return lax.dot_general(a, b, (((1,), (1,)), ((), ())), preferred_element_type=F32)


def _sigmoid(x):
    return 1.0 / (1.0 + jnp.exp(-x))


def _silu(x):
    return x * _sigmoid(x)


def _log_sigmoid(z):
    return jnp.minimum(z, 0.0) - jnp.log(1.0 + jnp.exp(-jnp.abs(z)))


def _rms_rows(x, g):
    ms = jnp.mean(x * x, axis=-1, keepdims=True)
    return x * lax.rsqrt(ms + EPS) * g


def _interleave(main, side):
    n_main, n_side = len(main), len(side)
    done = 0
    for i, thunk in enumerate(main):
        thunk()
        want = ((i + 1) * n_side) // n_main
        while done < want:
            side[done]()
            done += 1


def _gla_consts(forward):
    r = lax.broadcasted_iota(jnp.int32, (CHUNK, CHUNK), 0)
    c = lax.broadcasted_iota(jnp.int32, (CHUNK, CHUNK), 1)
    tri = (c <= r) if forward else (c >= r)
    tri_bf = jnp.where(tri, 1.0, 0.0).astype(BF16)
    rr = lax.broadcasted_iota(jnp.int32, (N_HEADS * CHUNK, LANES), 0) % CHUNK
    cc = lax.broadcasted_iota(jnp.int32, (N_HEADS * CHUNK, LANES), 1)
    keep = ((cc <= rr) if forward else (cc >= rr)) & (cc < CHUNK)
    lane = lax.broadcasted_iota(jnp.int32, (CHUNK, D_GLA_K), 1)
    head_masks = [(lane >= h * HEAD_K) & (lane < (h + 1) * HEAD_K) for h in range(N_HEADS)]
    return tri_bf, keep, head_masks


def _gla_cumsum(la, consts):
    tri_bf = consts[0]
    la_hi = la.astype(BF16)
    la_lo = (la - la_hi.astype(F32)).astype(BF16)
    return _dot(tri_bf, la_hi) + _dot(tri_bf, la_lo)


def _gla_prep(q, k, v_bf, b, consts, forward, with_output):
    _, keep, head_masks = consts
    b_last = b[CHUNK - 1:CHUNK, :] if forward else b[0:1, :]
    k_w = k * jnp.exp(b_last - b)
    e_last = jnp.exp(b_last)

    tb = jnp.concatenate(
        [k_w, jnp.broadcast_to(e_last, (8, D_GLA_K)), jnp.zeros((CHUNK - 8, D_GLA_K), F32)], axis=0)
    tbt = tb.T
    dec_col = tbt[:, CHUNK:CHUNK + 1]
    kwt_bf = tbt.astype(BF16)
    zeros_v = jnp.zeros((CHUNK, HEAD_V), BF16)
    v_pads = [jnp.concatenate([v_bf[:, h * HEAD_V:(h + 1) * HEAD_V], zeros_v], axis=0)
              for h in range(N_HEADS)]
    kv = jnp.concatenate([_dot(kwt_bf[h * CHUNK:(h + 1) * CHUNK], v_pads[h])
                          for h in range(N_HEADS)], axis=0)
    if not with_output:
        return dec_col, kv, None
    q_t = (q * jnp.exp(b) * (HEAD_K ** -0.5)).astype(BF16)
    k_t = (k * jnp.exp(-b)).astype(BF16)
    qm = jnp.concatenate([jnp.where(m, q_t, jnp.zeros_like(q_t)) for m in head_masks], axis=0)
    kt_pad = jnp.concatenate([k_t, jnp.zeros_like(k_t)], axis=0)
    attn = _dot_nt(qm, kt_pad)
    p_bf = jnp.where(keep, attn, 0.0).astype(BF16)
    return dec_col, kv, (qm, p_bf, v_pads)


def _gla_out(ops, s_enter):
    qm, p_bf, v_pads = ops
    o_inter = _dot(qm, s_enter.astype(BF16))
    outs = []
    for h in range(N_HEADS):
        rows = slice(h * CHUNK, (h + 1) * CHUNK)
        outs.append(_dot(p_bf[rows], v_pads[h]) + o_inter[rows])
    return jnp.concatenate(outs, axis=1)


def _log_decay(r, w_a2, b_a):
    z = _dot(r.astype(BF16), w_a2.astype(BF16)) + b_a
    return _log_sigmoid(z) * (1.0 / GLA_TAU)


def _chunk_rows(ci):
    return slice(ci * CHUNK, (ci + 1) * CHUNK)


def _mod_kernel(ct_ref, w_ref, b_ref, o_ref):
    s = _silu(ct_ref[...])
    w = w_ref[...]
    for j in range(2):
        o_ref[j:j + 1, :] = jnp.sum(w * s[:, j:j + 1], axis=0, keepdims=True) + b_ref[...]


def _mod_call(ct, w_mod, b_mod):
    tn = 768
    n = w_mod.shape[1]
    return pl.pallas_call(
        _mod_kernel,
        grid=(n // tn,),
        in_specs=[pl.BlockSpec((D_MODEL, 2), lambda j: (0, 0)),
                  pl.BlockSpec((D_MODEL, tn), lambda j: (0, j)),
                  pl.BlockSpec((1, tn), lambda j: (0, j))],
        out_specs=pl.BlockSpec((2, tn), lambda j: (0, j)),
        out_shape=jax.ShapeDtypeStruct((2, n), F32),
        compiler_params=pltpu.CompilerParams(dimension_semantics=("arbitrary",),
                                             vmem_limit_bytes=VMEM_LIMIT),
        name="adaln_mod",
    )(ct, w_mod, b_mod)


def _ctx_kernel(ctx_ref, mod_ref, n1g_ref, w_in_t_ref, wa2f_ref, baf_ref, wa2b_ref, bab_ref,
                sf_ref, sb_ref, w_in_ref):
    n_full = D_IN // MXU_N
    for j in range(n_full):
        cols = slice(j * MXU_N, (j + 1) * MXU_N)
        w_in_ref[:, cols] = w_in_t_ref[cols, :].T.astype(BF16)
    tail = w_in_t_ref[D_IN - LANES:D_IN, :].T
    w_in_ref[:, n_full * MXU_N:D_IN] = tail[:, LANES - (D_IN - n_full * MXU_N):].astype(BF16)
    n_chunks = ctx_ref.shape[0] // CHUNK
    sh = mod_ref[1:2, 0:D_MODEL]
    sc = mod_ref[1:2, D_MODEL:2 * D_MODEL]
    h = (_rms_rows(ctx_ref[...], n1g_ref[...]) * (1.0 + sc) + sh).astype(BF16)
    kv = _dot(h, w_in_ref[:, C_K:C_G])
    r = _dot(h, w_in_ref[:, C_R:D_IN])
    k = kv[:, :D_GLA_K]
    v_bf = kv[:, D_GLA_K:].astype(BF16)
    la_f = _log_decay(r[:, :GLA_RANK], wa2f_ref[...], baf_ref[...])
    la_b = _log_decay(r[:, GLA_RANK:], wa2b_ref[...], bab_ref[...])
    for forward, la, out_ref in ((True, la_f, sf_ref), (False, la_b, sb_ref)):
        consts = _gla_consts(forward)
        order = list(range(n_chunks)) if forward else list(range(n_chunks - 1, -1, -1))
        b = {ci: _gla_cumsum(la[_chunk_rows(ci)], consts) for ci in order}
        s = jnp.zeros((N_HEADS * HEAD_K, HEAD_V), F32)
        for ci in order:
            rows = _chunk_rows(ci)
            dec_col, kv_c, _ = _gla_prep(None, k[rows], v_bf[rows], b[ci], consts, forward, False)
            s = dec_col * s + kv_c
        out_ref[...] = s


def _ctx_call(ctx2d, mod, n1g, w_in_t, wa2f, baf, wa2b, bab):
    st = jax.ShapeDtypeStruct((N_HEADS * HEAD_K, HEAD_V), F32)
    return pl.pallas_call(
        _ctx_kernel,
        out_shape=(st, st, jax.ShapeDtypeStruct(w_in_t.shape[::-1], BF16)),
        compiler_params=pltpu.CompilerParams(vmem_limit_bytes=VMEM_LIMIT),
        name="context_states",
    )(ctx2d, mod, n1g, w_in_t, wa2f, baf, wa2b, bab)


def _pass1_step(x_ref, mod_ref, n1g_ref, w_in_ref, wa2f_ref, baf_ref, wa2b_ref, bab_ref,
                qk_ref, laf_ref, v_ref, g_ref, glu_ref, ob_ref, s_ref, carry_w, carry_r):
    proj_blocks, gla_stages = [], []

    if carry_r is not None:
        n_seg = x_ref.shape[0] // CHUNK
        qk_r, v_r, lab_r = carry_r
        order = list(range(n_seg - 1, -1, -1))
        consts = _gla_consts(False)
        b = {ci: _gla_cumsum(lab_r[_chunk_rows(ci), :], consts) for ci in order}

    if carry_w is not None:
        qk_w, v_w, lab_w = carry_w
        sh1 = mod_ref[0:1, 0:D_MODEL]
        sc1 = mod_ref[0:1, D_MODEL:2 * D_MODEL]
        h = (_rms_rows(x_ref[...], n1g_ref[...]) * (1.0 + sc1) + sh1).astype(BF16)

        def proj(lo, hi):
            return _dot(h, w_in_ref[:, lo:hi])

        def glu_block(j):
            lo, hi = j * MXU_N, (j + 1) * MXU_N
            glu_ref[:, lo:hi] = proj(lo, hi) * _sigmoid(proj(D_CONV + lo, D_CONV + hi))

        def qk_block(j):
            lo, hi = j * MXU_N, (j + 1) * MXU_N
            blk = proj(C_Q + lo, C_Q + hi)
            qk_ref[:, lo:hi] = blk
            qk_w[:, lo:hi] = blk

        def v_block(j):
            lo, hi = j * MXU_N, (j + 1) * MXU_N
            blk = proj(C_V + lo, C_V + hi).astype(BF16)
            v_ref[:, lo:hi] = blk
            v_w[:, lo:hi] = blk

        def g_block(j):
            lo, hi = j * MXU_N, (j + 1) * MXU_N
            g_ref[:, lo:hi] = proj(C_G + lo, C_G + hi).astype(BF16)

        def decay_block():
            r = proj(C_R, D_IN)
            laf_ref[...] = _log_decay(r[:, :GLA_RANK], wa2f_ref[...], baf_ref[...])
            lab_w[...] = _log_decay(r[:, GLA_RANK:], wa2b_ref[...], bab_ref[...])

        proj_blocks = ([functools.partial(glu_block, j) for j in range(D_CONV // MXU_N)]
                       + [functools.partial(qk_block, j) for j in range(2 * D_GLA_K // MXU_N)]
                       + [functools.partial(v_block, j) for j in range(D_GLA_V // MXU_N)]
                       + [functools.partial(g_block, j) for j in range(D_GLA_V // MXU_N)]
                       + [decay_block])

    if carry_r is not None:
        state = {"s": s_ref[...]}
        prep, s_enter = {}, {}

        def gla_prep(ci):
            rows = _chunk_rows(ci)
            qk_c = qk_r[rows, :]
            dec_col, kv, ops = _gla_prep(qk_c[:, :D_GLA_K], qk_c[:, D_GLA_K:], v_r[rows, :], b[ci],
                                         consts, False, True)
            prep[ci] = ops
            s_enter[ci] = state["s"]
            state["s"] = dec_col * state["s"] + kv

        def gla_out(ci):
            ob_ref[_chunk_rows(ci), :] = _gla_out(prep[ci], s_enter[ci]).astype(BF16)

        gla_stages = ([functools.partial(gla_prep, ci) for ci in order]
                      + [functools.partial(gla_out, ci) for ci in order])

    if proj_blocks and gla_stages:
        _interleave(proj_blocks, gla_stages)
    else:
        for thunk in proj_blocks + gla_stages:
            thunk()
    if carry_r is not None:
        s_ref[...] = state["s"]


def _pass1_kernel(n_steps,
                  x_ref, mod_ref, n1g_ref, w_in_ref, wa2f_ref, baf_ref, wa2b_ref, bab_ref, sb0_ref,
                  wo_f32_ref, wg_f32_ref, wu_f32_ref, wd_f32_ref,
                  qk_ref, laf_ref, v_ref, g_ref, glu_ref, ob_ref,
                  wo_bf_ref, wg_bf_ref, wu_bf_ref, wd_bf_ref,
                  s_ref, qk_a, v_a, lab_a, qk_b, v_b, lab_b):
    step = pl.program_id(0)
    last = n_steps - 1

    for src, dst in ((wo_f32_ref, wo_bf_ref), (wg_f32_ref, wg_bf_ref), (wu_f32_ref, wu_bf_ref),
                     (wd_f32_ref, wd_bf_ref)):
        dst[...] = src[...].astype(BF16)
    bufs = ((qk_a, v_a, lab_a), (qk_b, v_b, lab_b))

    run = functools.partial(_pass1_step, x_ref, mod_ref, n1g_ref, w_in_ref, wa2f_ref, baf_ref,
                            wa2b_ref, bab_ref, qk_ref, laf_ref, v_ref, g_ref, glu_ref, ob_ref, s_ref)

    @pl.when(step == 0)
    def _():
        s_ref[...] = sb0_ref[...]
        run(bufs[0], None)

    for parity in (0, 1):
        @pl.when((step > 0) & (step < last) & (step % 2 == parity))
        def _():
            run(bufs[parity], bufs[1 - parity])

    @pl.when(step == last)
    def _():
        run(None, bufs[(last - 1) % 2])


def _slabs(w, n_slabs):
    rows, cols = w.shape
    assert rows % (n_slabs * 16) == 0
    return w.reshape(n_slabs, rows // n_slabs, cols)


def _pass1_call(x2d, mod, n1g, w_in_bf, wa2f, baf, wa2b, bab, sb0, w_out, w_gate, w_up, w_down):
    n_tok = x2d.shape[0]
    n_tiles = n_tok // TILE
    cast_in = [_slabs(w, n) for w, n in ((w_out, 32), (w_gate, 32), (w_up, 32), (w_down, 44))]
    assert all(w.shape[0] <= n_tiles for w in cast_in)
    cast_specs = [pl.BlockSpec((1,) + w.shape[1:],
                               functools.partial(lambda n, s: (jnp.minimum(s, n - 1), 0, 0), w.shape[0]))
                  for w in cast_in]
    proj = lambda s: (n_tiles - 1 - jnp.minimum(s, n_tiles - 1), 0)
    gla = lambda s: (n_tiles - 1 - jnp.maximum(s - 1, 0), 0)
    const = lambda s: (0, 0)
    full = lambda a: pl.BlockSpec(a.shape, const)
    out_shape = (
        jax.ShapeDtypeStruct((n_tok, 2 * D_GLA_K), F32),
        jax.ShapeDtypeStruct((n_tok, D_GLA_K), F32),
        jax.ShapeDtypeStruct((n_tok, D_GLA_V), BF16),
        jax.ShapeDtypeStruct((n_tok, D_GLA_V), BF16),
        jax.ShapeDtypeStruct((n_tok, D_CONV), F32),
        jax.ShapeDtypeStruct((n_tok, D_GLA_V), BF16),
    )
    carry = [pltpu.VMEM((TILE, 2 * D_GLA_K), F32), pltpu.VMEM((TILE, D_GLA_V), BF16),
             pltpu.VMEM((TILE, D_GLA_K), F32)]
    return pl.pallas_call(
        functools.partial(_pass1_kernel, n_tiles + 1),
        grid=(n_tiles + 1,),
        in_specs=[pl.BlockSpec((TILE, D_MODEL), proj), full(mod), full(n1g), full(w_in_bf),
                  full(wa2f), full(baf), full(wa2b), full(bab), full(sb0)] + cast_specs,
        out_specs=(pl.BlockSpec((TILE, 2 * D_GLA_K), proj), pl.BlockSpec((TILE, D_GLA_K), proj),
                   pl.BlockSpec((TILE, D_GLA_V), proj), pl.BlockSpec((TILE, D_GLA_V), proj),
                   pl.BlockSpec((TILE, D_CONV), proj), pl.BlockSpec((TILE, D_GLA_V), gla))
        + tuple(cast_specs),
        out_shape=out_shape + tuple(jax.ShapeDtypeStruct(w.shape, BF16) for w in cast_in),
        scratch_shapes=[pltpu.VMEM((N_HEADS * HEAD_K, HEAD_V), F32)] + carry + carry,
        compiler_params=pltpu.CompilerParams(dimension_semantics=("arbitrary",),
                                             vmem_limit_bytes=VMEM_LIMIT),
        name="pass1_project_gla_bwd",
    )(x2d, mod, n1g, w_in_bf, wa2f, baf, wa2b, bab, sb0, *cast_in)


def _conv_stage_in(glu_ref, cbuf_ref):
    n_seg = glu_ref.shape[0] // CHUNK
    for j in range(D_CONV // LANES):
        for s in range(n_seg):
            base = s * SEG_PITCH + SEG_LEAD
            cbuf_ref[j, base:base + CHUNK, :] = glu_ref[_chunk_rows(s), j * LANES:(j + 1) * LANES]


def _conv_taps(segs, j, cbuf_ref, cw_ref, zero):
    accs = [zero for _ in segs]
    for t in range(CONV_WIDTH):
        w = cw_ref[t:t + 1, j * LANES:(j + 1) * LANES]
        for i, s in enumerate(segs):
            start = s * SEG_PITCH + SEG_LEAD - CONV_PAD + t
            accs[i] = accs[i] + cbuf_ref[j, start:start + CHUNK, :] * w
    return accs


def _conv_norm(cols, cb_ref, lng_ref, lnb_ref):
    y = jnp.concatenate(cols, axis=1) + cb_ref[...]
    mu = jnp.mean(y, axis=-1, keepdims=True)
    yc = y - mu
    var = jnp.mean(yc * yc, axis=-1, keepdims=True)
    return _silu(yc * lax.rsqrt(var + EPS) * lng_ref[...] + lnb_ref[...])


def _pass2_step(do_ffn, do_mix, x_ref, qk_ref, laf_ref, v_ref, g_ref, glu_ref, ob_ref, mod_ref,
                n2g_ref, gng_ref, fg_ref, cw_ref, cb_ref, lng_ref, lnb_ref,
                w_out_ref, wg_ref, wu_ref, wd_ref, never_ref, out_ref, s_ref, mix_ref, cbuf_ref):
    n_seg = x_ref.shape[0] // CHUNK
    n_ff = D_FF // MXU_N
    side, ffn = [], []
    latest = {}

    if do_mix:
        chunks = list(range(n_seg))
        consts = _gla_consts(True)
        gng = gng_ref[...]
        _conv_stage_in(glu_ref, cbuf_ref)
        b = {ci: _gla_cumsum(laf_ref[_chunk_rows(ci), :], consts) for ci in chunks}

    if do_ffn:
        g1 = mod_ref[0:1, 2 * D_MODEL:3 * D_MODEL]
        sh2 = mod_ref[0:1, 3 * D_MODEL:4 * D_MODEL]
        sc2 = mod_ref[0:1, 4 * D_MODEL:5 * D_MODEL]
        g2 = mod_ref[0:1, 5 * D_MODEL:6 * D_MODEL]
        x1 = x_ref[...] + g1 * _dot(mix_ref[...], w_out_ref[...])

    if do_mix:
        state = {"s": s_ref[...]}
        prep, s_enter = {}, {}

        def gla_prep(ci):
            rows = _chunk_rows(ci)
            qk_c = qk_ref[rows, :]
            dec_col, kv, ops = _gla_prep(qk_c[:, :D_GLA_K], qk_c[:, D_GLA_K:], v_ref[rows, :], b[ci],
                                         consts, True, True)
            prep[ci] = ops
            s_enter[ci] = state["s"]
            state["s"] = dec_col * state["s"] + kv

        def gla_out(ci):
            rows = _chunk_rows(ci)
            o = _gla_out(prep[ci], s_enter[ci]) + ob_ref[rows, :].astype(F32)
            gate = _silu(g_ref[rows, :].astype(F32))
            for h in range(N_HEADS):
                cols = slice(h * HEAD_V, (h + 1) * HEAD_V)
                mix_ref[rows, D_CONV + h * HEAD_V:D_CONV + (h + 1) * HEAD_V] = (
                    _rms_rows(o[:, cols], gng) * gate[:, cols]).astype(BF16)

        conv_cols = {}
        never = never_ref[...] != 0

        def conv_taps(segs, j):
            zero = (jnp.where(never, latest["ffn"], 0.0) if do_ffn
                    else jnp.zeros((CHUNK, LANES), F32))
            for s, acc in zip(segs, _conv_taps(segs, j, cbuf_ref, cw_ref, zero)):
                conv_cols[s, j] = acc

        def conv_out(s):
            cols = [conv_cols.pop((s, j)) for j in range(D_CONV // LANES)]
            mix_ref[_chunk_rows(s), 0:D_CONV] = _conv_norm(cols, cb_ref, lng_ref, lnb_ref).astype(BF16)

        side = [functools.partial(gla_prep, ci) for ci in chunks]
        for g0 in range(0, n_seg, CONV_GROUP):
            segs = chunks[g0:g0 + CONV_GROUP]
            side += [functools.partial(gla_out, ci) for ci in segs]
            side += [functools.partial(conv_taps, segs, j) for j in range(D_CONV // LANES)]
            side += [functools.partial(conv_out, ci) for ci in segs]

    n_early = min(2, len(side)) if do_ffn else len(side)
    for thunk in side[:n_early]:
        thunk()

    if do_ffn:
        h2 = (_rms_rows(x1, n2g_ref[...]) * (1.0 + sc2) + sh2).astype(BF16)
        act, acc = {}, {}

        def gate_up(j):
            cols = slice(j * MXU_N, (j + 1) * MXU_N)
            gate = _dot(h2, wg_ref[:, cols])
            latest["ffn"] = gate[0:CHUNK, 0:LANES]
            act[j] = (_silu(gate) * _dot(h2, wu_ref[:, cols])).astype(BF16)

        def down(j):
            part = _dot(act.pop(j), wd_ref[j * MXU_N:(j + 1) * MXU_N, :])
            acc["v"] = part if j == 0 else acc["v"] + part

        ffn = [functools.partial(gate_up, 0)]
        for j in range(1, n_ff):
            ffn += [functools.partial(gate_up, j), functools.partial(down, j - 1)]
        ffn.append(functools.partial(down, n_ff - 1))
        _interleave(ffn, side[n_early:])
    if do_mix:
        s_ref[...] = state["s"]
    if do_ffn:
        x2 = x1 + g2 * acc["v"]
        out_ref[...] = _rms_rows(x2, fg_ref[...])


def _pass2_kernel(n_steps,
                  x_ref, qk_ref, laf_ref, v_ref, g_ref, glu_ref, ob_ref, mod_ref, n2g_ref,
                  gng_ref, fg_ref, cw_ref, cb_ref, lng_ref, lnb_ref,
                  w_out_ref, wg_ref, wu_ref, wd_ref, sf0_ref, never_ref,
                  out_ref, s_ref, mix_ref, cbuf_ref):
    step = pl.program_id(0)
    last = n_steps - 1
    run = functools.partial(_pass2_step, x_ref=x_ref, qk_ref=qk_ref, laf_ref=laf_ref, v_ref=v_ref,
                            g_ref=g_ref, glu_ref=glu_ref, ob_ref=ob_ref, mod_ref=mod_ref,
                            n2g_ref=n2g_ref, gng_ref=gng_ref, fg_ref=fg_ref, cw_ref=cw_ref,
                            cb_ref=cb_ref, lng_ref=lng_ref, lnb_ref=lnb_ref, w_out_ref=w_out_ref,
                            wg_ref=wg_ref, wu_ref=wu_ref, wd_ref=wd_ref, never_ref=never_ref,
                            out_ref=out_ref, s_ref=s_ref, mix_ref=mix_ref, cbuf_ref=cbuf_ref)

    @pl.when(step == 0)
    def _():
        s_ref[...] = sf0_ref[...]
        cbuf_ref[...] = jnp.zeros(cbuf_ref.shape, F32)
        run(False, True)

    @pl.when((step > 0) & (step < last))
    def _():
        run(True, True)

    @pl.when(step == last)
    def _():
        run(True, False)


def _pass2_call(x2d, qk, laf, v, g, glu, ob, mod, n2g, gng, fg, cw, cb, lng, lnb,
                w_out_bf, wg_bf, wu_bf, wd_bf, sf0, never):
    n_tok = x2d.shape[0]
    n_tiles = n_tok // TILE
    mixer = lambda s: (jnp.minimum(s, n_tiles - 1), 0)
    ffn = lambda s: (jnp.maximum(s - 1, 0), 0)
    const = lambda s: (0, 0)
    full = lambda a: pl.BlockSpec(a.shape, const)
    mixer_spec = lambda w: pl.BlockSpec((TILE, w), mixer)
    return pl.pallas_call(
        functools.partial(_pass2_kernel, n_tiles + 1),
        grid=(n_tiles + 1,),
        in_specs=[pl.BlockSpec((TILE, D_MODEL), ffn), mixer_spec(2 * D_GLA_K), mixer_spec(D_GLA_K),
                  mixer_spec(D_GLA_V), mixer_spec(D_GLA_V), mixer_spec(D_CONV), mixer_spec(D_GLA_V),
                  full(mod), full(n2g), full(gng), full(fg), full(cw), full(cb), full(lng), full(lnb),
                  full(w_out_bf), full(wg_bf), full(wu_bf), full(wd_bf), full(sf0),
                  full(never)],
        out_specs=pl.BlockSpec((TILE, D_MODEL), ffn),
        out_shape=jax.ShapeDtypeStruct((n_tok, D_MODEL), F32),
        scratch_shapes=[pltpu.VMEM((N_HEADS * HEAD_K, HEAD_V), F32),
                        pltpu.VMEM((TILE, D_CONV + D_GLA_V), BF16),
                        pltpu.VMEM((D_CONV // LANES, (TILE // CHUNK) * SEG_PITCH, LANES), F32)],
        compiler_params=pltpu.CompilerParams(dimension_semantics=("arbitrary",),
                                             vmem_limit_bytes=VMEM_LIMIT),
        name="pass2_gla_fwd_conv_merge_ffn",
    )(x2d, qk, laf, v, g, glu, ob, mod, n2g, gng, fg, cw, cb, lng, lnb,
      w_out_bf, wg_bf, wu_bf, wd_bf, sf0, never)


def kernel(x, c, ctx, c_ctx, w_mod, b_mod, norm1_g, norm2_g, w_in, conv_w, conv_b, conv_ln_g,
           conv_ln_b, w_a2_f, b_a_f, w_a2_b, b_a_b, gla_norm_g, w_out, w_gate, w_up, w_down, final_g):
    bsz, n_lat, d = x.shape
    assert bsz == 1 and d == D_MODEL and n_lat % TILE == 0 and ctx.shape[1] % CHUNK == 0
    assert (TILE // CHUNK) % CONV_GROUP == 0
    assert w_mod.shape[0] == 1, "single layer"
    row = lambda a: a.reshape(1, -1)

    ct = jnp.concatenate([c.reshape(D_MODEL, 1), c_ctx.reshape(D_MODEL, 1)], axis=1)
    mod = _mod_call(ct, w_mod[0], row(b_mod[0]))

    wa2f, baf = w_a2_f[0], row(b_a_f[0])
    wa2b, bab = w_a2_b[0], row(b_a_b[0])
    n1g = row(norm1_g[0])

    sf0, sb0, w_in_bf = _ctx_call(ctx[0], mod, n1g, w_in[0].T, wa2f, baf, wa2b, bab)

    x2d = x[0]
    qk, laf, v, g, glu, ob, wo_bf, wg_bf, wu_bf, wd_bf = _pass1_call(
        x2d, mod, n1g, w_in_bf, wa2f, baf, wa2b, bab, sb0, w_out[0], w_gate[0], w_up[0], w_down[0])
    unslab = lambda w: w.reshape(-1, w.shape[-1])

    out = _pass2_call(
        x2d, qk, laf, v, g, glu, ob, mod, row(norm2_g[0]), row(gla_norm_g[0]), row(final_g),
        conv_w[0], row(conv_b[0]), row(conv_ln_g[0]), row(conv_ln_b[0]),
        unslab(wo_bf), unslab(wg_bf), unslab(wu_bf), unslab(wd_bf), sf0,
        jnp.zeros((CHUNK, LANES), jnp.int32))
    return out.reshape(bsz, n_lat, d)
```

```python
import functools

import jax
import jax.numpy as jnp
from jax import lax
from jax.experimental import pallas as pl
from jax.experimental.pallas import tpu as pltpu

D_MODEL = 1024
D_CONV = 512
CONV_WIDTH = 31
CONV_PAD = 15
N_HEADS = 4
HEAD_K = 64
HEAD_V = 128
D_GLA_K = N_HEADS * HEAD_K
D_GLA_V = N_HEADS * HEAD_V
GLA_RANK = 16
GLA_TAU = 16.0
CHUNK = 64
EPS = 1e-6
D_FF = 2816

C_Q = 2 * D_CONV
C_K = C_Q + D_GLA_K
C_V = C_K + D_GLA_K
C_G = C_V + D_GLA_V
C_R = C_G + D_GLA_V
D_IN = C_R + 2 * GLA_RANK

LANES = 128
MXU_N = 256
SEG_PITCH = 96
SEG_LEAD = 16
CONV_GROUP = 2
TILE = 256
VMEM_LIMIT = 56 * 1024 * 1024

F32 = jnp.float32
BF16 = jnp.bfloat16


def _dot(a, b):
    return jnp.dot(a, b, preferred_element_type=F32)


def _dot_nt(a, b):
    return lax.dot_general(a, b, (((1,), (1,)), ((), ())), preferred_element_type=F32)


def _sigmoid(x):
    return 1.0 / (1.0 + jnp.exp(-x))


def _silu(x):
    return x * _sigmoid(x)


def _log_sigmoid(z):
    return jnp.minimum(z, 0.0) - jnp.log(1.0 + jnp.exp(-jnp.abs(z)))


def _rms_rows(x, g):
    ms = jnp.mean(x * x, axis=-1, keepdims=True)
    return x * lax.rsqrt(ms + EPS) * g


def _interleave(main, side):
    n_main, n_side = len(main), len(side)
    done = 0
    for i, thunk in enumerate(main):
        thunk()
        want = ((i + 1) * n_side) // n_main
        while done < want:
            side[done]()
            done += 1


def _gla_consts(forward):
    r = lax.broadcasted_iota(jnp.int32, (CHUNK, CHUNK), 0)
    c = lax.broadcasted_iota(jnp.int32, (CHUNK, CHUNK), 1)
    tri = (c <= r) if forward else (c >= r)
    tri_bf = jnp.where(tri, 1.0, 0.0).astype(BF16)
    rr = lax.broadcasted_iota(jnp.int32, (N_HEADS * CHUNK, LANES), 0) % CHUNK
    cc = lax.broadcasted_iota(jnp.int32, (N_HEADS * CHUNK, LANES), 1)
    keep = ((cc <= rr) if forward else (cc >= rr)) & (cc < CHUNK)
    lane = lax.broadcasted_iota(jnp.int32, (CHUNK, D_GLA_K), 1)
    head_masks = [(lane >= h * HEAD_K) & (lane < (h + 1) * HEAD_K) for h in range(N_HEADS)]
    return tri_bf, keep, head_masks


def _gla_cumsum(la, consts):
    tri_bf = consts[0]
    la_hi = la.astype(BF16)
    la_lo = (la - la_hi.astype(F32)).astype(BF16)
    return _dot(tri_bf, la_hi) + _dot(tri_bf, la_lo)


def _gla_prep(q, k, v_bf, b, consts, forward, with_output):
    _, keep, head_masks = consts
    b_last = b[CHUNK - 1:CHUNK, :] if forward else b[0:1, :]
    k_w = k * jnp.exp(b_last - b)
    e_last = jnp.exp(b_last)

    tb = jnp.concatenate(
        [k_w, jnp.broadcast_to(e_last, (8, D_GLA_K)), jnp.zeros((CHUNK - 8, D_GLA_K), F32)], axis=0)
    tbt = tb.T
    dec_col = tbt[:, CHUNK:CHUNK + 1]
    kwt_bf = tbt.astype(BF16)
    zeros_v = jnp.zeros((CHUNK, HEAD_V), BF16)
    v_pads = [jnp.concatenate([v_bf[:, h * HEAD_V:(h + 1) * HEAD_V], zeros_v], axis=0)
              for h in range(N_HEADS)]
    kv = jnp.concatenate([_dot(kwt_bf[h * CHUNK:(h + 1) * CHUNK], v_pads[h])
                          for h in range(N_HEADS)], axis=0)
    if not with_output:
        return dec_col, kv, None
    q_t = (q * jnp.exp(b) * (HEAD_K ** -0.5)).astype(BF16)
    k_t = (k * jnp.exp(-b)).astype(BF16)
    qm = jnp.concatenate([jnp.where(m, q_t, jnp.zeros_like(q_t)) for m in head_masks], axis=0)
    kt_pad = jnp.concatenate([k_t, jnp.zeros_like(k_t)], axis=0)
    attn = _dot_nt(qm, kt_pad)
    p_bf = jnp.where(keep, attn, 0.0).astype(BF16)
    return dec_col, kv, (qm, p_bf, v_pads)


def _gla_out(ops, s_enter):
    qm, p_bf, v_pads = ops
    o_inter = _dot(qm, s_enter.astype(BF16))
    outs = []
    for h in range(N_HEADS):
        rows = slice(h * CHUNK, (h + 1) * CHUNK)
        outs.append(_dot(p_bf[rows], v_pads[h]) + o_inter[rows])
    return jnp.concatenate(outs, axis=1)


def _log_decay(r, w_a2, b_a):
    z = _dot(r.astype(BF16), w_a2.astype(BF16)) + b_a
    return _log_sigmoid(z) * (1.0 / GLA_TAU)


def _chunk_rows(ci):
    return slice(ci * CHUNK, (ci + 1) * CHUNK)


def _mod_kernel(ct_ref, w_ref, b_ref, o_ref, s_ref):
    s = _silu(ct_ref[...])
    s_ref[...] = s
    w = w_ref[...]
    for j in range(2):
        o_ref[j:j + 1, :] = jnp.sum(w * s[:, j:j + 1], axis=0, keepdims=True) + b_ref[...]


def _mod_call(ct, w_mod, b_mod):
    tn = 512
    n = 2 * D_MODEL
    return pl.pallas_call(
        _mod_kernel,
        grid=(n // tn,),
        in_specs=[pl.BlockSpec((D_MODEL, 2), lambda j: (0, 0)),
                  pl.BlockSpec((D_MODEL, tn), lambda j: (0, j)),
                  pl.BlockSpec((1, tn), lambda j: (0, j))],
        out_specs=(pl.BlockSpec((2, tn), lambda j: (0, j)),
                   pl.BlockSpec((D_MODEL, 2), lambda j: (0, 0))),
        out_shape=(jax.ShapeDtypeStruct((2, n), F32), jax.ShapeDtypeStruct((D_MODEL, 2), F32)),
        compiler_params=pltpu.CompilerParams(dimension_semantics=("arbitrary",),
                                             vmem_limit_bytes=VMEM_LIMIT),
        name="adaln_mod",
    )(ct, w_mod, b_mod)


def _ctx_kernel(ctx_ref, mod_ref, n1g_ref, w_in_t_ref, wa2f_ref, baf_ref, wa2b_ref, bab_ref,
                sf_ref, sb_ref, w_in_ref):
    n_full = D_IN // MXU_N
    for j in range(n_full):
        cols = slice(j * MXU_N, (j + 1) * MXU_N)
        w_in_ref[:, cols] = w_in_t_ref[cols, :].T.astype(BF16)
    tail = w_in_t_ref[D_IN - LANES:D_IN, :].T
    w_in_ref[:, n_full * MXU_N:D_IN] = tail[:, LANES - (D_IN - n_full * MXU_N):].astype(BF16)
    n_chunks = ctx_ref.shape[0] // CHUNK
    sh = mod_ref[1:2, 0:D_MODEL]
    sc = mod_ref[1:2, D_MODEL:2 * D_MODEL]
    h = (_rms_rows(ctx_ref[...], n1g_ref[...]) * (1.0 + sc) + sh).astype(BF16)
    kv = _dot(h, w_in_ref[:, C_K:C_G])
    r = _dot(h, w_in_ref[:, C_R:D_IN])
    k = kv[:, :D_GLA_K]
    v_bf = kv[:, D_GLA_K:].astype(BF16)
    la_f = _log_decay(r[:, :GLA_RANK], wa2f_ref[...], baf_ref[...])
    la_b = _log_decay(r[:, GLA_RANK:], wa2b_ref[...], bab_ref[...])
    for forward, la, out_ref in ((True, la_f, sf_ref), (False, la_b, sb_ref)):
        consts = _gla_consts(forward)
        order = list(range(n_chunks)) if forward else list(range(n_chunks - 1, -1, -1))
        b = {ci: _gla_cumsum(la[_chunk_rows(ci)], consts) for ci in order}
        s = jnp.zeros((N_HEADS * HEAD_K, HEAD_V), F32)
        for ci in order:
            rows = _chunk_rows(ci)
            dec_col, kv_c, _ = _gla_prep(None, k[rows], v_bf[rows], b[ci], consts, forward, False)
            s = dec_col * s + kv_c
        out_ref[...] = s


def _ctx_call(ctx2d, mod, n1g, w_in_t, wa2f, baf, wa2b, bab):
    st = jax.ShapeDtypeStruct((N_HEADS * HEAD_K, HEAD_V), F32)
    return pl.pallas_call(
        _ctx_kernel,
        out_shape=(st, st, jax.ShapeDtypeStruct(w_in_t.shape[::-1], BF16)),
        compiler_params=pltpu.CompilerParams(vmem_limit_bytes=VMEM_LIMIT),
        name="context_states",
    )(ctx2d, mod, n1g, w_in_t, wa2f, baf, wa2b, bab)


def _pass1_step(x_ref, mod_ref, n1g_ref, w_in_ref, wa2f_ref, baf_ref, wa2b_ref, bab_ref,
                qk_ref, laf_ref, v_ref, g_ref, glu_ref, ob_ref, s_ref, carry_w, carry_r):
    proj_blocks, gla_stages = [], []

    if carry_r is not None:
        n_seg = x_ref.shape[0] // CHUNK
        qk_r, v_r, lab_r = carry_r
        order = list(range(n_seg - 1, -1, -1))
        consts = _gla_consts(False)
        b = {ci: _gla_cumsum(lab_r[_chunk_rows(ci), :], consts) for ci in order}

    if carry_w is not None:
        qk_w, v_w, lab_w = carry_w
        sh1 = mod_ref[0:1, 0:D_MODEL]
        sc1 = mod_ref[0:1, D_MODEL:2 * D_MODEL]
        h = (_rms_rows(x_ref[...], n1g_ref[...]) * (1.0 + sc1) + sh1).astype(BF16)

        def proj(lo, hi):
            return _dot(h, w_in_ref[:, lo:hi])

        def glu_block(j):
            lo, hi = j * MXU_N, (j + 1) * MXU_N
            glu_ref[:, lo:hi] = proj(lo, hi) * _sigmoid(proj(D_CONV + lo, D_CONV + hi))

        def qk_block(j):
            lo, hi = j * MXU_N, (j + 1) * MXU_N
            blk = proj(C_Q + lo, C_Q + hi)
            qk_ref[:, lo:hi] = blk
            qk_w[:, lo:hi] = blk

        def v_block(j):
            lo, hi = j * MXU_N, (j + 1) * MXU_N
            blk = proj(C_V + lo, C_V + hi).astype(BF16)
            v_ref[:, lo:hi] = blk
            v_w[:, lo:hi] = blk

        def g_block(j):
            lo, hi = j * MXU_N, (j + 1) * MXU_N
            g_ref[:, lo:hi] = proj(C_G + lo, C_G + hi).astype(BF16)

        low_rank = {}

        def rank_block():
            low_rank["r"] = proj(C_R, D_IN)

        def decay_block():
            r = low_rank.pop("r")
            laf_ref[...] = _log_decay(r[:, :GLA_RANK], wa2f_ref[...], baf_ref[...])
            lab_w[...] = _log_decay(r[:, GLA_RANK:], wa2b_ref[...], bab_ref[...])

        proj_blocks = ([rank_block]
                       + [functools.partial(glu_block, j) for j in range(D_CONV // MXU_N)]
                       + [decay_block]
                       + [functools.partial(qk_block, j) for j in range(2 * D_GLA_K // MXU_N)]
                       + [functools.partial(v_block, j) for j in range(D_GLA_V // MXU_N)]
                       + [functools.partial(g_block, j) for j in range(D_GLA_V // MXU_N)])

    if carry_r is not None:
        state = {"s": s_ref[...]}
        prep, s_enter = {}, {}

        def gla_prep(ci):
            rows = _chunk_rows(ci)
            qk_c = qk_r[rows, :]
            dec_col, kv, ops = _gla_prep(qk_c[:, :D_GLA_K], qk_c[:, D_GLA_K:], v_r[rows, :], b[ci],
                                         consts, False, True)
            prep[ci] = ops
            s_enter[ci] = state["s"]
            state["s"] = dec_col * state["s"] + kv

        def gla_out(ci):
            ob_ref[_chunk_rows(ci), :] = _gla_out(prep[ci], s_enter[ci]).astype(BF16)

        gla_stages = ([functools.partial(gla_prep, ci) for ci in order]
                      + [functools.partial(gla_out, ci) for ci in order])

    if proj_blocks and gla_stages:
        _interleave(proj_blocks, gla_stages)
    else:
        for thunk in proj_blocks + gla_stages:
            thunk()
    if carry_r is not None:
        s_ref[...] = state["s"]


def _pass1_kernel(n_steps, n_mod2,
                  x_ref, mod_ref, n1g_ref, w_in_ref, wa2f_ref, baf_ref, wa2b_ref, bab_ref, sb0_ref,
                  wo_f32_ref, wg_f32_ref, wu_f32_ref, wd_f32_ref, silu_ref, w_mod_ref, b_mod_ref,
                  qk_ref, laf_ref, v_ref, g_ref, glu_ref, ob_ref,
                  wo_bf_ref, wg_bf_ref, wu_bf_ref, wd_bf_ref, mod2_ref,
                  s_ref, qk_a, v_a, lab_a, qk_b, v_b, lab_b):
    step = pl.program_id(0)
    last = n_steps - 1

    @pl.when(step < n_mod2)
    def _():
        mod2_ref[...] = (jnp.sum(w_mod_ref[...] * silu_ref[:, 0:1], axis=0, keepdims=True)
                         + b_mod_ref[...])

    for src, dst in ((wo_f32_ref, wo_bf_ref), (wg_f32_ref, wg_bf_ref), (wu_f32_ref, wu_bf_ref),
                     (wd_f32_ref, wd_bf_ref)):
        dst[...] = src[...].astype(BF16)
    bufs = ((qk_a, v_a, lab_a), (qk_b, v_b, lab_b))

    run = functools.partial(_pass1_step, x_ref, mod_ref, n1g_ref, w_in_ref, wa2f_ref, baf_ref,
                            wa2b_ref, bab_ref, qk_ref, laf_ref, v_ref, g_ref, glu_ref, ob_ref, s_ref)

    @pl.when(step == 0)
    def _():
        s_ref[...] = sb0_ref[...]
        run(bufs[0], None)

    for parity in (0, 1):
        @pl.when((step > 0) & (step < last) & (step % 2 == parity))
        def _():
            run(bufs[parity], bufs[1 - parity])

    @pl.when(step == last)
    def _():
        run(None, bufs[(last - 1) % 2])


def _slabs(w, n_slabs):
    rows, cols = w.shape
    assert rows % (n_slabs * 16) == 0
    return w.reshape(n_slabs, rows // n_slabs, cols)


def _pass1_call(x2d, mod, n1g, w_in_bf, wa2f, baf, wa2b, bab, sb0, w_out, w_gate, w_up, w_down,
                silu_ct, w_mod, b_mod):
    n_tok = x2d.shape[0]
    n_tiles = n_tok // TILE
    mod2_lo = mod.shape[1] // D_MODEL
    mod2_blocks = w_mod.shape[1] // D_MODEL - mod2_lo
    assert mod2_blocks <= n_tiles
    mod2_in = lambda s: (0, mod2_lo + jnp.minimum(s, mod2_blocks - 1))
    mod2_out = lambda s: (0, jnp.minimum(s, mod2_blocks - 1))
    cast_in = [_slabs(w, n) for w, n in ((w_out, 32), (w_gate, 32), (w_up, 32), (w_down, 44))]
    assert all(w.shape[0] <= n_tiles for w in cast_in)
    cast_specs = [pl.BlockSpec((1,) + w.shape[1:],
                               functools.partial(lambda n, s: (jnp.minimum(s, n - 1), 0, 0), w.shape[0]))
                  for w in cast_in]
    proj = lambda s: (n_tiles - 1 - jnp.minimum(s, n_tiles - 1), 0)
    gla = lambda s: (n_tiles - 1 - jnp.maximum(s - 1, 0), 0)
    const = lambda s: (0, 0)
    full = lambda a: pl.BlockSpec(a.shape, const)
    out_shape = (
        jax.ShapeDtypeStruct((n_tok, 2 * D_GLA_K), F32),
        jax.ShapeDtypeStruct((n_tok, D_GLA_K), F32),
        jax.ShapeDtypeStruct((n_tok, D_GLA_V), BF16),
        jax.ShapeDtypeStruct((n_tok, D_GLA_V), BF16),
        jax.ShapeDtypeStruct((n_tok, D_CONV), F32),
        jax.ShapeDtypeStruct((n_tok, D_GLA_V), BF16),
    )
    carry = [pltpu.VMEM((TILE, 2 * D_GLA_K), F32), pltpu.VMEM((TILE, D_GLA_V), BF16),
             pltpu.VMEM((TILE, D_GLA_K), F32)]
    return pl.pallas_call(
        functools.partial(_pass1_kernel, n_tiles + 1, mod2_blocks),
        grid=(n_tiles + 1,),
        in_specs=[pl.BlockSpec((TILE, D_MODEL), proj), full(mod), full(n1g), full(w_in_bf),
                  full(wa2f), full(baf), full(wa2b), full(bab), full(sb0)] + cast_specs
        + [full(silu_ct), pl.BlockSpec((D_MODEL, D_MODEL), mod2_in), pl.BlockSpec((1, D_MODEL), mod2_in)],
        out_specs=(pl.BlockSpec((TILE, 2 * D_GLA_K), proj), pl.BlockSpec((TILE, D_GLA_K), proj),
                   pl.BlockSpec((TILE, D_GLA_V), proj), pl.BlockSpec((TILE, D_GLA_V), proj),
                   pl.BlockSpec((TILE, D_CONV), proj), pl.BlockSpec((TILE, D_GLA_V), gla))
        + tuple(cast_specs) + (pl.BlockSpec((1, D_MODEL), mod2_out),),
        out_shape=out_shape + tuple(jax.ShapeDtypeStruct(w.shape, BF16) for w in cast_in)
        + (jax.ShapeDtypeStruct((1, mod2_blocks * D_MODEL), F32),),
        scratch_shapes=[pltpu.VMEM((N_HEADS * HEAD_K, HEAD_V), F32)] + carry + carry,
        compiler_params=pltpu.CompilerParams(dimension_semantics=("arbitrary",),
                                             vmem_limit_bytes=VMEM_LIMIT),
        name="pass1_project_gla_bwd",
    )(x2d, mod, n1g, w_in_bf, wa2f, baf, wa2b, bab, sb0, *cast_in, silu_ct, w_mod, b_mod)


def _conv_stage_in(glu_ref, cbuf_ref):
    n_seg = glu_ref.shape[0] // CHUNK
    for j in range(D_CONV // LANES):
        for s in range(n_seg):
            base = s * SEG_PITCH + SEG_LEAD
            cbuf_ref[j, base:base + CHUNK, :] = glu_ref[_chunk_rows(s), j * LANES:(j + 1) * LANES]


def _conv_taps(segs, j, cbuf_ref, cw_ref, zero):
    accs = [zero for _ in segs]
    for t in range(CONV_WIDTH):
        w = cw_ref[t:t + 1, j * LANES:(j + 1) * LANES]
        for i, s in enumerate(segs):
            start = s * SEG_PITCH + SEG_LEAD - CONV_PAD + t
            accs[i] = accs[i] + cbuf_ref[j, start:start + CHUNK, :] * w
    return accs


def _conv_norm(cols, cb_ref, lng_ref, lnb_ref):
    y = jnp.concatenate(cols, axis=1) + cb_ref[...]
    mu = jnp.mean(y, axis=-1, keepdims=True)
    yc = y - mu
    var = jnp.mean(yc * yc, axis=-1, keepdims=True)
    return _silu(yc * lax.rsqrt(var + EPS) * lng_ref[...] + lnb_ref[...])


def _pass2_step(do_ffn, do_mix, x_ref, qk_ref, laf_ref, v_ref, g_ref, glu_ref, ob_ref, mod_ref,
                n2g_ref, gng_ref, fg_ref, cw_ref, cb_ref, lng_ref, lnb_ref,
                w_out_ref, wg_ref, wu_ref, wd_ref, never_ref, out_ref, s_ref, mix_ref, cbuf_ref):
    n_seg = x_ref.shape[0] // CHUNK
    n_ff = D_FF // MXU_N
    side, ffn = [], []
    latest = {}

    if do_mix:
        chunks = list(range(n_seg))
        consts = _gla_consts(True)
        gng = gng_ref[...]
        _conv_stage_in(glu_ref, cbuf_ref)
        b = {ci: _gla_cumsum(laf_ref[_chunk_rows(ci), :], consts) for ci in chunks}

    if do_ffn:
        g1 = mod_ref[0:1, 0:D_MODEL]
        sh2 = mod_ref[0:1, D_MODEL:2 * D_MODEL]
        sc2 = mod_ref[0:1, 2 * D_MODEL:3 * D_MODEL]
        g2 = mod_ref[0:1, 3 * D_MODEL:4 * D_MODEL]
        x1 = x_ref[...] + g1 * _dot(mix_ref[...], w_out_ref[...])

    if do_mix:
        state = {"s": s_ref[...]}
        prep, s_enter = {}, {}

        def gla_prep(ci):
            rows = _chunk_rows(ci)
            qk_c = qk_ref[rows, :]
            dec_col, kv, ops = _gla_prep(qk_c[:, :D_GLA_K], qk_c[:, D_GLA_K:], v_ref[rows, :], b[ci],
                                         consts, True, True)
            prep[ci] = ops
            s_enter[ci] = state["s"]
            state["s"] = dec_col * state["s"] + kv

        def gla_out(ci):
            rows = _chunk_rows(ci)
            o = _gla_out(prep[ci], s_enter[ci]) + ob_ref[rows, :].astype(F32)
            gate = _silu(g_ref[rows, :].astype(F32))
            for h in range(N_HEADS):
                cols = slice(h * HEAD_V, (h + 1) * HEAD_V)
                mix_ref[rows, D_CONV + h * HEAD_V:D_CONV + (h + 1) * HEAD_V] = (
                    _rms_rows(o[:, cols], gng) * gate[:, cols]).astype(BF16)

        conv_cols = {}
        never = never_ref[...] != 0

        def conv_taps(segs, j):
            zero = (jnp.where(never, latest["ffn"], 0.0) if do_ffn
                    else jnp.zeros((CHUNK, LANES), F32))
            for s, acc in zip(segs, _conv_taps(segs, j, cbuf_ref, cw_ref, zero)):
                conv_cols[s, j] = acc

        def conv_out(s):
            cols = [conv_cols.pop((s, j)) for j in range(D_CONV // LANES)]
            mix_ref[_chunk_rows(s), 0:D_CONV] = _conv_norm(cols, cb_ref, lng_ref, lnb_ref).astype(BF16)

        side = [functools.partial(gla_prep, ci) for ci in chunks]
        for g0 in range(0, n_seg, CONV_GROUP):
            segs = chunks[g0:g0 + CONV_GROUP]
            side += [functools.partial(gla_out, ci) for ci in segs]
            side += [functools.partial(conv_taps, segs, j) for j in range(D_CONV // LANES)]
            side += [functools.partial(conv_out, ci) for ci in segs]

    n_early = min(2, len(side)) if do_ffn else len(side)
    for thunk in side[:n_early]:
        thunk()

    if do_ffn:
        h2 = (_rms_rows(x1, n2g_ref[...]) * (1.0 + sc2) + sh2).astype(BF16)
        act, acc = {}, {}

        def gate_up(j):
            cols = slice(j * MXU_N, (j + 1) * MXU_N)
            gate = _dot(h2, wg_ref[:, cols])
            latest["ffn"] = gate[0:CHUNK, 0:LANES]
            act[j] = (_silu(gate) * _dot(h2, wu_ref[:, cols])).astype(BF16)

        def down(j):
            part = _dot(act.pop(j), wd_ref[j * MXU_N:(j + 1) * MXU_N, :])
            acc["v"] = part if j == 0 else acc["v"] + part

        ffn = [functools.partial(gate_up, 0)]
        for j in range(1, n_ff):
            ffn += [functools.partial(gate_up, j), functools.partial(down, j - 1)]
        ffn.append(functools.partial(down, n_ff - 1))
        _interleave(ffn, side[n_early:])
    if do_mix:
        s_ref[...] = state["s"]
    if do_ffn:
        x2 = x1 + g2 * acc["v"]
        out_ref[...] = _rms_rows(x2, fg_ref[...])


def _pass2_kernel(n_steps,
                  x_ref, qk_ref, laf_ref, v_ref, g_ref, glu_ref, ob_ref, mod_ref, n2g_ref,
                  gng_ref, fg_ref, cw_ref, cb_ref, lng_ref, lnb_ref,
                  w_out_ref, wg_ref, wu_ref, wd_ref, sf0_ref, never_ref,
                  out_ref, s_ref, mix_ref, cbuf_ref):
    step = pl.program_id(0)
    last = n_steps - 1
    run = functools.partial(_pass2_step, x_ref=x_ref, qk_ref=qk_ref, laf_ref=laf_ref, v_ref=v_ref,
                            g_ref=g_ref, glu_ref=glu_ref, ob_ref=ob_ref, mod_ref=mod_ref,
                            n2g_ref=n2g_ref, gng_ref=gng_ref, fg_ref=fg_ref, cw_ref=cw_ref,
                            cb_ref=cb_ref, lng_ref=lng_ref, lnb_ref=lnb_ref, w_out_ref=w_out_ref,
                            wg_ref=wg_ref, wu_ref=wu_ref, wd_ref=wd_ref, never_ref=never_ref,
                            out_ref=out_ref, s_ref=s_ref, mix_ref=mix_ref, cbuf_ref=cbuf_ref)

    @pl.when(step == 0)
    def _():
        s_ref[...] = sf0_ref[...]
        cbuf_ref[...] = jnp.zeros(cbuf_ref.shape, F32)
        run(False, True)

    @pl.when((step > 0) & (step < last))
    def _():
        run(True, True)

    @pl.when(step == last)
    def _():
        run(True, False)


def _pass2_call(x2d, qk, laf, v, g, glu, ob, mod, n2g, gng, fg, cw, cb, lng, lnb,
                w_out_bf, wg_bf, wu_bf, wd_bf, sf0, never):
    n_tok = x2d.shape[0]
    n_tiles = n_tok // TILE
    mixer = lambda s: (jnp.minimum(s, n_tiles - 1), 0)
    ffn = lambda s: (jnp.maximum(s - 1, 0), 0)
    const = lambda s: (0, 0)
    full = lambda a: pl.BlockSpec(a.shape, const)
    mixer_spec = lambda w: pl.BlockSpec((TILE, w), mixer)
    return pl.pallas_call(
        functools.partial(_pass2_kernel, n_tiles + 1),
        grid=(n_tiles + 1,),
        in_specs=[pl.BlockSpec((TILE, D_MODEL), ffn), mixer_spec(2 * D_GLA_K), mixer_spec(D_GLA_K),
                  mixer_spec(D_GLA_V), mixer_spec(D_GLA_V), mixer_spec(D_CONV), mixer_spec(D_GLA_V),
                  full(mod), full(n2g), full(gng), full(fg), full(cw), full(cb), full(lng), full(lnb),
                  full(w_out_bf), full(wg_bf), full(wu_bf), full(wd_bf), full(sf0),
                  full(never)],
        out_specs=pl.BlockSpec((TILE, D_MODEL), ffn),
        out_shape=jax.ShapeDtypeStruct((n_tok, D_MODEL), F32),
        scratch_shapes=[pltpu.VMEM((N_HEADS * HEAD_K, HEAD_V), F32),
                        pltpu.VMEM((TILE, D_CONV + D_GLA_V), BF16),
                        pltpu.VMEM((D_CONV // LANES, (TILE // CHUNK) * SEG_PITCH, LANES), F32)],
        compiler_params=pltpu.CompilerParams(dimension_semantics=("arbitrary",),
                                             vmem_limit_bytes=VMEM_LIMIT),
        name="pass2_gla_fwd_conv_merge_ffn",
    )(x2d, qk, laf, v, g, glu, ob, mod, n2g, gng, fg, cw, cb, lng, lnb,
      w_out_bf, wg_bf, wu_bf, wd_bf, sf0, never)


def kernel(x, c, ctx, c_ctx, w_mod, b_mod, norm1_g, norm2_g, w_in, conv_w, conv_b, conv_ln_g,
           conv_ln_b, w_a2_f, b_a_f, w_a2_b, b_a_b, gla_norm_g, w_out, w_gate, w_up, w_down, final_g):
    bsz, n_lat, d = x.shape
    assert bsz == 1 and d == D_MODEL and n_lat % TILE == 0 and ctx.shape[1] % CHUNK == 0
    assert (TILE // CHUNK) % CONV_GROUP == 0
    assert w_mod.shape[0] == 1, "single layer"
    row = lambda a: a.reshape(1, -1)

    ct = jnp.concatenate([c.reshape(D_MODEL, 1), c_ctx.reshape(D_MODEL, 1)], axis=1)
    mod, silu_ct = _mod_call(ct, w_mod[0], row(b_mod[0]))

    wa2f, baf = w_a2_f[0], row(b_a_f[0])
    wa2b, bab = w_a2_b[0], row(b_a_b[0])
    n1g = row(norm1_g[0])

    sf0, sb0, w_in_bf = _ctx_call(ctx[0], mod, n1g, w_in[0].T, wa2f, baf, wa2b, bab)

    x2d = x[0]
    qk, laf, v, g, glu, ob, wo_bf, wg_bf, wu_bf, wd_bf, mod2 = _pass1_call(
        x2d, mod, n1g, w_in_bf, wa2f, baf, wa2b, bab, sb0, w_out[0], w_gate[0], w_up[0], w_down[0],
        silu_ct, w_mod[0], row(b_mod[0]))
    unslab = lambda w: w.reshape(-1, w.shape[-1])

    out = _pass2_call(
        x2d, qk, laf, v, g, glu, ob, mod2, row(norm2_g[0]), row(gla_norm_g[0]), row(final_g),
        conv_w[0], row(conv_b[0]), row(conv_ln_g[0]), row(conv_ln_b[0]),
        unslab(wo_bf), unslab(wg_bf), unslab(wu_bf), unslab(wd_bf), sf0,
        jnp.zeros((CHUNK, LANES), jnp.int32))
    return out.reshape(bsz, n_lat, d)
```

```python
import functools

import jax
import jax.numpy as jnp
from jax import lax
from jax.experimental import pallas as pl
from jax.experimental.pallas import tpu as pltpu

D_MODEL = 1024
D_CONV = 512
CONV_WIDTH = 31
CONV_PAD = 15
N_HEADS = 4
HEAD_K = 64
HEAD_V = 128
D_GLA_K = N_HEADS * HEAD_K
D_GLA_V = N_HEADS * HEAD_V
GLA_RANK = 16
GLA_TAU = 16.0
CHUNK = 64
EPS = 1e-6
D_FF = 2816

C_Q = 2 * D_CONV
C_K = C_Q + D_GLA_K
C_V = C_K + D_GLA_K
C_G = C_V + D_GLA_V
C_R = C_G + D_GLA_V
D_IN = C_R + 2 * GLA_RANK

LANES = 128
MXU_N = 256
SEG_PITCH = 96
SEG_LEAD = 16
CONV_GROUP = 2
SUB = 256
TILE1 = 512
TILE = 256
VMEM_LIMIT = 56 * 1024 * 1024

F32 = jnp.float32
BF16 = jnp.bfloat16


def _dot(a, b):
    return jnp.dot(a, b, preferred_element_type=F32)


def _dot_nt(a, b):
    return lax.dot_general(a, b, (((1,), (1,)), ((), ())), preferred_element_type=F32)


def _sigmoid(x):
    return 1.0 / (1.0 + jnp.exp(-x))


def _silu(x):
    return x * _sigmoid(x)


def _log_sigmoid(z):
    return jnp.minimum(z, 0.0) - jnp.log(1.0 + jnp.exp(-jnp.abs(z)))


def _rms_rows(x, g):
    ms = jnp.mean(x * x, axis=-1, keepdims=True)
    return x * lax.rsqrt(ms + EPS) * g


def _interleave(main, side):
    n_main, n_side = len(main), len(side)
    done = 0
    for i, thunk in enumerate(main):
        thunk()
        want = ((i + 1) * n_side) // n_main
        while done < want:
            side[done]()
            done += 1


def _gla_consts(forward):
    r = lax.broadcasted_iota(jnp.int32, (CHUNK, CHUNK), 0)
    c = lax.broadcasted_iota(jnp.int32, (CHUNK, CHUNK), 1)
    tri = (c <= r) if forward else (c >= r)
    tri_bf = jnp.where(tri, 1.0, 0.0).astype(BF16)
    rr = lax.broadcasted_iota(jnp.int32, (N_HEADS * CHUNK, LANES), 0) % CHUNK
    cc = lax.broadcasted_iota(jnp.int32, (N_HEADS * CHUNK, LANES), 1)
    keep = ((cc <= rr) if forward else (cc >= rr)) & (cc < CHUNK)
    lane = lax.broadcasted_iota(jnp.int32, (CHUNK, D_GLA_K), 1)
    head_masks = [(lane >= h * HEAD_K) & (lane < (h + 1) * HEAD_K) for h in range(N_HEADS)]
    return tri_bf, keep, head_masks


def _gla_cumsum(la, consts):
    tri_bf = consts[0]
    la_hi = la.astype(BF16)
    la_lo = (la - la_hi.astype(F32)).astype(BF16)
    return _dot(tri_bf, la_hi) + _dot(tri_bf, la_lo)


def _gla_prep(q, k, v_bf, b, consts, forward, with_output):
    _, keep, head_masks = consts
    b_last = b[CHUNK - 1:CHUNK, :] if forward else b[0:1, :]
    k_w = k * jnp.exp(b_last - b)
    e_last = jnp.exp(b_last)

    tb = jnp.concatenate(
        [k_w, jnp.broadcast_to(e_last, (8, D_GLA_K)), jnp.zeros((CHUNK - 8, D_GLA_K), F32)], axis=0)
    tbt = tb.T
    dec_col = tbt[:, CHUNK:CHUNK + 1]
    kwt_bf = tbt.astype(BF16)
    zeros_v = jnp.zeros((CHUNK, HEAD_V), BF16)
    v_pads = [jnp.concatenate([v_bf[:, h * HEAD_V:(h + 1) * HEAD_V], zeros_v], axis=0)
              for h in range(N_HEADS)]
    kv = jnp.concatenate([_dot(kwt_bf[h * CHUNK:(h + 1) * CHUNK], v_pads[h])
                          for h in range(N_HEADS)], axis=0)
    if not with_output:
        return dec_col, kv, None
    q_t = (q * jnp.exp(b) * (HEAD_K ** -0.5)).astype(BF16)
    k_t = (k * jnp.exp(-b)).astype(BF16)
    qm = jnp.concatenate([jnp.where(m, q_t, jnp.zeros_like(q_t)) for m in head_masks], axis=0)
    kt_pad = jnp.concatenate([k_t, jnp.zeros_like(k_t)], axis=0)
    attn = _dot_nt(qm, kt_pad)
    p_bf = jnp.where(keep, attn, 0.0).astype(BF16)
    return dec_col, kv, (qm, p_bf, v_pads)


def _gla_out(ops, s_enter):
    qm, p_bf, v_pads = ops
    o_inter = _dot(qm, s_enter.astype(BF16))
    outs = []
    for h in range(N_HEADS):
        rows = slice(h * CHUNK, (h + 1) * CHUNK)
        outs.append(_dot(p_bf[rows], v_pads[h]) + o_inter[rows])
    return jnp.concatenate(outs, axis=1)


def _log_decay(r, w_a2, b_a):
    z = _dot(r.astype(BF16), w_a2.astype(BF16)) + b_a
    return _log_sigmoid(z) * (1.0 / GLA_TAU)


def _chunk_rows(ci):
    return slice(ci * CHUNK, (ci + 1) * CHUNK)


def _mod_kernel(ct_ref, w_ref, b_ref, o_ref, s_ref):
    s = _silu(ct_ref[...])
    s_ref[...] = s
    w = w_ref[...]
    for j in range(2):
        o_ref[j:j + 1, :] = jnp.sum(w * s[:, j:j + 1], axis=0, keepdims=True) + b_ref[...]


def _mod_call(ct, w_mod, b_mod):
    tn = 512
    n = 2 * D_MODEL
    return pl.pallas_call(
        _mod_kernel,
        grid=(n // tn,),
        in_specs=[pl.BlockSpec((D_MODEL, 2), lambda j: (0, 0)),
                  pl.BlockSpec((D_MODEL, tn), lambda j: (0, j)),
                  pl.BlockSpec((1, tn), lambda j: (0, j))],
        out_specs=(pl.BlockSpec((2, tn), lambda j: (0, j)),
                   pl.BlockSpec((D_MODEL, 2), lambda j: (0, 0))),
        out_shape=(jax.ShapeDtypeStruct((2, n), F32), jax.ShapeDtypeStruct((D_MODEL, 2), F32)),
        compiler_params=pltpu.CompilerParams(dimension_semantics=("arbitrary",),
                                             vmem_limit_bytes=VMEM_LIMIT),
        name="adaln_mod",
    )(ct, w_mod, b_mod)


def _ctx_kernel(ctx_ref, mod_ref, n1g_ref, w_in_t_ref, wa2f_ref, baf_ref, wa2b_ref, bab_ref,
                sf_ref, sb_ref, w_in_ref):
    n_full = D_IN // MXU_N
    for j in range(n_full):
        cols = slice(j * MXU_N, (j + 1) * MXU_N)
        w_in_ref[:, cols] = w_in_t_ref[cols, :].T.astype(BF16)
    tail = w_in_t_ref[D_IN - LANES:D_IN, :].T
    w_in_ref[:, n_full * MXU_N:D_IN] = tail[:, LANES - (D_IN - n_full * MXU_N):].astype(BF16)
    n_chunks = ctx_ref.shape[0] // CHUNK
    sh = mod_ref[1:2, 0:D_MODEL]
    sc = mod_ref[1:2, D_MODEL:2 * D_MODEL]
    h = (_rms_rows(ctx_ref[...], n1g_ref[...]) * (1.0 + sc) + sh).astype(BF16)
    kv = _dot(h, w_in_ref[:, C_K:C_G])
    r = _dot(h, w_in_ref[:, C_R:D_IN])
    k = kv[:, :D_GLA_K]
    v_bf = kv[:, D_GLA_K:].astype(BF16)
    la_f = _log_decay(r[:, :GLA_RANK], wa2f_ref[...], baf_ref[...])
    la_b = _log_decay(r[:, GLA_RANK:], wa2b_ref[...], bab_ref[...])
    for forward, la, out_ref in ((True, la_f, sf_ref), (False, la_b, sb_ref)):
        consts = _gla_consts(forward)
        order = list(range(n_chunks)) if forward else list(range(n_chunks - 1, -1, -1))
        b = {ci: _gla_cumsum(la[_chunk_rows(ci)], consts) for ci in order}
        s = jnp.zeros((N_HEADS * HEAD_K, HEAD_V), F32)
        for ci in order:
            rows = _chunk_rows(ci)
            dec_col, kv_c, _ = _gla_prep(None, k[rows], v_bf[rows], b[ci], consts, forward, False)
            s = dec_col * s + kv_c
        out_ref[...] = s


def _ctx_call(ctx2d, mod, n1g, w_in_t, wa2f, baf, wa2b, bab):
    st = jax.ShapeDtypeStruct((N_HEADS * HEAD_K, HEAD_V), F32)
    return pl.pallas_call(
        _ctx_kernel,
        out_shape=(st, st, jax.ShapeDtypeStruct(w_in_t.shape[::-1], BF16)),
        compiler_params=pltpu.CompilerParams(vmem_limit_bytes=VMEM_LIMIT),
        name="context_states",
    )(ctx2d, mod, n1g, w_in_t, wa2f, baf, wa2b, bab)


def _pass1_rows(r0, state, x_ref, mod_ref, n1g_ref, w_in_ref, wa2f_ref, baf_ref, wa2b_ref, bab_ref,
                qk_ref, laf_ref, v_ref, g_ref, glu_ref, ob_ref, carry_w, carry_r):
    rows = slice(r0, r0 + SUB)
    proj_blocks, gla_stages = [], []

    if carry_r is not None:
        qk_r, v_r, lab_r = carry_r
        first = r0 // CHUNK
        order = list(range(first + SUB // CHUNK - 1, first - 1, -1))
        consts = _gla_consts(False)
        b = {ci: _gla_cumsum(lab_r[_chunk_rows(ci), :], consts) for ci in order}

    if carry_w is not None:
        qk_w, v_w, lab_w = carry_w
        sh1 = mod_ref[0:1, 0:D_MODEL]
        sc1 = mod_ref[0:1, D_MODEL:2 * D_MODEL]
        h = (_rms_rows(x_ref[rows, :], n1g_ref[...]) * (1.0 + sc1) + sh1).astype(BF16)

        def proj(lo, hi):
            return _dot(h, w_in_ref[:, lo:hi])

        def glu_block(j):
            lo, hi = j * MXU_N, (j + 1) * MXU_N
            glu_ref[rows, lo:hi] = proj(lo, hi) * _sigmoid(proj(D_CONV + lo, D_CONV + hi))

        def qk_block(j):
            lo, hi = j * MXU_N, (j + 1) * MXU_N
            blk = proj(C_Q + lo, C_Q + hi)
            qk_ref[rows, lo:hi] = blk
            qk_w[rows, lo:hi] = blk

        def v_block(j):
            lo, hi = j * MXU_N, (j + 1) * MXU_N
            blk = proj(C_V + lo, C_V + hi).astype(BF16)
            v_ref[rows, lo:hi] = blk
            v_w[rows, lo:hi] = blk

        def g_block(j):
            lo, hi = j * MXU_N, (j + 1) * MXU_N
            g_ref[rows, lo:hi] = proj(C_G + lo, C_G + hi).astype(BF16)

        low_rank = {}

        def rank_block():
            low_rank["r"] = proj(C_R, D_IN)

        def decay_block():
            r = low_rank.pop("r")
            laf_ref[rows, :] = _log_decay(r[:, :GLA_RANK], wa2f_ref[...], baf_ref[...])
            lab_w[rows, :] = _log_decay(r[:, GLA_RANK:], wa2b_ref[...], bab_ref[...])

        proj_blocks = ([rank_block]
                       + [functools.partial(glu_block, j) for j in range(D_CONV // MXU_N)]
                       + [decay_block]
                       + [functools.partial(qk_block, j) for j in range(2 * D_GLA_K // MXU_N)]
                       + [functools.partial(v_block, j) for j in range(D_GLA_V // MXU_N)]
                       + [functools.partial(g_block, j) for j in range(D_GLA_V // MXU_N)])

    if carry_r is not None:
        prep, s_enter = {}, {}

        def gla_prep(ci):
            crows = _chunk_rows(ci)
            qk_c = qk_r[crows, :]
            dec_col, kv, ops = _gla_prep(qk_c[:, :D_GLA_K], qk_c[:, D_GLA_K:], v_r[crows, :], b[ci],
                                         consts, False, True)
            prep[ci] = ops
            s_enter[ci] = state["s"]
            state["s"] = dec_col * state["s"] + kv

        def gla_out(ci):
            ob_ref[_chunk_rows(ci), :] = _gla_out(prep[ci], s_enter[ci]).astype(BF16)

        gla_stages = ([functools.partial(gla_prep, ci) for ci in order]
                      + [functools.partial(gla_out, ci) for ci in order])

    if proj_blocks and gla_stages:
        _interleave(proj_blocks, gla_stages)
    else:
        for thunk in proj_blocks + gla_stages:
            thunk()


def _pass1_step(x_ref, mod_ref, n1g_ref, w_in_ref, wa2f_ref, baf_ref, wa2b_ref, bab_ref,
                qk_ref, laf_ref, v_ref, g_ref, glu_ref, ob_ref, s_ref, carry_w, carry_r):
    state = {"s": s_ref[...]} if carry_r is not None else None
    for r0 in range(x_ref.shape[0] - SUB, -1, -SUB):
        _pass1_rows(r0, state, x_ref, mod_ref, n1g_ref, w_in_ref, wa2f_ref, baf_ref, wa2b_ref,
                    bab_ref, qk_ref, laf_ref, v_ref, g_ref, glu_ref, ob_ref, carry_w, carry_r)
    if carry_r is not None:
        s_ref[...] = state["s"]


def _pass1_kernel(n_steps, n_mod2,
                  x_ref, mod_ref, n1g_ref, w_in_ref, wa2f_ref, baf_ref, wa2b_ref, bab_ref, sb0_ref,
                  wo_f32_ref, wg_f32_ref, wu_f32_ref, wd_f32_ref, silu_ref, w_mod_ref, b_mod_ref,
                  qk_ref, laf_ref, v_ref, g_ref, glu_ref, ob_ref,
                  wo_bf_ref, wg_bf_ref, wu_bf_ref, wd_bf_ref, mod2_ref,
                  s_ref, qk_a, v_a, lab_a, qk_b, v_b, lab_b):
    step = pl.program_id(0)
    last = n_steps - 1

    @pl.when(step < n_mod2)
    def _():
        mod2_ref[...] = (jnp.sum(w_mod_ref[...] * silu_ref[:, 0:1], axis=0, keepdims=True)
                         + b_mod_ref[...])

    for src, dst in ((wo_f32_ref, wo_bf_ref), (wg_f32_ref, wg_bf_ref), (wu_f32_ref, wu_bf_ref),
                     (wd_f32_ref, wd_bf_ref)):
        dst[...] = src[...].astype(BF16)
    bufs = ((qk_a, v_a, lab_a), (qk_b, v_b, lab_b))

    run = functools.partial(_pass1_step, x_ref, mod_ref, n1g_ref, w_in_ref, wa2f_ref, baf_ref,
                            wa2b_ref, bab_ref, qk_ref, laf_ref, v_ref, g_ref, glu_ref, ob_ref, s_ref)

    @pl.when(step == 0)
    def _():
        s_ref[...] = sb0_ref[...]
        run(bufs[0], None)

    for parity in (0, 1):
        @pl.when((step > 0) & (step < last) & (step % 2 == parity))
        def _():
            run(bufs[parity], bufs[1 - parity])

    @pl.when(step == last)
    def _():
        run(None, bufs[(last - 1) % 2])


def _slabs(w, n_slabs):
    rows, cols = w.shape
    assert rows % (n_slabs * 16) == 0
    return w.reshape(n_slabs, rows // n_slabs, cols)


def _pass1_call(x2d, mod, n1g, w_in_bf, wa2f, baf, wa2b, bab, sb0, w_out, w_gate, w_up, w_down,
                silu_ct, w_mod, b_mod):
    n_tok = x2d.shape[0]
    n_tiles = n_tok // TILE1
    mod2_lo = mod.shape[1] // D_MODEL
    mod2_blocks = w_mod.shape[1] // D_MODEL - mod2_lo
    assert mod2_blocks <= n_tiles
    mod2_in = lambda s: (0, mod2_lo + jnp.minimum(s, mod2_blocks - 1))
    mod2_out = lambda s: (0, jnp.minimum(s, mod2_blocks - 1))
    cast_in = [_slabs(w, n) for w, n in ((w_out, 32), (w_gate, 32), (w_up, 32), (w_down, 22))]
    assert all(w.shape[0] <= n_tiles for w in cast_in)
    cast_specs = [pl.BlockSpec((1,) + w.shape[1:],
                               functools.partial(lambda n, s: (jnp.minimum(s, n - 1), 0, 0), w.shape[0]))
                  for w in cast_in]
    proj = lambda s: (n_tiles - 1 - jnp.minimum(s, n_tiles - 1), 0)
    gla = lambda s: (n_tiles - 1 - jnp.maximum(s - 1, 0), 0)
    const = lambda s: (0, 0)
    full = lambda a: pl.BlockSpec(a.shape, const)
    out_shape = (
        jax.ShapeDtypeStruct((n_tok, 2 * D_GLA_K), F32),
        jax.ShapeDtypeStruct((n_tok, D_GLA_K), F32),
        jax.ShapeDtypeStruct((n_tok, D_GLA_V), BF16),
        jax.ShapeDtypeStruct((n_tok, D_GLA_V), BF16),
        jax.ShapeDtypeStruct((n_tok, D_CONV), F32),
        jax.ShapeDtypeStruct((n_tok, D_GLA_V), BF16),
    )
    carry = [pltpu.VMEM((TILE1, 2 * D_GLA_K), F32), pltpu.VMEM((TILE1, D_GLA_V), BF16),
             pltpu.VMEM((TILE1, D_GLA_K), F32)]
    return pl.pallas_call(
        functools.partial(_pass1_kernel, n_tiles + 1, mod2_blocks),
        grid=(n_tiles + 1,),
        in_specs=[pl.BlockSpec((TILE1, D_MODEL), proj), full(mod), full(n1g), full(w_in_bf),
                  full(wa2f), full(baf), full(wa2b), full(bab), full(sb0)] + cast_specs
        + [full(silu_ct), pl.BlockSpec((D_MODEL, D_MODEL), mod2_in), pl.BlockSpec((1, D_MODEL), mod2_in)],
        out_specs=(pl.BlockSpec((TILE1, 2 * D_GLA_K), proj), pl.BlockSpec((TILE1, D_GLA_K), proj),
                   pl.BlockSpec((TILE1, D_GLA_V), proj), pl.BlockSpec((TILE1, D_GLA_V), proj),
                   pl.BlockSpec((TILE1, D_CONV), proj), pl.BlockSpec((TILE1, D_GLA_V), gla))
        + tuple(cast_specs) + (pl.BlockSpec((1, D_MODEL), mod2_out),),
        out_shape=out_shape + tuple(jax.ShapeDtypeStruct(w.shape, BF16) for w in cast_in)
        + (jax.ShapeDtypeStruct((1, mod2_blocks * D_MODEL), F32),),
        scratch_shapes=[pltpu.VMEM((N_HEADS * HEAD_K, HEAD_V), F32)] + carry + carry,
        compiler_params=pltpu.CompilerParams(dimension_semantics=("arbitrary",),
                                             vmem_limit_bytes=VMEM_LIMIT),
        name="pass1_project_gla_bwd",
    )(x2d, mod, n1g, w_in_bf, wa2f, baf, wa2b, bab, sb0, *cast_in, silu_ct, w_mod, b_mod)


def _conv_stage_in(glu_ref, cbuf_ref):
    n_seg = glu_ref.shape[0] // CHUNK
    for j in range(D_CONV // LANES):
        for s in range(n_seg):
            base = s * SEG_PITCH + SEG_LEAD
            cbuf_ref[j, base:base + CHUNK, :] = glu_ref[_chunk_rows(s), j * LANES:(j + 1) * LANES]


def _conv_taps(segs, j, cbuf_ref, cw_ref, zero):
    accs = [zero for _ in segs]
    for t in range(CONV_WIDTH):
        w = cw_ref[t:t + 1, j * LANES:(j + 1) * LANES]
        for i, s in enumerate(segs):
            start = s * SEG_PITCH + SEG_LEAD - CONV_PAD + t
            accs[i] = accs[i] + cbuf_ref[j, start:start + CHUNK, :] * w
    return accs


def _conv_norm(cols, cb_ref, lng_ref, lnb_ref):
    y = jnp.concatenate(cols, axis=1) + cb_ref[...]
    mu = jnp.mean(y, axis=-1, keepdims=True)
    yc = y - mu
    var = jnp.mean(yc * yc, axis=-1, keepdims=True)
    return _silu(yc * lax.rsqrt(var + EPS) * lng_ref[...] + lnb_ref[...])


def _pass2_step(do_ffn, do_mix, x_ref, qk_ref, laf_ref, v_ref, g_ref, glu_ref, ob_ref, mod_ref,
                n2g_ref, gng_ref, fg_ref, cw_ref, cb_ref, lng_ref, lnb_ref,
                w_out_ref, wg_ref, wu_ref, wd_ref, never_ref, out_ref, s_ref, mix_ref, cbuf_ref):
    n_seg = x_ref.shape[0] // CHUNK
    n_ff = D_FF // MXU_N
    side, ffn = [], []
    latest = {}

    if do_mix:
        chunks = list(range(n_seg))
        consts = _gla_consts(True)
        gng = gng_ref[...]
        _conv_stage_in(glu_ref, cbuf_ref)
        b = {ci: _gla_cumsum(laf_ref[_chunk_rows(ci), :], consts) for ci in chunks}

    if do_ffn:
        g1 = mod_ref[0:1, 0:D_MODEL]
        sh2 = mod_ref[0:1, D_MODEL:2 * D_MODEL]
        sc2 = mod_ref[0:1, 2 * D_MODEL:3 * D_MODEL]
        g2 = mod_ref[0:1, 3 * D_MODEL:4 * D_MODEL]
        x1 = x_ref[...] + g1 * _dot(mix_ref[...], w_out_ref[...])

    if do_mix:
        state = {"s": s_ref[...]}
        prep, s_enter = {}, {}

        def gla_prep(ci):
            rows = _chunk_rows(ci)
            qk_c = qk_ref[rows, :]
            dec_col, kv, ops = _gla_prep(qk_c[:, :D_GLA_K], qk_c[:, D_GLA_K:], v_ref[rows, :], b[ci],
                                         consts, True, True)
            prep[ci] = ops
            s_enter[ci] = state["s"]
            state["s"] = dec_col * state["s"] + kv

        def gla_out(ci):
            rows = _chunk_rows(ci)
            o = _gla_out(prep[ci], s_enter[ci]) + ob_ref[rows, :].astype(F32)
            gate = _silu(g_ref[rows, :].astype(F32))
            for h in range(N_HEADS):
                cols = slice(h * HEAD_V, (h + 1) * HEAD_V)
                mix_ref[rows, D_CONV + h * HEAD_V:D_CONV + (h + 1) * HEAD_V] = (
                    _rms_rows(o[:, cols], gng) * gate[:, cols]).astype(BF16)

        conv_cols = {}
        never = never_ref[...] != 0

        def conv_taps(segs, j):
            zero = (jnp.where(never, latest["ffn"], 0.0) if do_ffn
                    else jnp.zeros((CHUNK, LANES), F32))
            for s, acc in zip(segs, _conv_taps(segs, j, cbuf_ref, cw_ref, zero)):
                conv_cols[s, j] = acc

        def conv_out(s):
            cols = [conv_cols.pop((s, j)) for j in range(D_CONV // LANES)]
            mix_ref[_chunk_rows(s), 0:D_CONV] = _conv_norm(cols, cb_ref, lng_ref, lnb_ref).astype(BF16)

        side = [functools.partial(gla_prep, ci) for ci in chunks]
        for g0 in range(0, n_seg, CONV_GROUP):
            segs = chunks[g0:g0 + CONV_GROUP]
            side += [functools.partial(gla_out, ci) for ci in segs]
            side += [functools.partial(conv_taps, segs, j) for j in range(D_CONV // LANES)]
            side += [functools.partial(conv_out, ci) for ci in segs]

    n_early = min(2, len(side)) if do_ffn else len(side)
    for thunk in side[:n_early]:
        thunk()

    if do_ffn:
        h2 = (_rms_rows(x1, n2g_ref[...]) * (1.0 + sc2) + sh2).astype(BF16)
        act, acc = {}, {}

        def gate_up(j):
            cols = slice(j * MXU_N, (j + 1) * MXU_N)
            gate = _dot(h2, wg_ref[:, cols])
            latest["ffn"] = gate[0:CHUNK, 0:LANES]
            act[j] = (_silu(gate) * _dot(h2, wu_ref[:, cols])).astype(BF16)

        def down(j):
            part = _dot(act.pop(j), wd_ref[j * MXU_N:(j + 1) * MXU_N, :])
            acc["v"] = part if j == 0 else acc["v"] + part

        ffn = [functools.partial(gate_up, 0)]
        for j in range(1, n_ff):
            ffn += [functools.partial(gate_up, j), functools.partial(down, j - 1)]
        ffn.append(functools.partial(down, n_ff - 1))
        _interleave(ffn, side[n_early:])
    if do_mix:
        s_ref[...] = state["s"]
    if do_ffn:
        x2 = x1 + g2 * acc["v"]
        out_ref[...] = _rms_rows(x2, fg_ref[...])


def _pass2_kernel(n_steps,
                  x_ref, qk_ref, laf_ref, v_ref, g_ref, glu_ref, ob_ref, mod_ref, n2g_ref,
                  gng_ref, fg_ref, cw_ref, cb_ref, lng_ref, lnb_ref,
                  w_out_ref, wg_ref, wu_ref, wd_ref, sf0_ref, never_ref,
                  out_ref, s_ref, mix_ref, cbuf_ref):
    step = pl.program_id(0)
    last = n_steps - 1
    run = functools.partial(_pass2_step, x_ref=x_ref, qk_ref=qk_ref, laf_ref=laf_ref, v_ref=v_ref,
                            g_ref=g_ref, glu_ref=glu_ref, ob_ref=ob_ref, mod_ref=mod_ref,
                            n2g_ref=n2g_ref, gng_ref=gng_ref, fg_ref=fg_ref, cw_ref=cw_ref,
                            cb_ref=cb_ref, lng_ref=lng_ref, lnb_ref=lnb_ref, w_out_ref=w_out_ref,
                            wg_ref=wg_ref, wu_ref=wu_ref, wd_ref=wd_ref, never_ref=never_ref,
                            out_ref=out_ref, s_ref=s_ref, mix_ref=mix_ref, cbuf_ref=cbuf_ref)

    @pl.when(step == 0)
    def _():
        s_ref[...] = sf0_ref[...]
        cbuf_ref[...] = jnp.zeros(cbuf_ref.shape, F32)
        run(False, True)

    @pl.when((step > 0) & (step < last))
    def _():
        run(True, True)

    @pl.when(step == last)
    def _():
        run(True, False)


def _pass2_call(x2d, qk, laf, v, g, glu, ob, mod, n2g, gng, fg, cw, cb, lng, lnb,
                w_out_bf, wg_bf, wu_bf, wd_bf, sf0, never):
    n_tok = x2d.shape[0]
    n_tiles = n_tok // TILE
    mixer = lambda s: (jnp.minimum(s, n_tiles - 1), 0)
    ffn = lambda s: (jnp.maximum(s - 1, 0), 0)
    const = lambda s: (0, 0)
    full = lambda a: pl.BlockSpec(a.shape, const)
    mixer_spec = lambda w: pl.BlockSpec((TILE, w), mixer)
    return pl.pallas_call(
        functools.partial(_pass2_kernel, n_tiles + 1),
        grid=(n_tiles + 1,),
        in_specs=[pl.BlockSpec((TILE, D_MODEL), ffn), mixer_spec(2 * D_GLA_K), mixer_spec(D_GLA_K),
                  mixer_spec(D_GLA_V), mixer_spec(D_GLA_V), mixer_spec(D_CONV), mixer_spec(D_GLA_V),
                  full(mod), full(n2g), full(gng), full(fg), full(cw), full(cb), full(lng), full(lnb),
                  full(w_out_bf), full(wg_bf), full(wu_bf), full(wd_bf), full(sf0),
                  full(never)],
        out_specs=pl.BlockSpec((TILE, D_MODEL), ffn),
        out_shape=jax.ShapeDtypeStruct((n_tok, D_MODEL), F32),
        scratch_shapes=[pltpu.VMEM((N_HEADS * HEAD_K, HEAD_V), F32),
                        pltpu.VMEM((TILE, D_CONV + D_GLA_V), BF16),
                        pltpu.VMEM((D_CONV // LANES, (TILE // CHUNK) * SEG_PITCH, LANES), F32)],
        compiler_params=pltpu.CompilerParams(dimension_semantics=("arbitrary",),
                                             vmem_limit_bytes=VMEM_LIMIT),
        name="pass2_gla_fwd_conv_merge_ffn",
    )(x2d, qk, laf, v, g, glu, ob, mod, n2g, gng, fg, cw, cb, lng, lnb,
      w_out_bf, wg_bf, wu_bf, wd_bf, sf0, never)


def kernel(x, c, ctx, c_ctx, w_mod, b_mod, norm1_g, norm2_g, w_in, conv_w, conv_b, conv_ln_g,
           conv_ln_b, w_a2_f, b_a_f, w_a2_b, b_a_b, gla_norm_g, w_out, w_gate, w_up, w_down, final_g):
    bsz, n_lat, d = x.shape
    assert bsz == 1 and d == D_MODEL and n_lat % TILE == 0 and ctx.shape[1] % CHUNK == 0
    assert n_lat % TILE1 == 0 and TILE1 % SUB == 0
    assert (TILE // CHUNK) % CONV_GROUP == 0
    assert w_mod.shape[0] == 1, "single layer"
    row = lambda a: a.reshape(1, -1)

    ct = jnp.concatenate([c.reshape(D_MODEL, 1), c_ctx.reshape(D_MODEL, 1)], axis=1)
    mod, silu_ct = _mod_call(ct, w_mod[0], row(b_mod[0]))

    wa2f, baf = w_a2_f[0], row(b_a_f[0])
    wa2b, bab = w_a2_b[0], row(b_a_b[0])
    n1g = row(norm1_g[0])

    sf0, sb0, w_in_bf = _ctx_call(ctx[0], mod, n1g, w_in[0].T, wa2f, baf, wa2b, bab)

    x2d = x[0]
    qk, laf, v, g, glu, ob, wo_bf, wg_bf, wu_bf, wd_bf, mod2 = _pass1_call(
        x2d, mod, n1g, w_in_bf, wa2f, baf, wa2b, bab, sb0, w_out[0], w_gate[0], w_up[0], w_down[0],
        silu_ct, w_mod[0], row(b_mod[0]))
    unslab = lambda w: w.reshape(-1, w.shape[-1])

    out = _pass2_call(
        x2d, qk, laf, v, g, glu, ob, mod2, row(norm2_g[0]), row(gla_norm_g[0]), row(final_g),
        conv_w[0], row(conv_b[0]), row(conv_ln_g[0]), row(conv_ln_b[0]),
        unslab(wo_bf), unslab(wg_bf), unslab(wu_bf), unslab(wd_bf), sf0,
        jnp.zeros((CHUNK, LANES), jnp.int32))
    return out.reshape(bsz, n_lat, d)
```

```python
import functools

import jax
import jax.numpy as jnp
from jax import lax
from jax.experimental import pallas as pl
from jax.experimental.pallas import tpu as pltpu

D_MODEL = 1024
D_CONV = 512
CONV_WIDTH = 31
CONV_PAD = 15
N_HEADS = 4
HEAD_K = 64
HEAD_V = 128
D_GLA_K = N_HEADS * HEAD_K
D_GLA_V = N_HEADS * HEAD_V
GLA_RANK = 16
GLA_TAU = 16.0
CHUNK = 64
EPS = 1e-6
D_FF = 2816

C_Q = 2 * D_CONV
C_K = C_Q + D_GLA_K
C_V = C_K + D_GLA_K
C_G = C_V + D_GLA_V
C_R = C_G + D_GLA_V
D_IN = C_R + 2 * GLA_RANK

LANES = 128
MXU_N = 256
SEG_PITCH = 96
SEG_LEAD = 16
SUB = 256
TILE1 = 512
TILE = 256
VMEM_LIMIT = 56 * 1024 * 1024

F32 = jnp.float32
BF16 = jnp.bfloat16


def _dot(a, b):
    return jnp.dot(a, b, preferred_element_type=F32)


def _dot_nt(a, b):
    return lax.dot_general(a, b, (((1,), (1,)), ((), ())), preferred_element_type=F32)


def _sigmoid(x):
    return 1.0 / (1.0 + jnp.exp(-x))


def _silu(x):
    return x * _sigmoid(x)


def _log_sigmoid(z):
    return jnp.minimum(z, 0.0) - jnp.log(1.0 + jnp.exp(-jnp.abs(z)))


def _rms_rows(x, g):
    ms = jnp.mean(x * x, axis=-1, keepdims=True)
    return x * lax.rsqrt(ms + EPS) * g


def _interleave(main, side):
    n_main, n_side = len(main), len(side)
    done = 0
    for i, thunk in enumerate(main):
        thunk()
        want = ((i + 1) * n_side) // n_main
        while done < want:
            side[done]()
            done += 1


def _gla_consts(forward):
    r = lax.broadcasted_iota(jnp.int32, (CHUNK, CHUNK), 0)
    c = lax.broadcasted_iota(jnp.int32, (CHUNK, CHUNK), 1)
    tri = (c <= r) if forward else (c >= r)
    tri_bf = jnp.where(tri, 1.0, 0.0).astype(BF16)
    rr = lax.broadcasted_iota(jnp.int32, (N_HEADS * CHUNK, LANES), 0) % CHUNK
    cc = lax.broadcasted_iota(jnp.int32, (N_HEADS * CHUNK, LANES), 1)
    keep = ((cc <= rr) if forward else (cc >= rr)) & (cc < CHUNK)
    lane = lax.broadcasted_iota(jnp.int32, (CHUNK, D_GLA_K), 1)
    head_masks = [(lane >= h * HEAD_K) & (lane < (h + 1) * HEAD_K) for h in range(N_HEADS)]
    return tri_bf, keep, head_masks


def _gla_cumsum(la, consts):
    tri_bf = consts[0]
    la_hi = la.astype(BF16)
    la_lo = (la - la_hi.astype(F32)).astype(BF16)
    return _dot(tri_bf, la_hi) + _dot(tri_bf, la_lo)


def _gla_prep(q, k, v_bf, b, consts, forward, with_output):
    _, keep, head_masks = consts
    b_last = b[CHUNK - 1:CHUNK, :] if forward else b[0:1, :]
    k_w = k * jnp.exp(b_last - b)
    e_last = jnp.exp(b_last)

    tb = jnp.concatenate(
        [k_w, jnp.broadcast_to(e_last, (8, D_GLA_K)), jnp.zeros((CHUNK - 8, D_GLA_K), F32)], axis=0)
    tbt = tb.T
    dec_col = tbt[:, CHUNK:CHUNK + 1]
    kwt_bf = tbt.astype(BF16)
    zeros_v = jnp.zeros((CHUNK, HEAD_V), BF16)
    v_pads = [jnp.concatenate([v_bf[:, h * HEAD_V:(h + 1) * HEAD_V], zeros_v], axis=0)
              for h in range(N_HEADS)]
    kv = jnp.concatenate([_dot(kwt_bf[h * CHUNK:(h + 1) * CHUNK], v_pads[h])
                          for h in range(N_HEADS)], axis=0)
    if not with_output:
        return dec_col, kv, None
    q_t = (q * jnp.exp(b) * (HEAD_K ** -0.5)).astype(BF16)
    k_t = (k * jnp.exp(-b)).astype(BF16)
    qm = jnp.concatenate([jnp.where(m, q_t, jnp.zeros_like(q_t)) for m in head_masks], axis=0)
    kt_pad = jnp.concatenate([k_t, jnp.zeros_like(k_t)], axis=0)
    attn = _dot_nt(qm, kt_pad)
    p_bf = jnp.where(keep, attn, 0.0).astype(BF16)
    return dec_col, kv, (qm, p_bf, v_pads)


def _gla_out(ops, s_enter):
    qm, p_bf, v_pads = ops
    o_inter = _dot(qm, s_enter.astype(BF16))
    outs = []
    for h in range(N_HEADS):
        rows = slice(h * CHUNK, (h + 1) * CHUNK)
        outs.append(_dot(p_bf[rows], v_pads[h]) + o_inter[rows])
    return jnp.concatenate(outs, axis=1)


def _log_decay(r, w_a2, b_a):
    z = _dot(r.astype(BF16), w_a2.astype(BF16)) + b_a
    return _log_sigmoid(z) * (1.0 / GLA_TAU)


def _chunk_rows(ci):
    return slice(ci * CHUNK, (ci + 1) * CHUNK)


def _mod_kernel(ct_ref, w_ref, b_ref, o_ref, s_ref):
    s = _silu(ct_ref[...])
    s_ref[...] = s
    w = w_ref[...]
    for j in range(2):
        o_ref[j:j + 1, :] = jnp.sum(w * s[:, j:j + 1], axis=0, keepdims=True) + b_ref[...]


def _mod_call(ct, w_mod, b_mod):
    tn = 512
    n = 2 * D_MODEL
    return pl.pallas_call(
        _mod_kernel,
        grid=(n // tn,),
        in_specs=[pl.BlockSpec((D_MODEL, 2), lambda j: (0, 0)),
                  pl.BlockSpec((D_MODEL, tn), lambda j: (0, j)),
                  pl.BlockSpec((1, tn), lambda j: (0, j))],
        out_specs=(pl.BlockSpec((2, tn), lambda j: (0, j)),
                   pl.BlockSpec((D_MODEL, 2), lambda j: (0, 0))),
        out_shape=(jax.ShapeDtypeStruct((2, n), F32), jax.ShapeDtypeStruct((D_MODEL, 2), F32)),
        compiler_params=pltpu.CompilerParams(dimension_semantics=("arbitrary",),
                                             vmem_limit_bytes=VMEM_LIMIT),
        name="adaln_mod",
    )(ct, w_mod, b_mod)


def _ctx_kernel(ctx_ref, mod_ref, n1g_ref, w_in_t_ref, wa2f_ref, baf_ref, wa2b_ref, bab_ref,
                sf_ref, sb_ref, w_in_ref):
    n_full = D_IN // MXU_N
    for j in range(n_full):
        cols = slice(j * MXU_N, (j + 1) * MXU_N)
        w_in_ref[:, cols] = w_in_t_ref[cols, :].T.astype(BF16)
    tail = w_in_t_ref[D_IN - LANES:D_IN, :].T
    w_in_ref[:, n_full * MXU_N:D_IN] = tail[:, LANES - (D_IN - n_full * MXU_N):].astype(BF16)
    n_chunks = ctx_ref.shape[0] // CHUNK
    sh = mod_ref[1:2, 0:D_MODEL]
    sc = mod_ref[1:2, D_MODEL:2 * D_MODEL]
    h = (_rms_rows(ctx_ref[...], n1g_ref[...]) * (1.0 + sc) + sh).astype(BF16)
    kv = _dot(h, w_in_ref[:, C_K:C_G])
    r = _dot(h, w_in_ref[:, C_R:D_IN])
    k = kv[:, :D_GLA_K]
    v_bf = kv[:, D_GLA_K:].astype(BF16)
    la_f = _log_decay(r[:, :GLA_RANK], wa2f_ref[...], baf_ref[...])
    la_b = _log_decay(r[:, GLA_RANK:], wa2b_ref[...], bab_ref[...])
    for forward, la, out_ref in ((True, la_f, sf_ref), (False, la_b, sb_ref)):
        consts = _gla_consts(forward)
        order = list(range(n_chunks)) if forward else list(range(n_chunks - 1, -1, -1))
        b = {ci: _gla_cumsum(la[_chunk_rows(ci)], consts) for ci in order}
        s = jnp.zeros((N_HEADS * HEAD_K, HEAD_V), F32)
        for ci in order:
            rows = _chunk_rows(ci)
            dec_col, kv_c, _ = _gla_prep(None, k[rows], v_bf[rows], b[ci], consts, forward, False)
            s = dec_col * s + kv_c
        out_ref[...] = s


def _ctx_call(ctx2d, mod, n1g, w_in_t, wa2f, baf, wa2b, bab):
    st = jax.ShapeDtypeStruct((N_HEADS * HEAD_K, HEAD_V), F32)
    return pl.pallas_call(
        _ctx_kernel,
        out_shape=(st, st, jax.ShapeDtypeStruct(w_in_t.shape[::-1], BF16)),
        compiler_params=pltpu.CompilerParams(vmem_limit_bytes=VMEM_LIMIT),
        name="context_states",
    )(ctx2d, mod, n1g, w_in_t, wa2f, baf, wa2b, bab)


def _pass1_rows(r0, state, x_ref, mod_ref, n1g_ref, w_in_ref, wa2f_ref, baf_ref, wa2b_ref, bab_ref,
                qk_ref, laf_ref, v_ref, g_ref, glu_ref, ob_ref, carry_w, carry_r):
    rows = slice(r0, r0 + SUB)
    proj_blocks, gla_stages = [], []

    if carry_r is not None:
        qk_r, v_r, lab_r = carry_r
        first = r0 // CHUNK
        order = list(range(first + SUB // CHUNK - 1, first - 1, -1))
        consts = _gla_consts(False)
        b = {ci: _gla_cumsum(lab_r[_chunk_rows(ci), :], consts) for ci in order}

    if carry_w is not None:
        qk_w, v_w, lab_w = carry_w
        sh1 = mod_ref[0:1, 0:D_MODEL]
        sc1 = mod_ref[0:1, D_MODEL:2 * D_MODEL]
        h = (_rms_rows(x_ref[rows, :], n1g_ref[...]) * (1.0 + sc1) + sh1).astype(BF16)

        def proj(lo, hi):
            return _dot(h, w_in_ref[:, lo:hi])

        def glu_block(j):
            lo, hi = j * MXU_N, (j + 1) * MXU_N
            glu_ref[rows, lo:hi] = proj(lo, hi) * _sigmoid(proj(D_CONV + lo, D_CONV + hi))

        def qk_block(j):
            lo, hi = j * MXU_N, (j + 1) * MXU_N
            blk = proj(C_Q + lo, C_Q + hi)
            qk_ref[rows, lo:hi] = blk
            qk_w[rows, lo:hi] = blk

        def v_block(j):
            lo, hi = j * MXU_N, (j + 1) * MXU_N
            blk = proj(C_V + lo, C_V + hi).astype(BF16)
            v_ref[rows, lo:hi] = blk
            v_w[rows, lo:hi] = blk

        def g_block(j):
            lo, hi = j * MXU_N, (j + 1) * MXU_N
            g_ref[rows, lo:hi] = proj(C_G + lo, C_G + hi).astype(BF16)

        low_rank = {}

        def rank_block():
            low_rank["r"] = proj(C_R, D_IN)

        def decay_block():
            r = low_rank.pop("r")
            laf_ref[rows, :] = _log_decay(r[:, :GLA_RANK], wa2f_ref[...], baf_ref[...])
            lab_w[rows, :] = _log_decay(r[:, GLA_RANK:], wa2b_ref[...], bab_ref[...])

        proj_blocks = ([rank_block]
                       + [functools.partial(glu_block, j) for j in range(D_CONV // MXU_N)]
                       + [decay_block]
                       + [functools.partial(qk_block, j) for j in range(2 * D_GLA_K // MXU_N)]
                       + [functools.partial(v_block, j) for j in range(D_GLA_V // MXU_N)]
                       + [functools.partial(g_block, j) for j in range(D_GLA_V // MXU_N)])

    if carry_r is not None:
        prep, s_enter = {}, {}

        def gla_prep(ci):
            crows = _chunk_rows(ci)
            qk_c = qk_r[crows, :]
            dec_col, kv, ops = _gla_prep(qk_c[:, :D_GLA_K], qk_c[:, D_GLA_K:], v_r[crows, :], b[ci],
                                         consts, False, True)
            prep[ci] = ops
            s_enter[ci] = state["s"]
            state["s"] = dec_col * state["s"] + kv

        def gla_out(ci):
            ob_ref[_chunk_rows(ci), :] = _gla_out(prep[ci], s_enter[ci]).astype(BF16)

        gla_stages = ([functools.partial(gla_prep, ci) for ci in order]
                      + [functools.partial(gla_out, ci) for ci in order])

    if proj_blocks and gla_stages:
        _interleave(proj_blocks, gla_stages)
    else:
        for thunk in proj_blocks + gla_stages:
            thunk()


def _pass1_step(x_ref, mod_ref, n1g_ref, w_in_ref, wa2f_ref, baf_ref, wa2b_ref, bab_ref,
                qk_ref, laf_ref, v_ref, g_ref, glu_ref, ob_ref, s_ref, carry_w, carry_r):
    state = {"s": s_ref[...]} if carry_r is not None else None
    for r0 in range(x_ref.shape[0] - SUB, -1, -SUB):
        _pass1_rows(r0, state, x_ref, mod_ref, n1g_ref, w_in_ref, wa2f_ref, baf_ref, wa2b_ref,
                    bab_ref, qk_ref, laf_ref, v_ref, g_ref, glu_ref, ob_ref, carry_w, carry_r)
    if carry_r is not None:
        s_ref[...] = state["s"]


def _pass1_kernel(n_steps, n_mod2,
                  x_ref, mod_ref, n1g_ref, w_in_ref, wa2f_ref, baf_ref, wa2b_ref, bab_ref, sb0_ref,
                  wo_f32_ref, wg_f32_ref, wu_f32_ref, wd_f32_ref, silu_ref, w_mod_ref, b_mod_ref,
                  qk_ref, laf_ref, v_ref, g_ref, glu_ref, ob_ref,
                  wo_bf_ref, wg_bf_ref, wu_bf_ref, wd_bf_ref, mod2_ref,
                  s_ref, qk_a, v_a, lab_a, qk_b, v_b, lab_b):
    step = pl.program_id(0)
    last = n_steps - 1

    @pl.when(step < n_mod2)
    def _():
        mod2_ref[...] = (jnp.sum(w_mod_ref[...] * silu_ref[:, 0:1], axis=0, keepdims=True)
                         + b_mod_ref[...])

    for src, dst in ((wo_f32_ref, wo_bf_ref), (wg_f32_ref, wg_bf_ref), (wu_f32_ref, wu_bf_ref),
                     (wd_f32_ref, wd_bf_ref)):
        dst[...] = src[...].astype(BF16)
    bufs = ((qk_a, v_a, lab_a), (qk_b, v_b, lab_b))

    run = functools.partial(_pass1_step, x_ref, mod_ref, n1g_ref, w_in_ref, wa2f_ref, baf_ref,
                            wa2b_ref, bab_ref, qk_ref, laf_ref, v_ref, g_ref, glu_ref, ob_ref, s_ref)

    @pl.when(step == 0)
    def _():
        s_ref[...] = sb0_ref[...]
        run(bufs[0], None)

    for parity in (0, 1):
        @pl.when((step > 0) & (step < last) & (step % 2 == parity))
        def _():
            run(bufs[parity], bufs[1 - parity])

    @pl.when(step == last)
    def _():
        run(None, bufs[(last - 1) % 2])


def _slabs(w, n_slabs):
    rows, cols = w.shape
    assert rows % (n_slabs * 16) == 0
    return w.reshape(n_slabs, rows // n_slabs, cols)


def _pass1_call(x2d, mod, n1g, w_in_bf, wa2f, baf, wa2b, bab, sb0, w_out, w_gate, w_up, w_down,
                silu_ct, w_mod, b_mod):
    n_tok = x2d.shape[0]
    n_tiles = n_tok // TILE1
    mod2_lo = mod.shape[1] // D_MODEL
    mod2_blocks = w_mod.shape[1] // D_MODEL - mod2_lo
    assert mod2_blocks <= n_tiles
    mod2_in = lambda s: (0, mod2_lo + jnp.minimum(s, mod2_blocks - 1))
    mod2_out = lambda s: (0, jnp.minimum(s, mod2_blocks - 1))
    cast_in = [_slabs(w, n) for w, n in ((w_out, 32), (w_gate, 32), (w_up, 32), (w_down, 22))]
    assert all(w.shape[0] <= n_tiles for w in cast_in)
    cast_specs = [pl.BlockSpec((1,) + w.shape[1:],
                               functools.partial(lambda n, s: (jnp.minimum(s, n - 1), 0, 0), w.shape[0]))
                  for w in cast_in]
    proj = lambda s: (n_tiles - 1 - jnp.minimum(s, n_tiles - 1), 0)
    gla = lambda s: (n_tiles - 1 - jnp.maximum(s - 1, 0), 0)
    const = lambda s: (0, 0)
    full = lambda a: pl.BlockSpec(a.shape, const)
    out_shape = (
        jax.ShapeDtypeStruct((n_tok, 2 * D_GLA_K), F32),
        jax.ShapeDtypeStruct((n_tok, D_GLA_K), F32),
        jax.ShapeDtypeStruct((n_tok, D_GLA_V), BF16),
        jax.ShapeDtypeStruct((n_tok, D_GLA_V), BF16),
        jax.ShapeDtypeStruct((n_tok, D_CONV), F32),
        jax.ShapeDtypeStruct((n_tok, D_GLA_V), BF16),
    )
    carry = [pltpu.VMEM((TILE1, 2 * D_GLA_K), F32), pltpu.VMEM((TILE1, D_GLA_V), BF16),
             pltpu.VMEM((TILE1, D_GLA_K), F32)]
    return pl.pallas_call(
        functools.partial(_pass1_kernel, n_tiles + 1, mod2_blocks),
        grid=(n_tiles + 1,),
        in_specs=[pl.BlockSpec((TILE1, D_MODEL), proj), full(mod), full(n1g), full(w_in_bf),
                  full(wa2f), full(baf), full(wa2b), full(bab), full(sb0)] + cast_specs
        + [full(silu_ct), pl.BlockSpec((D_MODEL, D_MODEL), mod2_in), pl.BlockSpec((1, D_MODEL), mod2_in)],
        out_specs=(pl.BlockSpec((TILE1, 2 * D_GLA_K), proj), pl.BlockSpec((TILE1, D_GLA_K), proj),
                   pl.BlockSpec((TILE1, D_GLA_V), proj), pl.BlockSpec((TILE1, D_GLA_V), proj),
                   pl.BlockSpec((TILE1, D_CONV), proj), pl.BlockSpec((TILE1, D_GLA_V), gla))
        + tuple(cast_specs) + (pl.BlockSpec((1, D_MODEL), mod2_out),),
        out_shape=out_shape + tuple(jax.ShapeDtypeStruct(w.shape, BF16) for w in cast_in)
        + (jax.ShapeDtypeStruct((1, mod2_blocks * D_MODEL), F32),),
        scratch_shapes=[pltpu.VMEM((N_HEADS * HEAD_K, HEAD_V), F32)] + carry + carry,
        compiler_params=pltpu.CompilerParams(dimension_semantics=("arbitrary",),
                                             vmem_limit_bytes=VMEM_LIMIT),
        name="pass1_project_gla_bwd",
    )(x2d, mod, n1g, w_in_bf, wa2f, baf, wa2b, bab, sb0, *cast_in, silu_ct, w_mod, b_mod)


def _conv_stage_in(glu_ref, cbuf_ref):
    n_seg = glu_ref.shape[0] // CHUNK
    for j in range(D_CONV // LANES):
        for s in range(n_seg):
            base = s * SEG_PITCH + SEG_LEAD
            cbuf_ref[j, base:base + CHUNK, :] = glu_ref[_chunk_rows(s), j * LANES:(j + 1) * LANES]


def _conv_taps(s, j, cbuf_ref, cw_ref, anchor):
    acc = None
    for t in range(CONV_WIDTH):
        w = anchor(cw_ref[t:t + 1, j * LANES:(j + 1) * LANES])
        start = s * SEG_PITCH + SEG_LEAD - CONV_PAD + t
        term = cbuf_ref[j, start:start + CHUNK, :] * w
        acc = term if acc is None else acc + term
    return acc


def _conv_norm(cols, cb_ref, lng_ref, lnb_ref):
    y = jnp.concatenate(cols, axis=1) + cb_ref[...]
    mu = jnp.mean(y, axis=-1, keepdims=True)
    yc = y - mu
    var = jnp.mean(yc * yc, axis=-1, keepdims=True)
    return _silu(yc * lax.rsqrt(var + EPS) * lng_ref[...] + lnb_ref[...])


def _pass2_step(do_ffn, do_mix, x_ref, qk_ref, laf_ref, v_ref, g_ref, glu_ref, ob_ref, mod_ref,
                n2g_ref, gng_ref, fg_ref, cw_ref, cb_ref, lng_ref, lnb_ref,
                w_out_ref, wg_ref, wu_ref, wd_ref, never_ref, out_ref, s_ref, mix_ref, cbuf_ref):
    n_seg = x_ref.shape[0] // CHUNK
    n_ff = D_FF // MXU_N
    side, ffn = [], []
    latest = {}

    if do_mix:
        chunks = list(range(n_seg))
        consts = _gla_consts(True)
        gng = gng_ref[...]
        _conv_stage_in(glu_ref, cbuf_ref)
        b = {ci: _gla_cumsum(laf_ref[_chunk_rows(ci), :], consts) for ci in chunks}

    if do_ffn:
        g1 = mod_ref[0:1, 0:D_MODEL]
        sh2 = mod_ref[0:1, D_MODEL:2 * D_MODEL]
        sc2 = mod_ref[0:1, 2 * D_MODEL:3 * D_MODEL]
        g2 = mod_ref[0:1, 3 * D_MODEL:4 * D_MODEL]
        x1 = x_ref[...] + g1 * _dot(mix_ref[...], w_out_ref[...])

    if do_mix:
        state = {"s": s_ref[...]}
        prep, s_enter = {}, {}

        def gla_prep(ci):
            rows = _chunk_rows(ci)
            qk_c = qk_ref[rows, :]
            dec_col, kv, ops = _gla_prep(qk_c[:, :D_GLA_K], qk_c[:, D_GLA_K:], v_ref[rows, :], b[ci],
                                         consts, True, True)
            prep[ci] = ops
            s_enter[ci] = state["s"]
            state["s"] = dec_col * state["s"] + kv

        def gla_out(ci):
            rows = _chunk_rows(ci)
            o = _gla_out(prep[ci], s_enter[ci]) + ob_ref[rows, :].astype(F32)
            gate = _silu(g_ref[rows, :].astype(F32))
            for h in range(N_HEADS):
                cols = slice(h * HEAD_V, (h + 1) * HEAD_V)
                mix_ref[rows, D_CONV + h * HEAD_V:D_CONV + (h + 1) * HEAD_V] = (
                    _rms_rows(o[:, cols], gng) * gate[:, cols]).astype(BF16)

        conv_cols = {}
        never = never_ref[0:1, :] != 0

        def conv_taps(s, j):
            anchor = lambda w: w
            if "ffn" in latest:
                after = latest["ffn"]
                anchor = lambda w: jnp.where(never, after, w)
            conv_cols[s, j] = _conv_taps(s, j, cbuf_ref, cw_ref, anchor)

        def conv_out(s):
            cols = [conv_cols.pop((s, j)) for j in range(D_CONV // LANES)]
            mix_ref[_chunk_rows(s), 0:D_CONV] = _conv_norm(cols, cb_ref, lng_ref, lnb_ref).astype(BF16)

        side = [functools.partial(gla_prep, ci) for ci in chunks]
        for s in chunks:
            side += [functools.partial(conv_taps, s, j) for j in range(D_CONV // LANES)]
            side.append(functools.partial(conv_out, s))
        side += [functools.partial(gla_out, ci) for ci in chunks]

    n_early = n_seg if do_ffn else len(side)
    for thunk in side[:n_early]:
        thunk()

    if do_ffn:
        h2 = (_rms_rows(x1, n2g_ref[...]) * (1.0 + sc2) + sh2).astype(BF16)
        act, acc = {}, {}

        def gate_up(j):
            cols = slice(j * MXU_N, (j + 1) * MXU_N)
            gate = _dot(h2, wg_ref[:, cols])
            latest["ffn"] = gate[0:1, 0:LANES]
            act[j] = (_silu(gate) * _dot(h2, wu_ref[:, cols])).astype(BF16)

        def down(j):
            part = _dot(act.pop(j), wd_ref[j * MXU_N:(j + 1) * MXU_N, :])
            acc["v"] = part if j == 0 else acc["v"] + part

        ffn = [functools.partial(gate_up, 0)]
        for j in range(1, n_ff):
            ffn += [functools.partial(gate_up, j), functools.partial(down, j - 1)]
        ffn.append(functools.partial(down, n_ff - 1))
        _interleave(ffn, side[n_early:])
    if do_mix:
        s_ref[...] = state["s"]
    if do_ffn:
        x2 = x1 + g2 * acc["v"]
        out_ref[...] = _rms_rows(x2, fg_ref[...])


def _pass2_kernel(n_steps,
                  x_ref, qk_ref, laf_ref, v_ref, g_ref, glu_ref, ob_ref, mod_ref, n2g_ref,
                  gng_ref, fg_ref, cw_ref, cb_ref, lng_ref, lnb_ref,
                  w_out_ref, wg_ref, wu_ref, wd_ref, sf0_ref, never_ref,
                  out_ref, s_ref, mix_ref, cbuf_ref):
    step = pl.program_id(0)
    last = n_steps - 1
    run = functools.partial(_pass2_step, x_ref=x_ref, qk_ref=qk_ref, laf_ref=laf_ref, v_ref=v_ref,
                            g_ref=g_ref, glu_ref=glu_ref, ob_ref=ob_ref, mod_ref=mod_ref,
                            n2g_ref=n2g_ref, gng_ref=gng_ref, fg_ref=fg_ref, cw_ref=cw_ref,
                            cb_ref=cb_ref, lng_ref=lng_ref, lnb_ref=lnb_ref, w_out_ref=w_out_ref,
                            wg_ref=wg_ref, wu_ref=wu_ref, wd_ref=wd_ref, never_ref=never_ref,
                            out_ref=out_ref, s_ref=s_ref, mix_ref=mix_ref, cbuf_ref=cbuf_ref)

    @pl.when(step == 0)
    def _():
        s_ref[...] = sf0_ref[...]
        cbuf_ref[...] = jnp.zeros(cbuf_ref.shape, F32)
        run(False, True)

    @pl.when((step > 0) & (step < last))
    def _():
        run(True, True)

    @pl.when(step == last)
    def _():
        run(True, False)


def _pass2_call(x2d, qk, laf, v, g, glu, ob, mod, n2g, gng, fg, cw, cb, lng, lnb,
                w_out_bf, wg_bf, wu_bf, wd_bf, sf0, never):
    n_tok = x2d.shape[0]
    n_tiles = n_tok // TILE
    mixer = lambda s: (jnp.minimum(s, n_tiles - 1), 0)
    ffn = lambda s: (jnp.maximum(s - 1, 0), 0)
    const = lambda s: (0, 0)
    full = lambda a: pl.BlockSpec(a.shape, const)
    mixer_spec = lambda w: pl.BlockSpec((TILE, w), mixer)
    return pl.pallas_call(
        functools.partial(_pass2_kernel, n_tiles + 1),
        grid=(n_tiles + 1,),
        in_specs=[pl.BlockSpec((TILE, D_MODEL), ffn), mixer_spec(2 * D_GLA_K), mixer_spec(D_GLA_K),
                  mixer_spec(D_GLA_V), mixer_spec(D_GLA_V), mixer_spec(D_CONV), mixer_spec(D_GLA_V),
                  full(mod), full(n2g), full(gng), full(fg), full(cw), full(cb), full(lng), full(lnb),
                  full(w_out_bf), full(wg_bf), full(wu_bf), full(wd_bf), full(sf0),
                  full(never)],
        out_specs=pl.BlockSpec((TILE, D_MODEL), ffn),
        out_shape=jax.ShapeDtypeStruct((n_tok, D_MODEL), F32),
        scratch_shapes=[pltpu.VMEM((N_HEADS * HEAD_K, HEAD_V), F32),
                        pltpu.VMEM((TILE, D_CONV + D_GLA_V), BF16),
                        pltpu.VMEM((D_CONV // LANES, (TILE // CHUNK) * SEG_PITCH, LANES), F32)],
        compiler_params=pltpu.CompilerParams(dimension_semantics=("arbitrary",),
                                             vmem_limit_bytes=VMEM_LIMIT),
        name="pass2_gla_fwd_conv_merge_ffn",
    )(x2d, qk, laf, v, g, glu, ob, mod, n2g, gng, fg, cw, cb, lng, lnb,
      w_out_bf, wg_bf, wu_bf, wd_bf, sf0, never)


def kernel(x, c, ctx, c_ctx, w_mod, b_mod, norm1_g, norm2_g, w_in, conv_w, conv_b, conv_ln_g,
           conv_ln_b, w_a2_f, b_a_f, w_a2_b, b_a_b, gla_norm_g, w_out, w_gate, w_up, w_down, final_g):
    bsz, n_lat, d = x.shape
    assert bsz == 1 and d == D_MODEL and n_lat % TILE == 0 and ctx.shape[1] % CHUNK == 0
    assert n_lat % TILE1 == 0 and TILE1 % SUB == 0
    assert w_mod.shape[0] == 1, "single layer"
    row = lambda a: a.reshape(1, -1)

    ct = jnp.concatenate([c.reshape(D_MODEL, 1), c_ctx.reshape(D_MODEL, 1)], axis=1)
    mod, silu_ct = _mod_call(ct, w_mod[0], row(b_mod[0]))

    wa2f, baf = w_a2_f[0], row(b_a_f[0])
    wa2b, bab = w_a2_b[0], row(b_a_b[0])
    n1g = row(norm1_g[0])

    sf0, sb0, w_in_bf = _ctx_call(ctx[0], mod, n1g, w_in[0].T, wa2f, baf, wa2b, bab)

    x2d = x[0]
    qk, laf, v, g, glu, ob, wo_bf, wg_bf, wu_bf, wd_bf, mod2 = _pass1_call(
        x2d, mod, n1g, w_in_bf, wa2f, baf, wa2b, bab, sb0, w_out[0], w_gate[0], w_up[0], w_down[0],
        silu_ct, w_mod[0], row(b_mod[0]))
    unslab = lambda w: w.reshape(-1, w.shape[-1])

    out = _pass2_call(
        x2d, qk, laf, v, g, glu, ob, mod2, row(norm2_g[0]), row(gla_norm_g[0]), row(final_g),
        conv_w[0], row(conv_b[0]), row(conv_ln_g[0]), row(conv_ln_b[0]),
        unslab(wo_bf), unslab(wg_bf), unslab(wu_bf), unslab(wd_bf), sf0,
        jnp.zeros((CHUNK, LANES), jnp.int32))
    return out.reshape(bsz, n_lat, d)
```

```python
import functools

import jax
import jax.numpy as jnp
from jax import lax
from jax.experimental import pallas as pl
from jax.experimental.pallas import tpu as pltpu

D_MODEL = 1024
D_CONV = 512
CONV_WIDTH = 31
CONV_PAD = 15
N_HEADS = 4
HEAD_K = 64
HEAD_V = 128
D_GLA_K = N_HEADS * HEAD_K
D_GLA_V = N_HEADS * HEAD_V
GLA_RANK = 16
GLA_TAU = 16.0
CHUNK = 64
EPS = 1e-6
D_FF = 2816

C_Q = 2 * D_CONV
C_K = C_Q + D_GLA_K
C_V = C_K + D_GLA_K
C_G = C_V + D_GLA_V
C_R = C_G + D_GLA_V
D_IN = C_R + 2 * GLA_RANK

LANES = 128
MXU_N = 256
SEG_PITCH = 96
SEG_LEAD = 16
SUB = 256
TILE1 = 512
TILE = 512
VMEM_LIMIT = 56 * 1024 * 1024

F32 = jnp.float32
BF16 = jnp.bfloat16


def _dot(a, b):
    return jnp.dot(a, b, preferred_element_type=F32)


def _dot_nt(a, b):
    return lax.dot_general(a, b, (((1,), (1,)), ((), ())), preferred_element_type=F32)


def _sigmoid(x):
    return 1.0 / (1.0 + jnp.exp(-x))


def _silu(x):
    return x * _sigmoid(x)


def _log_sigmoid(z):
    return jnp.minimum(z, 0.0) - jnp.log(1.0 + jnp.exp(-jnp.abs(z)))


def _rms_rows(x, g):
    ms = jnp.mean(x * x, axis=-1, keepdims=True)
    return x * lax.rsqrt(ms + EPS) * g


def _interleave(main, side):
    n_main, n_side = len(main), len(side)
    done = 0
    for i, thunk in enumerate(main):
        thunk()
        want = ((i + 1) * n_side) // n_main
        while done < want:
            side[done]()
            done += 1


def _gla_consts(forward):
    r = lax.broadcasted_iota(jnp.int32, (CHUNK, CHUNK), 0)
    c = lax.broadcasted_iota(jnp.int32, (CHUNK, CHUNK), 1)
    tri = (c <= r) if forward else (c >= r)
    tri_bf = jnp.where(tri, 1.0, 0.0).astype(BF16)
    rr = lax.broadcasted_iota(jnp.int32, (N_HEADS * CHUNK, LANES), 0) % CHUNK
    cc = lax.broadcasted_iota(jnp.int32, (N_HEADS * CHUNK, LANES), 1)
    keep = ((cc <= rr) if forward else (cc >= rr)) & (cc < CHUNK)
    lane = lax.broadcasted_iota(jnp.int32, (CHUNK, D_GLA_K), 1)
    head_masks = [(lane >= h * HEAD_K) & (lane < (h + 1) * HEAD_K) for h in range(N_HEADS)]
    return tri_bf, keep, head_masks


def _gla_cumsum(la, consts):
    tri_bf = consts[0]
    la_hi = la.astype(BF16)
    la_lo = (la - la_hi.astype(F32)).astype(BF16)
    return _dot(tri_bf, la_hi) + _dot(tri_bf, la_lo)


def _gla_prep(q, k, v_bf, b, consts, forward, with_output):
    _, keep, head_masks = consts
    b_last = b[CHUNK - 1:CHUNK, :] if forward else b[0:1, :]
    k_w = k * jnp.exp(b_last - b)
    e_last = jnp.exp(b_last)

    tb = jnp.concatenate(
        [k_w, jnp.broadcast_to(e_last, (8, D_GLA_K)), jnp.zeros((CHUNK - 8, D_GLA_K), F32)], axis=0)
    tbt = tb.T
    dec_col = tbt[:, CHUNK:CHUNK + 1]
    kwt_bf = tbt.astype(BF16)
    zeros_v = jnp.zeros((CHUNK, HEAD_V), BF16)
    v_pads = [jnp.concatenate([v_bf[:, h * HEAD_V:(h + 1) * HEAD_V], zeros_v], axis=0)
              for h in range(N_HEADS)]
    kv = jnp.concatenate([_dot(kwt_bf[h * CHUNK:(h + 1) * CHUNK], v_pads[h])
                          for h in range(N_HEADS)], axis=0)
    if not with_output:
        return dec_col, kv, None
    q_t = (q * jnp.exp(b) * (HEAD_K ** -0.5)).astype(BF16)
    k_t = (k * jnp.exp(-b)).astype(BF16)
    qm = jnp.concatenate([jnp.where(m, q_t, jnp.zeros_like(q_t)) for m in head_masks], axis=0)
    kt_pad = jnp.concatenate([k_t, jnp.zeros_like(k_t)], axis=0)
    attn = _dot_nt(qm, kt_pad)
    p_bf = jnp.where(keep, attn, 0.0).astype(BF16)
    return dec_col, kv, (qm, p_bf, v_pads)


def _gla_out(ops, s_enter):
    qm, p_bf, v_pads = ops
    o_inter = _dot(qm, s_enter.astype(BF16))
    outs = []
    for h in range(N_HEADS):
        rows = slice(h * CHUNK, (h + 1) * CHUNK)
        outs.append(_dot(p_bf[rows], v_pads[h]) + o_inter[rows])
    return jnp.concatenate(outs, axis=1)


def _log_decay(r, w_a2, b_a):
    z = _dot(r.astype(BF16), w_a2.astype(BF16)) + b_a
    return _log_sigmoid(z) * (1.0 / GLA_TAU)


def _chunk_rows(ci):
    return slice(ci * CHUNK, (ci + 1) * CHUNK)


def _mod_kernel(ct_ref, w_ref, b_ref, o_ref, s_ref):
    s = _silu(ct_ref[...])
    s_ref[...] = s
    w = w_ref[...]
    for j in range(2):
        o_ref[j:j + 1, :] = jnp.sum(w * s[:, j:j + 1], axis=0, keepdims=True) + b_ref[...]


def _mod_call(ct, w_mod, b_mod):
    tn = 512
    n = 2 * D_MODEL
    return pl.pallas_call(
        _mod_kernel,
        grid=(n // tn,),
        in_specs=[pl.BlockSpec((D_MODEL, 2), lambda j: (0, 0)),
                  pl.BlockSpec((D_MODEL, tn), lambda j: (0, j)),
                  pl.BlockSpec((1, tn), lambda j: (0, j))],
        out_specs=(pl.BlockSpec((2, tn), lambda j: (0, j)),
                   pl.BlockSpec((D_MODEL, 2), lambda j: (0, 0))),
        out_shape=(jax.ShapeDtypeStruct((2, n), F32), jax.ShapeDtypeStruct((D_MODEL, 2), F32)),
        compiler_params=pltpu.CompilerParams(dimension_semantics=("arbitrary",),
                                             vmem_limit_bytes=VMEM_LIMIT),
        name="adaln_mod",
    )(ct, w_mod, b_mod)


def _ctx_kernel(ctx_ref, mod_ref, n1g_ref, w_in_t_ref, wa2f_ref, baf_ref, wa2b_ref, bab_ref,
                sf_ref, sb_ref, w_in_ref):
    n_full = D_IN // MXU_N
    for j in range(n_full):
        cols = slice(j * MXU_N, (j + 1) * MXU_N)
        w_in_ref[:, cols] = w_in_t_ref[cols, :].T.astype(BF16)
    tail = w_in_t_ref[D_IN - LANES:D_IN, :].T
    w_in_ref[:, n_full * MXU_N:D_IN] = tail[:, LANES - (D_IN - n_full * MXU_N):].astype(BF16)
    n_chunks = ctx_ref.shape[0] // CHUNK
    sh = mod_ref[1:2, 0:D_MODEL]
    sc = mod_ref[1:2, D_MODEL:2 * D_MODEL]
    h = (_rms_rows(ctx_ref[...], n1g_ref[...]) * (1.0 + sc) + sh).astype(BF16)
    kv = _dot(h, w_in_ref[:, C_K:C_G])
    r = _dot(h, w_in_ref[:, C_R:D_IN])
    k = kv[:, :D_GLA_K]
    v_bf = kv[:, D_GLA_K:].astype(BF16)
    la_f = _log_decay(r[:, :GLA_RANK], wa2f_ref[...], baf_ref[...])
    la_b = _log_decay(r[:, GLA_RANK:], wa2b_ref[...], bab_ref[...])
    for forward, la, out_ref in ((True, la_f, sf_ref), (False, la_b, sb_ref)):
        consts = _gla_consts(forward)
        order = list(range(n_chunks)) if forward else list(range(n_chunks - 1, -1, -1))
        b = {ci: _gla_cumsum(la[_chunk_rows(ci)], consts) for ci in order}
        s = jnp.zeros((N_HEADS * HEAD_K, HEAD_V), F32)
        for ci in order:
            rows = _chunk_rows(ci)
            dec_col, kv_c, _ = _gla_prep(None, k[rows], v_bf[rows], b[ci], consts, forward, False)
            s = dec_col * s + kv_c
        out_ref[...] = s


def _ctx_call(ctx2d, mod, n1g, w_in_t, wa2f, baf, wa2b, bab):
    st = jax.ShapeDtypeStruct((N_HEADS * HEAD_K, HEAD_V), F32)
    return pl.pallas_call(
        _ctx_kernel,
        out_shape=(st, st, jax.ShapeDtypeStruct(w_in_t.shape[::-1], BF16)),
        compiler_params=pltpu.CompilerParams(vmem_limit_bytes=VMEM_LIMIT),
        name="context_states",
    )(ctx2d, mod, n1g, w_in_t, wa2f, baf, wa2b, bab)


def _pass1_rows(r0, state, x_ref, mod_ref, n1g_ref, w_in_ref, wa2f_ref, baf_ref, wa2b_ref, bab_ref,
                qk_ref, laf_ref, v_ref, g_ref, glu_ref, ob_ref, carry_w, carry_r):
    rows = slice(r0, r0 + SUB)
    proj_blocks, gla_stages = [], []

    if carry_r is not None:
        qk_r, v_r, lab_r = carry_r
        first = r0 // CHUNK
        order = list(range(first + SUB // CHUNK - 1, first - 1, -1))
        consts = _gla_consts(False)
        b = {ci: _gla_cumsum(lab_r[_chunk_rows(ci), :], consts) for ci in order}

    if carry_w is not None:
        qk_w, v_w, lab_w = carry_w
        sh1 = mod_ref[0:1, 0:D_MODEL]
        sc1 = mod_ref[0:1, D_MODEL:2 * D_MODEL]
        h = (_rms_rows(x_ref[rows, :], n1g_ref[...]) * (1.0 + sc1) + sh1).astype(BF16)

        def proj(lo, hi):
            return _dot(h, w_in_ref[:, lo:hi])

        def glu_block(j):
            lo, hi = j * MXU_N, (j + 1) * MXU_N
            glu_ref[rows, lo:hi] = proj(lo, hi) * _sigmoid(proj(D_CONV + lo, D_CONV + hi))

        def qk_block(j):
            lo, hi = j * MXU_N, (j + 1) * MXU_N
            blk = proj(C_Q + lo, C_Q + hi)
            qk_ref[rows, lo:hi] = blk
            qk_w[rows, lo:hi] = blk

        def v_block(j):
            lo, hi = j * MXU_N, (j + 1) * MXU_N
            blk = proj(C_V + lo, C_V + hi).astype(BF16)
            v_ref[rows, lo:hi] = blk
            v_w[rows, lo:hi] = blk

        def g_block(j):
            lo, hi = j * MXU_N, (j + 1) * MXU_N
            g_ref[rows, lo:hi] = proj(C_G + lo, C_G + hi).astype(BF16)

        low_rank = {}

        def rank_block():
            low_rank["r"] = proj(C_R, D_IN)

        def decay_block():
            r = low_rank.pop("r")
            laf_ref[rows, :] = _log_decay(r[:, :GLA_RANK], wa2f_ref[...], baf_ref[...])
            lab_w[rows, :] = _log_decay(r[:, GLA_RANK:], wa2b_ref[...], bab_ref[...])

        proj_blocks = ([rank_block]
                       + [functools.partial(glu_block, j) for j in range(D_CONV // MXU_N)]
                       + [decay_block]
                       + [functools.partial(qk_block, j) for j in range(2 * D_GLA_K // MXU_N)]
                       + [functools.partial(v_block, j) for j in range(D_GLA_V // MXU_N)]
                       + [functools.partial(g_block, j) for j in range(D_GLA_V // MXU_N)])

    if carry_r is not None:
        prep, s_enter = {}, {}

        def gla_prep(ci):
            crows = _chunk_rows(ci)
            qk_c = qk_r[crows, :]
            dec_col, kv, ops = _gla_prep(qk_c[:, :D_GLA_K], qk_c[:, D_GLA_K:], v_r[crows, :], b[ci],
                                         consts, False, True)
            prep[ci] = ops
            s_enter[ci] = state["s"]
            state["s"] = dec_col * state["s"] + kv

        def gla_out(ci):
            ob_ref[_chunk_rows(ci), :] = _gla_out(prep[ci], s_enter[ci]).astype(BF16)

        gla_stages = ([functools.partial(gla_prep, ci) for ci in order]
                      + [functools.partial(gla_out, ci) for ci in order])

    if proj_blocks and gla_stages:
        _interleave(proj_blocks, gla_stages)
    else:
        for thunk in proj_blocks + gla_stages:
            thunk()


def _pass1_step(x_ref, mod_ref, n1g_ref, w_in_ref, wa2f_ref, baf_ref, wa2b_ref, bab_ref,
                qk_ref, laf_ref, v_ref, g_ref, glu_ref, ob_ref, s_ref, carry_w, carry_r):
    state = {"s": s_ref[...]} if carry_r is not None else None
    for r0 in range(x_ref.shape[0] - SUB, -1, -SUB):
        _pass1_rows(r0, state, x_ref, mod_ref, n1g_ref, w_in_ref, wa2f_ref, baf_ref, wa2b_ref,
                    bab_ref, qk_ref, laf_ref, v_ref, g_ref, glu_ref, ob_ref, carry_w, carry_r)
    if carry_r is not None:
        s_ref[...] = state["s"]


def _pass1_kernel(n_steps, n_mod2,
                  x_ref, mod_ref, n1g_ref, w_in_ref, wa2f_ref, baf_ref, wa2b_ref, bab_ref, sb0_ref,
                  wo_f32_ref, wg_f32_ref, wu_f32_ref, wd_f32_ref, silu_ref, w_mod_ref, b_mod_ref,
                  qk_ref, laf_ref, v_ref, g_ref, glu_ref, ob_ref,
                  wo_bf_ref, wg_bf_ref, wu_bf_ref, wd_bf_ref, mod2_ref,
                  s_ref, qk_a, v_a, lab_a, qk_b, v_b, lab_b):
    step = pl.program_id(0)
    last = n_steps - 1

    @pl.when(step < n_mod2)
    def _():
        mod2_ref[...] = (jnp.sum(w_mod_ref[...] * silu_ref[:, 0:1], axis=0, keepdims=True)
                         + b_mod_ref[...])

    for src, dst in ((wo_f32_ref, wo_bf_ref), (wg_f32_ref, wg_bf_ref), (wu_f32_ref, wu_bf_ref),
                     (wd_f32_ref, wd_bf_ref)):
        dst[...] = src[...].astype(BF16)
    bufs = ((qk_a, v_a, lab_a), (qk_b, v_b, lab_b))

    run = functools.partial(_pass1_step, x_ref, mod_ref, n1g_ref, w_in_ref, wa2f_ref, baf_ref,
                            wa2b_ref, bab_ref, qk_ref, laf_ref, v_ref, g_ref, glu_ref, ob_ref, s_ref)

    @pl.when(step == 0)
    def _():
        s_ref[...] = sb0_ref[...]
        run(bufs[0], None)

    for parity in (0, 1):
        @pl.when((step > 0) & (step < last) & (step % 2 == parity))
        def _():
            run(bufs[parity], bufs[1 - parity])

    @pl.when(step == last)
    def _():
        run(None, bufs[(last - 1) % 2])


def _slabs(w, n_slabs):
    rows, cols = w.shape
    assert rows % (n_slabs * 16) == 0
    return w.reshape(n_slabs, rows // n_slabs, cols)


def _pass1_call(x2d, mod, n1g, w_in_bf, wa2f, baf, wa2b, bab, sb0, w_out, w_gate, w_up, w_down,
                silu_ct, w_mod, b_mod):
    n_tok = x2d.shape[0]
    n_tiles = n_tok // TILE1
    mod2_lo = mod.shape[1] // D_MODEL
    mod2_blocks = w_mod.shape[1] // D_MODEL - mod2_lo
    assert mod2_blocks <= n_tiles
    mod2_in = lambda s: (0, mod2_lo + jnp.minimum(s, mod2_blocks - 1))
    mod2_out = lambda s: (0, jnp.minimum(s, mod2_blocks - 1))
    cast_in = [_slabs(w, n) for w, n in ((w_out, 32), (w_gate, 32), (w_up, 32), (w_down, 22))]
    assert all(w.shape[0] <= n_tiles for w in cast_in)
    cast_specs = [pl.BlockSpec((1,) + w.shape[1:],
                               functools.partial(lambda n, s: (jnp.minimum(s, n - 1), 0, 0), w.shape[0]))
                  for w in cast_in]
    proj = lambda s: (n_tiles - 1 - jnp.minimum(s, n_tiles - 1), 0)
    gla = lambda s: (n_tiles - 1 - jnp.maximum(s - 1, 0), 0)
    const = lambda s: (0, 0)
    full = lambda a: pl.BlockSpec(a.shape, const)
    out_shape = (
        jax.ShapeDtypeStruct((n_tok, 2 * D_GLA_K), F32),
        jax.ShapeDtypeStruct((n_tok, D_GLA_K), F32),
        jax.ShapeDtypeStruct((n_tok, D_GLA_V), BF16),
        jax.ShapeDtypeStruct((n_tok, D_GLA_V), BF16),
        jax.ShapeDtypeStruct((n_tok, D_CONV), F32),
        jax.ShapeDtypeStruct((n_tok, D_GLA_V), BF16),
    )
    carry = [pltpu.VMEM((TILE1, 2 * D_GLA_K), F32), pltpu.VMEM((TILE1, D_GLA_V), BF16),
             pltpu.VMEM((TILE1, D_GLA_K), F32)]
    return pl.pallas_call(
        functools.partial(_pass1_kernel, n_tiles + 1, mod2_blocks),
        grid=(n_tiles + 1,),
        in_specs=[pl.BlockSpec((TILE1, D_MODEL), proj), full(mod), full(n1g), full(w_in_bf),
                  full(wa2f), full(baf), full(wa2b), full(bab), full(sb0)] + cast_specs
        + [full(silu_ct), pl.BlockSpec((D_MODEL, D_MODEL), mod2_in), pl.BlockSpec((1, D_MODEL), mod2_in)],
        out_specs=(pl.BlockSpec((TILE1, 2 * D_GLA_K), proj), pl.BlockSpec((TILE1, D_GLA_K), proj),
                   pl.BlockSpec((TILE1, D_GLA_V), proj), pl.BlockSpec((TILE1, D_GLA_V), proj),
                   pl.BlockSpec((TILE1, D_CONV), proj), pl.BlockSpec((TILE1, D_GLA_V), gla))
        + tuple(cast_specs) + (pl.BlockSpec((1, D_MODEL), mod2_out),),
        out_shape=out_shape + tuple(jax.ShapeDtypeStruct(w.shape, BF16) for w in cast_in)
        + (jax.ShapeDtypeStruct((1, mod2_blocks * D_MODEL), F32),),
        scratch_shapes=[pltpu.VMEM((N_HEADS * HEAD_K, HEAD_V), F32)] + carry + carry,
        compiler_params=pltpu.CompilerParams(dimension_semantics=("arbitrary",),
                                             vmem_limit_bytes=VMEM_LIMIT),
        name="pass1_project_gla_bwd",
    )(x2d, mod, n1g, w_in_bf, wa2f, baf, wa2b, bab, sb0, *cast_in, silu_ct, w_mod, b_mod)


def _conv_stage_in(glu_ref, cbuf_ref):
    n_seg = glu_ref.shape[0] // CHUNK
    for j in range(D_CONV // LANES):
        for s in range(n_seg):
            base = s * SEG_PITCH + SEG_LEAD
            cbuf_ref[j, base:base + CHUNK, :] = glu_ref[_chunk_rows(s), j * LANES:(j + 1) * LANES]


def _conv_taps(s, j, cbuf_ref, cw_ref, anchor):
    acc = None
    for t in range(CONV_WIDTH):
        w = anchor(cw_ref[t:t + 1, j * LANES:(j + 1) * LANES])
        start = s * SEG_PITCH + SEG_LEAD - CONV_PAD + t
        term = cbuf_ref[j, start:start + CHUNK, :] * w
        acc = term if acc is None else acc + term
    return acc


def _conv_norm(cols, cb_ref, lng_ref, lnb_ref):
    y = jnp.concatenate(cols, axis=1) + cb_ref[...]
    mu = jnp.mean(y, axis=-1, keepdims=True)
    yc = y - mu
    var = jnp.mean(yc * yc, axis=-1, keepdims=True)
    return _silu(yc * lax.rsqrt(var + EPS) * lng_ref[...] + lnb_ref[...])


def _pass2_step(do_ffn, do_mix, x_ref, qk_ref, laf_ref, v_ref, g_ref, glu_ref, ob_ref, mod_ref,
                n2g_ref, gng_ref, fg_ref, cw_ref, cb_ref, lng_ref, lnb_ref,
                w_out_ref, wg_ref, wu_ref, wd_ref, never_ref, out_ref, s_ref, mix_ref, cbuf_ref):
    n_seg = x_ref.shape[0] // CHUNK
    n_ff = D_FF // MXU_N
    side, ffn = [], []
    latest = {}

    if do_mix:
        chunks = list(range(n_seg))
        consts = _gla_consts(True)
        gng = gng_ref[...]
        _conv_stage_in(glu_ref, cbuf_ref)
        b = {ci: _gla_cumsum(laf_ref[_chunk_rows(ci), :], consts) for ci in chunks}

    if do_ffn:
        g1 = mod_ref[0:1, 0:D_MODEL]
        sh2 = mod_ref[0:1, D_MODEL:2 * D_MODEL]
        sc2 = mod_ref[0:1, 2 * D_MODEL:3 * D_MODEL]
        g2 = mod_ref[0:1, 3 * D_MODEL:4 * D_MODEL]
        x1 = x_ref[...] + g1 * _dot(mix_ref[...], w_out_ref[...])

    if do_mix:
        state = {"s": s_ref[...]}
        prep, s_enter = {}, {}

        def gla_prep(ci):
            rows = _chunk_rows(ci)
            qk_c = qk_ref[rows, :]
            dec_col, kv, ops = _gla_prep(qk_c[:, :D_GLA_K], qk_c[:, D_GLA_K:], v_ref[rows, :], b[ci],
                                         consts, True, True)
            prep[ci] = ops
            s_enter[ci] = state["s"]
            state["s"] = dec_col * state["s"] + kv

        def gla_out(ci):
            rows = _chunk_rows(ci)
            o = _gla_out(prep[ci], s_enter[ci]) + ob_ref[rows, :].astype(F32)
            gate = _silu(g_ref[rows, :].astype(F32))
            for h in range(N_HEADS):
                cols = slice(h * HEAD_V, (h + 1) * HEAD_V)
                mix_ref[rows, D_CONV + h * HEAD_V:D_CONV + (h + 1) * HEAD_V] = (
                    _rms_rows(o[:, cols], gng) * gate[:, cols]).astype(BF16)

        conv_cols = {}
        never = never_ref[0:1, :] != 0

        def conv_taps(s, j):
            anchor = lambda w: w
            if "ffn" in latest:
                after = latest["ffn"]
                anchor = lambda w: jnp.where(never, after, w)
            conv_cols[s, j] = _conv_taps(s, j, cbuf_ref, cw_ref, anchor)

        def conv_out(s):
            cols = [conv_cols.pop((s, j)) for j in range(D_CONV // LANES)]
            mix_ref[_chunk_rows(s), 0:D_CONV] = _conv_norm(cols, cb_ref, lng_ref, lnb_ref).astype(BF16)

        side = [functools.partial(gla_prep, ci) for ci in chunks]
        for s in chunks:
            side += [functools.partial(conv_taps, s, j) for j in range(D_CONV // LANES)]
            side.append(functools.partial(conv_out, s))
        side += [functools.partial(gla_out, ci) for ci in chunks]

    n_early = n_seg if do_ffn else len(side)
    for thunk in side[:n_early]:
        thunk()

    if do_ffn:
        h2 = (_rms_rows(x1, n2g_ref[...]) * (1.0 + sc2) + sh2).astype(BF16)
        act, acc = {}, {}

        def gate_up(j):
            cols = slice(j * MXU_N, (j + 1) * MXU_N)
            gate = _dot(h2, wg_ref[:, cols])
            latest["ffn"] = gate[0:1, 0:LANES]
            act[j] = (_silu(gate) * _dot(h2, wu_ref[:, cols])).astype(BF16)

        def down(j):
            part = _dot(act.pop(j), wd_ref[j * MXU_N:(j + 1) * MXU_N, :])
            acc["v"] = part if j == 0 else acc["v"] + part

        ffn = [functools.partial(gate_up, 0)]
        for j in range(1, n_ff):
            ffn += [functools.partial(gate_up, j), functools.partial(down, j - 1)]
        ffn.append(functools.partial(down, n_ff - 1))
        _interleave(ffn, side[n_early:])
    if do_mix:
        s_ref[...] = state["s"]
    if do_ffn:
        x2 = x1 + g2 * acc["v"]
        out_ref[...] = _rms_rows(x2, fg_ref[...])


def _pass2_kernel(n_steps,
                  x_ref, qk_ref, laf_ref, v_ref, g_ref, glu_ref, ob_ref, mod_ref, n2g_ref,
                  gng_ref, fg_ref, cw_ref, cb_ref, lng_ref, lnb_ref,
                  w_out_ref, wg_ref, wu_ref, wd_ref, sf0_ref, never_ref,
                  out_ref, s_ref, mix_ref, cbuf_ref):
    step = pl.program_id(0)
    last = n_steps - 1
    run = functools.partial(_pass2_step, x_ref=x_ref, qk_ref=qk_ref, laf_ref=laf_ref, v_ref=v_ref,
                            g_ref=g_ref, glu_ref=glu_ref, ob_ref=ob_ref, mod_ref=mod_ref,
                            n2g_ref=n2g_ref, gng_ref=gng_ref, fg_ref=fg_ref, cw_ref=cw_ref,
                            cb_ref=cb_ref, lng_ref=lng_ref, lnb_ref=lnb_ref, w_out_ref=w_out_ref,
                            wg_ref=wg_ref, wu_ref=wu_ref, wd_ref=wd_ref, never_ref=never_ref,
                            out_ref=out_ref, s_ref=s_ref, mix_ref=mix_ref, cbuf_ref=cbuf_ref)

    @pl.when(step == 0)
    def _():
        s_ref[...] = sf0_ref[...]
        cbuf_ref[...] = jnp.zeros(cbuf_ref.shape, F32)
        run(False, True)

    @pl.when((step > 0) & (step < last))
    def _():
        run(True, True)

    @pl.when(step == last)
    def _():
        run(True, False)


def _pass2_call(x2d, qk, laf, v, g, glu, ob, mod, n2g, gng, fg, cw, cb, lng, lnb,
                w_out_bf, wg_bf, wu_bf, wd_bf, sf0, never):
    n_tok = x2d.shape[0]
    n_tiles = n_tok // TILE
    mixer = lambda s: (jnp.minimum(s, n_tiles - 1), 0)
    ffn = lambda s: (jnp.maximum(s - 1, 0), 0)
    const = lambda s: (0, 0)
    full = lambda a: pl.BlockSpec(a.shape, const)
    mixer_spec = lambda w: pl.BlockSpec((TILE, w), mixer)
    return pl.pallas_call(
        functools.partial(_pass2_kernel, n_tiles + 1),
        grid=(n_tiles + 1,),
        in_specs=[pl.BlockSpec((TILE, D_MODEL), ffn), mixer_spec(2 * D_GLA_K), mixer_spec(D_GLA_K),
                  mixer_spec(D_GLA_V), mixer_spec(D_GLA_V), mixer_spec(D_CONV), mixer_spec(D_GLA_V),
                  full(mod), full(n2g), full(gng), full(fg), full(cw), full(cb), full(lng), full(lnb),
                  full(w_out_bf), full(wg_bf), full(wu_bf), full(wd_bf), full(sf0),
                  full(never)],
        out_specs=pl.BlockSpec((TILE, D_MODEL), ffn),
        out_shape=jax.ShapeDtypeStruct((n_tok, D_MODEL), F32),
        scratch_shapes=[pltpu.VMEM((N_HEADS * HEAD_K, HEAD_V), F32),
                        pltpu.VMEM((TILE, D_CONV + D_GLA_V), BF16),
                        pltpu.VMEM((D_CONV // LANES, (TILE // CHUNK) * SEG_PITCH, LANES), F32)],
        compiler_params=pltpu.CompilerParams(dimension_semantics=("arbitrary",),
                                             vmem_limit_bytes=VMEM_LIMIT),
        name="pass2_gla_fwd_conv_merge_ffn",
    )(x2d, qk, laf, v, g, glu, ob, mod, n2g, gng, fg, cw, cb, lng, lnb,
      w_out_bf, wg_bf, wu_bf, wd_bf, sf0, never)


def kernel(x, c, ctx, c_ctx, w_mod, b_mod, norm1_g, norm2_g, w_in, conv_w, conv_b, conv_ln_g,
           conv_ln_b, w_a2_f, b_a_f, w_a2_b, b_a_b, gla_norm_g, w_out, w_gate, w_up, w_down, final_g):
    bsz, n_lat, d = x.shape
    assert bsz == 1 and d == D_MODEL and n_lat % TILE == 0 and ctx.shape[1] % CHUNK == 0
    assert n_lat % TILE1 == 0 and TILE1 % SUB == 0
    assert w_mod.shape[0] == 1, "single layer"
    row = lambda a: a.reshape(1, -1)

    ct = jnp.concatenate([c.reshape(D_MODEL, 1), c_ctx.reshape(D_MODEL, 1)], axis=1)
    mod, silu_ct = _mod_call(ct, w_mod[0], row(b_mod[0]))

    wa2f, baf = w_a2_f[0], row(b_a_f[0])
    wa2b, bab = w_a2_b[0], row(b_a_b[0])
    n1g = row(norm1_g[0])

    sf0, sb0, w_in_bf = _ctx_call(ctx[0], mod, n1g, w_in[0].T, wa2f, baf, wa2b, bab)

    x2d = x[0]
    qk, laf, v, g, glu, ob, wo_bf, wg_bf, wu_bf, wd_bf, mod2 = _pass1_call(
        x2d, mod, n1g, w_in_bf, wa2f, baf, wa2b, bab, sb0, w_out[0], w_gate[0], w_up[0], w_down[0],
        silu_ct, w_mod[0], row(b_mod[0]))
    unslab = lambda w: w.reshape(-1, w.shape[-1])

    out = _pass2_call(
        x2d, qk, laf, v, g, glu, ob, mod2, row(norm2_g[0]), row(gla_norm_g[0]), row(final_g),
        conv_w[0], row(conv_b[0]), row(conv_ln_g[0]), row(conv_ln_b[0]),
        unslab(wo_bf), unslab(wg_bf), unslab(wu_bf), unslab(wd_bf), sf0,
        jnp.zeros((CHUNK, LANES), jnp.int32))
    return out.reshape(bsz, n_lat, d)
```

```python
import functools

import jax
import jax.numpy as jnp
from jax import lax
from jax.experimental import pallas as pl
from jax.experimental.pallas import tpu as pltpu

D_MODEL = 1024
D_CONV = 512
CONV_WIDTH = 31
CONV_PAD = 15
N_HEADS = 4
HEAD_K = 64
HEAD_V = 128
D_GLA_K = N_HEADS * HEAD_K
D_GLA_V = N_HEADS * HEAD_V
GLA_RANK = 16
GLA_TAU = 16.0
CHUNK = 64
EPS = 1e-6
D_FF = 2816

C_Q = 2 * D_CONV
C_K = C_Q + D_GLA_K
C_V = C_K + D_GLA_K
C_G = C_V + D_GLA_V
C_R = C_G + D_GLA_V
D_IN = C_R + 2 * GLA_RANK

LANES = 128
MXU_N = 256
SEG_PITCH = 96
SEG_LEAD = 16
CONV_GROUP = 2
SUB = 256
TILE1 = 512
TILE = 256
VMEM_LIMIT = 56 * 1024 * 1024

F32 = jnp.float32
BF16 = jnp.bfloat16


def _dot(a, b):
    return jnp.dot(a, b, preferred_element_type=F32)


def _dot_nt(a, b):
    return lax.dot_general(a, b, (((1,), (1,)), ((), ())), preferred_element_type=F32)


def _sigmoid(x):
    return 1.0 / (1.0 + jnp.exp(-x))


def _silu(x):
    return x * _sigmoid(x)


def _log_sigmoid(z):
    return jnp.minimum(z, 0.0) - jnp.log(1.0 + jnp.exp(-jnp.abs(z)))


def _rms_rows(x, g):
    ms = jnp.mean(x * x, axis=-1, keepdims=True)
    return x * lax.rsqrt(ms + EPS) * g


def _interleave(main, side):
    n_main, n_side = len(main), len(side)
    done = 0
    for i, thunk in enumerate(main):
        thunk()
        want = ((i + 1) * n_side) // n_main
        while done < want:
            side[done]()
            done += 1


def _gla_consts(forward):
    r = lax.broadcasted_iota(jnp.int32, (CHUNK, CHUNK), 0)
    c = lax.broadcasted_iota(jnp.int32, (CHUNK, CHUNK), 1)
    tri = (c <= r) if forward else (c >= r)
    tri_bf = jnp.where(tri, 1.0, 0.0).astype(BF16)
    rr = lax.broadcasted_iota(jnp.int32, (N_HEADS * CHUNK, LANES), 0) % CHUNK
    cc = lax.broadcasted_iota(jnp.int32, (N_HEADS * CHUNK, LANES), 1)
    keep = ((cc <= rr) if forward else (cc >= rr)) & (cc < CHUNK)
    lane = lax.broadcasted_iota(jnp.int32, (CHUNK, D_GLA_K), 1)
    head_masks = [(lane >= h * HEAD_K) & (lane < (h + 1) * HEAD_K) for h in range(N_HEADS)]
    return tri_bf, keep, head_masks


def _gla_cumsum(la, consts):
    tri_bf = consts[0]
    la_hi = la.astype(BF16)
    la_lo = (la - la_hi.astype(F32)).astype(BF16)
    return _dot(tri_bf, la_hi) + _dot(tri_bf, la_lo)


def _gla_prep(q, k, v_bf, b, consts, forward, with_output):
    _, keep, head_masks = consts
    b_last = b[CHUNK - 1:CHUNK, :] if forward else b[0:1, :]
    k_w = k * jnp.exp(b_last - b)
    e_last = jnp.exp(b_last)

    tb = jnp.concatenate(
        [k_w, jnp.broadcast_to(e_last, (8, D_GLA_K)), jnp.zeros((CHUNK - 8, D_GLA_K), F32)], axis=0)
    tbt = tb.T
    dec_col = tbt[:, CHUNK:CHUNK + 1]
    kwt_bf = tbt.astype(BF16)
    zeros_v = jnp.zeros((CHUNK, HEAD_V), BF16)
    v_pads = [jnp.concatenate([v_bf[:, h * HEAD_V:(h + 1) * HEAD_V], zeros_v], axis=0)
              for h in range(N_HEADS)]
    kv = jnp.concatenate([_dot(kwt_bf[h * CHUNK:(h + 1) * CHUNK], v_pads[h])
                          for h in range(N_HEADS)], axis=0)
    if not with_output:
        return dec_col, kv, None
    q_t = (q * jnp.exp(b) * (HEAD_K ** -0.5)).astype(BF16)
    k_t = (k * jnp.exp(-b)).astype(BF16)
    qm = jnp.concatenate([jnp.where(m, q_t, jnp.zeros_like(q_t)) for m in head_masks], axis=0)
    kt_pad = jnp.concatenate([k_t, jnp.zeros_like(k_t)], axis=0)
    attn = _dot_nt(qm, kt_pad)
    p_bf = jnp.where(keep, attn, 0.0).astype(BF16)
    return dec_col, kv, (qm, p_bf, v_pads)


def _gla_out(ops, s_enter):
    qm, p_bf, v_pads = ops
    o_inter = _dot(qm, s_enter.astype(BF16))
    outs = []
    for h in range(N_HEADS):
        rows = slice(h * CHUNK, (h + 1) * CHUNK)
        outs.append(_dot(p_bf[rows], v_pads[h]) + o_inter[rows])
    return jnp.concatenate(outs, axis=1)


def _log_decay(r, w_a2, b_a):
    z = _dot(r.astype(BF16), w_a2.astype(BF16)) + b_a
    return _log_sigmoid(z) * (1.0 / GLA_TAU)


def _chunk_rows(ci):
    return slice(ci * CHUNK, (ci + 1) * CHUNK)


def _mod_kernel(ct_ref, w_ref, b_ref, o_ref, s_ref):
    s = _silu(ct_ref[...])
    s_ref[...] = s
    w = w_ref[...]
    for j in range(2):
        o_ref[j:j + 1, :] = jnp.sum(w * s[:, j:j + 1], axis=0, keepdims=True) + b_ref[...]


def _mod_call(ct, w_mod, b_mod):
    tn = 512
    n = 2 * D_MODEL
    return pl.pallas_call(
        _mod_kernel,
        grid=(n // tn,),
        in_specs=[pl.BlockSpec((D_MODEL, 2), lambda j: (0, 0)),
                  pl.BlockSpec((D_MODEL, tn), lambda j: (0, j)),
                  pl.BlockSpec((1, tn), lambda j: (0, j))],
        out_specs=(pl.BlockSpec((2, tn), lambda j: (0, j)),
                   pl.BlockSpec((D_MODEL, 2), lambda j: (0, 0))),
        out_shape=(jax.ShapeDtypeStruct((2, n), F32), jax.ShapeDtypeStruct((D_MODEL, 2), F32)),
        compiler_params=pltpu.CompilerParams(dimension_semantics=("arbitrary",),
                                             vmem_limit_bytes=VMEM_LIMIT),
        name="adaln_mod",
    )(ct, w_mod, b_mod)


def _ctx_kernel(ctx_ref, mod_ref, n1g_ref, w_in_t_ref, wa2f_ref, baf_ref, wa2b_ref, bab_ref,
                sf_ref, sb_ref, w_in_ref):
    n_full = D_IN // MXU_N
    for j in range(n_full):
        cols = slice(j * MXU_N, (j + 1) * MXU_N)
        w_in_ref[:, cols] = w_in_t_ref[cols, :].T.astype(BF16)
    tail = w_in_t_ref[D_IN - LANES:D_IN, :].T
    w_in_ref[:, n_full * MXU_N:D_IN] = tail[:, LANES - (D_IN - n_full * MXU_N):].astype(BF16)
    n_chunks = ctx_ref.shape[0] // CHUNK
    sh = mod_ref[1:2, 0:D_MODEL]
    sc = mod_ref[1:2, D_MODEL:2 * D_MODEL]
    h = (_rms_rows(ctx_ref[...], n1g_ref[...]) * (1.0 + sc) + sh).astype(BF16)
    kv = _dot(h, w_in_ref[:, C_K:C_G])
    r = _dot(h, w_in_ref[:, C_R:D_IN])
    k = kv[:, :D_GLA_K]
    v_bf = kv[:, D_GLA_K:].astype(BF16)
    la_f = _log_decay(r[:, :GLA_RANK], wa2f_ref[...], baf_ref[...])
    la_b = _log_decay(r[:, GLA_RANK:], wa2b_ref[...], bab_ref[...])
    for forward, la, out_ref in ((True, la_f, sf_ref), (False, la_b, sb_ref)):
        consts = _gla_consts(forward)
        order = list(range(n_chunks)) if forward else list(range(n_chunks - 1, -1, -1))
        b = {ci: _gla_cumsum(la[_chunk_rows(ci)], consts) for ci in order}
        s = jnp.zeros((N_HEADS * HEAD_K, HEAD_V), F32)
        for ci in order:
            rows = _chunk_rows(ci)
            dec_col, kv_c, _ = _gla_prep(None, k[rows], v_bf[rows], b[ci], consts, forward, False)
            s = dec_col * s + kv_c
        out_ref[...] = s


def _ctx_call(ctx2d, mod, n1g, w_in_t, wa2f, baf, wa2b, bab):
    st = jax.ShapeDtypeStruct((N_HEADS * HEAD_K, HEAD_V), F32)
    return pl.pallas_call(
        _ctx_kernel,
        out_shape=(st, st, jax.ShapeDtypeStruct(w_in_t.shape[::-1], BF16)),
        compiler_params=pltpu.CompilerParams(vmem_limit_bytes=VMEM_LIMIT),
        name="context_states",
    )(ctx2d, mod, n1g, w_in_t, wa2f, baf, wa2b, bab)


def _pass1_rows(r0, state, x_ref, mod_ref, n1g_ref, w_in_refs, wa2f_ref, baf_ref, wa2b_ref, bab_ref,
                qk_ref, laf_ref, v_ref, g_ref, glu_ref, ob_ref, carry_w, carry_r):
    rows = slice(r0, r0 + SUB)
    w_in_ref, w_rank_t_ref = w_in_refs
    proj_blocks, gla_stages = [], []

    if carry_r is not None:
        qk_r, v_r, lab_r = carry_r
        first = r0 // CHUNK
        order = list(range(first + SUB // CHUNK - 1, first - 1, -1))
        consts = _gla_consts(False)
        b = {ci: _gla_cumsum(lab_r[_chunk_rows(ci), :], consts) for ci in order}

    if carry_w is not None:
        qk_w, v_w, lab_w = carry_w
        sh1 = mod_ref[0:1, 0:D_MODEL]
        sc1 = mod_ref[0:1, D_MODEL:2 * D_MODEL]
        h = (_rms_rows(x_ref[rows, :], n1g_ref[...]) * (1.0 + sc1) + sh1).astype(BF16)

        def proj(lo, hi):
            return _dot(h, w_in_ref[:, lo:hi])

        def glu_block(j):
            lo, hi = j * MXU_N, (j + 1) * MXU_N
            glu_ref[rows, lo:hi] = proj(lo, hi) * _sigmoid(proj(D_CONV + lo, D_CONV + hi))

        def qk_block(j):
            lo, hi = j * MXU_N, (j + 1) * MXU_N
            blk = proj(C_Q + lo, C_Q + hi)
            qk_ref[rows, lo:hi] = blk
            qk_w[rows, lo:hi] = blk

        def v_block(j):
            lo, hi = j * MXU_N, (j + 1) * MXU_N
            blk = proj(C_V + lo, C_V + hi).astype(BF16)
            v_ref[rows, lo:hi] = blk
            v_w[rows, lo:hi] = blk

        def g_block(j):
            lo, hi = j * MXU_N, (j + 1) * MXU_N
            g_ref[rows, lo:hi] = proj(C_G + lo, C_G + hi).astype(BF16)

        low_rank = {}

        def rank_block():
            r_t = _dot_nt(w_rank_t_ref[...].astype(BF16), h)
            r_t = jnp.concatenate([r_t, jnp.zeros((LANES - 2 * GLA_RANK, SUB), F32)], axis=0)
            low_rank["r"] = r_t.T

        def decay_block():
            r = low_rank.pop("r")
            laf_ref[rows, :] = _log_decay(r[:, :GLA_RANK], wa2f_ref[...], baf_ref[...])
            lab_w[rows, :] = _log_decay(r[:, GLA_RANK:2 * GLA_RANK], wa2b_ref[...], bab_ref[...])

        proj_blocks = ([rank_block]
                       + [functools.partial(glu_block, j) for j in range(D_CONV // MXU_N)]
                       + [decay_block]
                       + [functools.partial(qk_block, j) for j in range(2 * D_GLA_K // MXU_N)]
                       + [functools.partial(v_block, j) for j in range(D_GLA_V // MXU_N)]
                       + [functools.partial(g_block, j) for j in range(D_GLA_V // MXU_N)])

    if carry_r is not None:
        prep, s_enter = {}, {}

        def gla_prep(ci):
            crows = _chunk_rows(ci)
            qk_c = qk_r[crows, :]
            dec_col, kv, ops = _gla_prep(qk_c[:, :D_GLA_K], qk_c[:, D_GLA_K:], v_r[crows, :], b[ci],
                                         consts, False, True)
            prep[ci] = ops
            s_enter[ci] = state["s"]
            state["s"] = dec_col * state["s"] + kv

        def gla_out(ci):
            ob_ref[_chunk_rows(ci), :] = _gla_out(prep[ci], s_enter[ci]).astype(BF16)

        gla_stages = ([functools.partial(gla_prep, ci) for ci in order]
                      + [functools.partial(gla_out, ci) for ci in order])

    if proj_blocks and gla_stages:
        _interleave(proj_blocks, gla_stages)
    else:
        for thunk in proj_blocks + gla_stages:
            thunk()


def _pass1_step(x_ref, mod_ref, n1g_ref, w_in_ref, wa2f_ref, baf_ref, wa2b_ref, bab_ref,
                qk_ref, laf_ref, v_ref, g_ref, glu_ref, ob_ref, s_ref, carry_w, carry_r):
    state = {"s": s_ref[...]} if carry_r is not None else None
    for r0 in range(x_ref.shape[0] - SUB, -1, -SUB):
        _pass1_rows(r0, state, x_ref, mod_ref, n1g_ref, w_in_ref, wa2f_ref, baf_ref, wa2b_ref,
                    bab_ref, qk_ref, laf_ref, v_ref, g_ref, glu_ref, ob_ref, carry_w, carry_r)
    if carry_r is not None:
        s_ref[...] = state["s"]


def _pass1_kernel(n_steps, n_mod2,
                  x_ref, mod_ref, n1g_ref, w_in_ref, w_rank_t_ref, wa2f_ref, baf_ref, wa2b_ref, bab_ref,
                  sb0_ref, wo_f32_ref, wg_f32_ref, wu_f32_ref, wd_f32_ref, silu_ref, w_mod_ref, b_mod_ref,
                  qk_ref, laf_ref, v_ref, g_ref, glu_ref, ob_ref,
                  wo_bf_ref, wg_bf_ref, wu_bf_ref, wd_bf_ref, mod2_ref,
                  s_ref, qk_a, v_a, lab_a, qk_b, v_b, lab_b):
    step = pl.program_id(0)
    last = n_steps - 1

    @pl.when(step < n_mod2)
    def _():
        mod2_ref[...] = (jnp.sum(w_mod_ref[...] * silu_ref[:, 0:1], axis=0, keepdims=True)
                         + b_mod_ref[...])

    for src, dst in ((wo_f32_ref, wo_bf_ref), (wg_f32_ref, wg_bf_ref), (wu_f32_ref, wu_bf_ref),
                     (wd_f32_ref, wd_bf_ref)):
        dst[...] = src[...].astype(BF16)
    bufs = ((qk_a, v_a, lab_a), (qk_b, v_b, lab_b))

    run = functools.partial(_pass1_step, x_ref, mod_ref, n1g_ref, (w_in_ref, w_rank_t_ref), wa2f_ref, baf_ref,
                            wa2b_ref, bab_ref, qk_ref, laf_ref, v_ref, g_ref, glu_ref, ob_ref, s_ref)

    @pl.when(step == 0)
    def _():
        s_ref[...] = sb0_ref[...]
        run(bufs[0], None)

    for parity in (0, 1):
        @pl.when((step > 0) & (step < last) & (step % 2 == parity))
        def _():
            run(bufs[parity], bufs[1 - parity])

    @pl.when(step == last)
    def _():
        run(None, bufs[(last - 1) % 2])


def _slabs(w, n_slabs):
    rows, cols = w.shape
    assert rows % (n_slabs * 16) == 0
    return w.reshape(n_slabs, rows // n_slabs, cols)


def _pass1_call(x2d, mod, n1g, w_in_bf, w_in_t, wa2f, baf, wa2b, bab, sb0, w_out, w_gate, w_up, w_down,
                silu_ct, w_mod, b_mod):
    n_tok = x2d.shape[0]
    n_tiles = n_tok // TILE1
    mod2_lo = mod.shape[1] // D_MODEL
    mod2_blocks = w_mod.shape[1] // D_MODEL - mod2_lo
    assert mod2_blocks <= n_tiles
    mod2_in = lambda s: (0, mod2_lo + jnp.minimum(s, mod2_blocks - 1))
    mod2_out = lambda s: (0, jnp.minimum(s, mod2_blocks - 1))
    cast_in = [_slabs(w, n) for w, n in ((w_out, 32), (w_gate, 32), (w_up, 32), (w_down, 22))]
    assert all(w.shape[0] <= n_tiles for w in cast_in)
    cast_specs = [pl.BlockSpec((1,) + w.shape[1:],
                               functools.partial(lambda n, s: (jnp.minimum(s, n - 1), 0, 0), w.shape[0]))
                  for w in cast_in]
    proj = lambda s: (n_tiles - 1 - jnp.minimum(s, n_tiles - 1), 0)
    gla = lambda s: (n_tiles - 1 - jnp.maximum(s - 1, 0), 0)
    const = lambda s: (0, 0)
    full = lambda a: pl.BlockSpec(a.shape, const)
    out_shape = (
        jax.ShapeDtypeStruct((n_tok, 2 * D_GLA_K), F32),
        jax.ShapeDtypeStruct((n_tok, D_GLA_K), F32),
        jax.ShapeDtypeStruct((n_tok, D_GLA_V), BF16),
        jax.ShapeDtypeStruct((n_tok, D_GLA_V), BF16),
        jax.ShapeDtypeStruct((n_tok, D_CONV), F32),
        jax.ShapeDtypeStruct((n_tok, D_GLA_V), BF16),
    )
    carry = [pltpu.VMEM((TILE1, 2 * D_GLA_K), F32), pltpu.VMEM((TILE1, D_GLA_V), BF16),
             pltpu.VMEM((TILE1, D_GLA_K), F32)]
    return pl.pallas_call(
        functools.partial(_pass1_kernel, n_tiles + 1, mod2_blocks),
        grid=(n_tiles + 1,),
        in_specs=[pl.BlockSpec((TILE1, D_MODEL), proj), full(mod), full(n1g), full(w_in_bf),
                  pl.BlockSpec((2 * GLA_RANK, D_MODEL), lambda s: (C_R // (2 * GLA_RANK), 0)),
                  full(wa2f), full(baf), full(wa2b), full(bab), full(sb0)] + cast_specs
        + [full(silu_ct), pl.BlockSpec((D_MODEL, D_MODEL), mod2_in), pl.BlockSpec((1, D_MODEL), mod2_in)],
        out_specs=(pl.BlockSpec((TILE1, 2 * D_GLA_K), proj), pl.BlockSpec((TILE1, D_GLA_K), proj),
                   pl.BlockSpec((TILE1, D_GLA_V), proj), pl.BlockSpec((TILE1, D_GLA_V), proj),
                   pl.BlockSpec((TILE1, D_CONV), proj), pl.BlockSpec((TILE1, D_GLA_V), gla))
        + tuple(cast_specs) + (pl.BlockSpec((1, D_MODEL), mod2_out),),
        out_shape=out_shape + tuple(jax.ShapeDtypeStruct(w.shape, BF16) for w in cast_in)
        + (jax.ShapeDtypeStruct((1, mod2_blocks * D_MODEL), F32),),
        scratch_shapes=[pltpu.VMEM((N_HEADS * HEAD_K, HEAD_V), F32)] + carry + carry,
        compiler_params=pltpu.CompilerParams(dimension_semantics=("arbitrary",),
                                             vmem_limit_bytes=VMEM_LIMIT),
        name="pass1_project_gla_bwd",
    )(x2d, mod, n1g, w_in_bf, w_in_t, wa2f, baf, wa2b, bab, sb0, *cast_in, silu_ct, w_mod, b_mod)


def _conv_stage_in(glu_ref, cbuf_ref):
    n_seg = glu_ref.shape[0] // CHUNK
    for j in range(D_CONV // LANES):
        for s in range(n_seg):
            base = s * SEG_PITCH + SEG_LEAD
            cbuf_ref[j, base:base + CHUNK, :] = glu_ref[_chunk_rows(s), j * LANES:(j + 1) * LANES]


def _conv_taps(segs, j, cbuf_ref, cw_ref, zero):
    accs = [zero for _ in segs]
    for t in range(CONV_WIDTH):
        w = cw_ref[t:t + 1, j * LANES:(j + 1) * LANES]
        for i, s in enumerate(segs):
            start = s * SEG_PITCH + SEG_LEAD - CONV_PAD + t
            accs[i] = accs[i] + cbuf_ref[j, start:start + CHUNK, :] * w
    return accs


def _conv_norm(cols, cb_ref, lng_ref, lnb_ref):
    y = jnp.concatenate(cols, axis=1) + cb_ref[...]
    mu = jnp.mean(y, axis=-1, keepdims=True)
    yc = y - mu
    var = jnp.mean(yc * yc, axis=-1, keepdims=True)
    return _silu(yc * lax.rsqrt(var + EPS) * lng_ref[...] + lnb_ref[...])


def _pass2_step(do_ffn, do_mix, x_ref, qk_ref, laf_ref, v_ref, g_ref, glu_ref, ob_ref, mod_ref,
                n2g_ref, gng_ref, fg_ref, cw_ref, cb_ref, lng_ref, lnb_ref,
                w_out_ref, wg_ref, wu_ref, wd_ref, never_ref, out_ref, s_ref, mix_ref, cbuf_ref):
    n_seg = x_ref.shape[0] // CHUNK
    n_ff = D_FF // MXU_N
    side, ffn = [], []
    latest = {}

    if do_mix:
        chunks = list(range(n_seg))
        consts = _gla_consts(True)
        gng = gng_ref[...]
        _conv_stage_in(glu_ref, cbuf_ref)
        b = {ci: _gla_cumsum(laf_ref[_chunk_rows(ci), :], consts) for ci in chunks}

    if do_ffn:
        g1 = mod_ref[0:1, 0:D_MODEL]
        sh2 = mod_ref[0:1, D_MODEL:2 * D_MODEL]
        sc2 = mod_ref[0:1, 2 * D_MODEL:3 * D_MODEL]
        g2 = mod_ref[0:1, 3 * D_MODEL:4 * D_MODEL]
        x1 = x_ref[...] + g1 * _dot(mix_ref[...], w_out_ref[...])

    if do_mix:
        state = {"s": s_ref[...]}
        prep, s_enter = {}, {}

        def gla_prep(ci):
            rows = _chunk_rows(ci)
            qk_c = qk_ref[rows, :]
            dec_col, kv, ops = _gla_prep(qk_c[:, :D_GLA_K], qk_c[:, D_GLA_K:], v_ref[rows, :], b[ci],
                                         consts, True, True)
            prep[ci] = ops
            s_enter[ci] = state["s"]
            state["s"] = dec_col * state["s"] + kv

        def gla_out(ci):
            rows = _chunk_rows(ci)
            o = _gla_out(prep[ci], s_enter[ci]) + ob_ref[rows, :].astype(F32)
            gate = _silu(g_ref[rows, :].astype(F32))
            for h in range(N_HEADS):
                cols = slice(h * HEAD_V, (h + 1) * HEAD_V)
                mix_ref[rows, D_CONV + h * HEAD_V:D_CONV + (h + 1) * HEAD_V] = (
                    _rms_rows(o[:, cols], gng) * gate[:, cols]).astype(BF16)

        conv_cols = {}
        never = never_ref[...] != 0

        def conv_taps(segs, j):
            zero = (jnp.where(never, latest["ffn"], 0.0) if do_ffn
                    else jnp.zeros((CHUNK, LANES), F32))
            for s, acc in zip(segs, _conv_taps(segs, j, cbuf_ref, cw_ref, zero)):
                conv_cols[s, j] = acc

        def conv_out(s):
            cols = [conv_cols.pop((s, j)) for j in range(D_CONV // LANES)]
            mix_ref[_chunk_rows(s), 0:D_CONV] = _conv_norm(cols, cb_ref, lng_ref, lnb_ref).astype(BF16)

        side = [functools.partial(gla_prep, ci) for ci in chunks]
        for g0 in range(0, n_seg, CONV_GROUP):
            segs = chunks[g0:g0 + CONV_GROUP]
            side += [functools.partial(gla_out, ci) for ci in segs]
            side += [functools.partial(conv_taps, segs, j) for j in range(D_CONV // LANES)]
            side += [functools.partial(conv_out, ci) for ci in segs]

    n_early = min(2, len(side)) if do_ffn else len(side)
    for thunk in side[:n_early]:
        thunk()

    if do_ffn:
        h2 = (_rms_rows(x1, n2g_ref[...]) * (1.0 + sc2) + sh2).astype(BF16)
        act, acc = {}, {}

        def gate_up(j):
            cols = slice(j * MXU_N, (j + 1) * MXU_N)
            gate = _dot(h2, wg_ref[:, cols])
            latest["ffn"] = gate[0:CHUNK, 0:LANES]
            act[j] = (_silu(gate) * _dot(h2, wu_ref[:, cols])).astype(BF16)

        def down(j):
            part = _dot(act.pop(j), wd_ref[j * MXU_N:(j + 1) * MXU_N, :])
            acc["v"] = part if j == 0 else acc["v"] + part

        ffn = [functools.partial(gate_up, 0)]
        for j in range(1, n_ff):
            ffn += [functools.partial(gate_up, j), functools.partial(down, j - 1)]
        ffn.append(functools.partial(down, n_ff - 1))
        _interleave(ffn, side[n_early:])
    if do_mix:
        s_ref[...] = state["s"]
    if do_ffn:
        x2 = x1 + g2 * acc["v"]
        out_ref[...] = _rms_rows(x2, fg_ref[...])


def _pass2_kernel(n_steps,
                  x_ref, qk_ref, laf_ref, v_ref, g_ref, glu_ref, ob_ref, mod_ref, n2g_ref,
                  gng_ref, fg_ref, cw_ref, cb_ref, lng_ref, lnb_ref,
                  w_out_ref, wg_ref, wu_ref, wd_ref, sf0_ref, never_ref,
                  out_ref, s_ref, mix_ref, cbuf_ref):
    step = pl.program_id(0)
    last = n_steps - 1
    run = functools.partial(_pass2_step, x_ref=x_ref, qk_ref=qk_ref, laf_ref=laf_ref, v_ref=v_ref,
                            g_ref=g_ref, glu_ref=glu_ref, ob_ref=ob_ref, mod_ref=mod_ref,
                            n2g_ref=n2g_ref, gng_ref=gng_ref, fg_ref=fg_ref, cw_ref=cw_ref,
                            cb_ref=cb_ref, lng_ref=lng_ref, lnb_ref=lnb_ref, w_out_ref=w_out_ref,
                            wg_ref=wg_ref, wu_ref=wu_ref, wd_ref=wd_ref, never_ref=never_ref,
                            out_ref=out_ref, s_ref=s_ref, mix_ref=mix_ref, cbuf_ref=cbuf_ref)

    @pl.when(step == 0)
    def _():
        s_ref[...] = sf0_ref[...]
        cbuf_ref[...] = jnp.zeros(cbuf_ref.shape, F32)
        run(False, True)

    @pl.when((step > 0) & (step < last))
    def _():
        run(True, True)

    @pl.when(step == last)
    def _():
        run(True, False)


def _pass2_call(x2d, qk, laf, v, g, glu, ob, mod, n2g, gng, fg, cw, cb, lng, lnb,
                w_out_bf, wg_bf, wu_bf, wd_bf, sf0, never):
    n_tok = x2d.shape[0]
    n_tiles = n_tok // TILE
    mixer = lambda s: (jnp.minimum(s, n_tiles - 1), 0)
    ffn = lambda s: (jnp.maximum(s - 1, 0), 0)
    const = lambda s: (0, 0)
    full = lambda a: pl.BlockSpec(a.shape, const)
    mixer_spec = lambda w: pl.BlockSpec((TILE, w), mixer)
    return pl.pallas_call(
        functools.partial(_pass2_kernel, n_tiles + 1),
        grid=(n_tiles + 1,),
        in_specs=[pl.BlockSpec((TILE, D_MODEL), ffn), mixer_spec(2 * D_GLA_K), mixer_spec(D_GLA_K),
                  mixer_spec(D_GLA_V), mixer_spec(D_GLA_V), mixer_spec(D_CONV), mixer_spec(D_GLA_V),
                  full(mod), full(n2g), full(gng), full(fg), full(cw), full(cb), full(lng), full(lnb),
                  full(w_out_bf), full(wg_bf), full(wu_bf), full(wd_bf), full(sf0),
                  full(never)],
        out_specs=pl.BlockSpec((TILE, D_MODEL), ffn),
        out_shape=jax.ShapeDtypeStruct((n_tok, D_MODEL), F32),
        scratch_shapes=[pltpu.VMEM((N_HEADS * HEAD_K, HEAD_V), F32),
                        pltpu.VMEM((TILE, D_CONV + D_GLA_V), BF16),
                        pltpu.VMEM((D_CONV // LANES, (TILE // CHUNK) * SEG_PITCH, LANES), F32)],
        compiler_params=pltpu.CompilerParams(dimension_semantics=("arbitrary",),
                                             vmem_limit_bytes=VMEM_LIMIT),
        name="pass2_gla_fwd_conv_merge_ffn",
    )(x2d, qk, laf, v, g, glu, ob, mod, n2g, gng, fg, cw, cb, lng, lnb,
      w_out_bf, wg_bf, wu_bf, wd_bf, sf0, never)


def kernel(x, c, ctx, c_ctx, w_mod, b_mod, norm1_g, norm2_g, w_in, conv_w, conv_b, conv_ln_g,
           conv_ln_b, w_a2_f, b_a_f, w_a2_b, b_a_b, gla_norm_g, w_out, w_gate, w_up, w_down, final_g):
    bsz, n_lat, d = x.shape
    assert bsz == 1 and d == D_MODEL and n_lat % TILE == 0 and ctx.shape[1] % CHUNK == 0
    assert n_lat % TILE1 == 0 and TILE1 % SUB == 0
    assert (TILE // CHUNK) % CONV_GROUP == 0
    assert w_mod.shape[0] == 1, "single layer"
    row = lambda a: a.reshape(1, -1)

    ct = jnp.concatenate([c.reshape(D_MODEL, 1), c_ctx.reshape(D_MODEL, 1)], axis=1)
    mod, silu_ct = _mod_call(ct, w_mod[0], row(b_mod[0]))

    wa2f, baf = w_a2_f[0], row(b_a_f[0])
    wa2b, bab = w_a2_b[0], row(b_a_b[0])
    n1g = row(norm1_g[0])

    w_in_t = w_in[0].T
    sf0, sb0, w_in_bf = _ctx_call(ctx[0], mod, n1g, w_in_t, wa2f, baf, wa2b, bab)

    x2d = x[0]
    qk, laf, v, g, glu, ob, wo_bf, wg_bf, wu_bf, wd_bf, mod2 = _pass1_call(
        x2d, mod, n1g, w_in_bf, w_in_t, wa2f, baf, wa2b, bab, sb0, w_out[0], w_gate[0], w_up[0], w_down[0],
        silu_ct, w_mod[0], row(b_mod[0]))
    unslab = lambda w: w.reshape(-1, w.shape[-1])

    out = _pass2_call(
        x2d, qk, laf, v, g, glu, ob, mod2, row(norm2_g[0]), row(gla_norm_g[0]), row(final_g),
        conv_w[0], row(conv_b[0]), row(conv_ln_g[0]), row(conv_ln_b[0]),
        unslab(wo_bf), unslab(wg_bf), unslab(wu_bf), unslab(wd_bf), sf0,
        jnp.zeros((CHUNK, LANES), jnp.int32))
    return out.reshape(bsz, n_lat, d)
```

```python
import functools

import jax
import jax.numpy as jnp
from jax import lax
from jax.experimental import pallas as pl
from jax.experimental.pallas import tpu as pltpu

D_MODEL = 1024
D_CONV = 512
CONV_WIDTH = 31
CONV_PAD = 15
N_HEADS = 4
HEAD_K = 64
HEAD_V = 128
D_GLA_K = N_HEADS * HEAD_K
D_GLA_V = N_HEADS * HEAD_V
GLA_RANK = 16
GLA_TAU = 16.0
CHUNK = 64
EPS = 1e-6
D_FF = 2816

C_Q = 2 * D_CONV
C_K = C_Q + D_GLA_K
C_V = C_K + D_GLA_K
C_G = C_V + D_GLA_V
C_R = C_G + D_GLA_V
D_IN = C_R + 2 * GLA_RANK

LANES = 128
MXU_N = 256
SEG_PITCH = 96
SEG_LEAD = 16
CONV_GROUP = 2
SUB = 256
TILE1 = 512
CAST_SLABS = 8
TILE = 256
VMEM_LIMIT = 56 * 1024 * 1024

F32 = jnp.float32
BF16 = jnp.bfloat16


def _dot(a, b):
    return jnp.dot(a, b, preferred_element_type=F32)


def _dot_nt(a, b):
    return lax.dot_general(a, b, (((1,), (1,)), ((), ())), preferred_element_type=F32)


def _sigmoid(x):
    return 1.0 / (1.0 + jnp.exp(-x))


def _silu(x):
    return x * _sigmoid(x)


def _log_sigmoid(z):
    return jnp.minimum(z, 0.0) - jnp.log(1.0 + jnp.exp(-jnp.abs(z)))


def _rms_rows(x, g):
    ms = jnp.mean(x * x, axis=-1, keepdims=True)
    return x * lax.rsqrt(ms + EPS) * g


def _interleave(main, side):
    n_main, n_side = len(main), len(side)
    done = 0
    for i, thunk in enumerate(main):
        thunk()
        want = ((i + 1) * n_side) // n_main
        while done < want:
            side[done]()
            done += 1


def _gla_consts(forward):
    r = lax.broadcasted_iota(jnp.int32, (CHUNK, CHUNK), 0)
    c = lax.broadcasted_iota(jnp.int32, (CHUNK, CHUNK), 1)
    tri = (c <= r) if forward else (c >= r)
    tri_bf = jnp.where(tri, 1.0, 0.0).astype(BF16)
    rr = lax.broadcasted_iota(jnp.int32, (N_HEADS * CHUNK, LANES), 0) % CHUNK
    cc = lax.broadcasted_iota(jnp.int32, (N_HEADS * CHUNK, LANES), 1)
    keep = ((cc <= rr) if forward else (cc >= rr)) & (cc < CHUNK)
    lane = lax.broadcasted_iota(jnp.int32, (CHUNK, D_GLA_K), 1)
    head_masks = [(lane >= h * HEAD_K) & (lane < (h + 1) * HEAD_K) for h in range(N_HEADS)]
    return tri_bf, keep, head_masks


def _gla_cumsum(la, consts):
    tri_bf = consts[0]
    la_hi = la.astype(BF16)
    la_lo = (la - la_hi.astype(F32)).astype(BF16)
    return _dot(tri_bf, la_hi) + _dot(tri_bf, la_lo)


def _gla_prep(q, k, v_bf, b, consts, forward, with_output):
    _, keep, head_masks = consts
    b_last = b[CHUNK - 1:CHUNK, :] if forward else b[0:1, :]
    k_w = k * jnp.exp(b_last - b)
    e_last = jnp.exp(b_last)

    tb = jnp.concatenate(
        [k_w, jnp.broadcast_to(e_last, (8, D_GLA_K)), jnp.zeros((CHUNK - 8, D_GLA_K), F32)], axis=0)
    tbt = tb.T
    dec_col = tbt[:, CHUNK:CHUNK + 1]
    kwt_bf = tbt.astype(BF16)
    zeros_v = jnp.zeros((CHUNK, HEAD_V), BF16)
    v_pads = [jnp.concatenate([v_bf[:, h * HEAD_V:(h + 1) * HEAD_V], zeros_v], axis=0)
              for h in range(N_HEADS)]
    kv = jnp.concatenate([_dot(kwt_bf[h * CHUNK:(h + 1) * CHUNK], v_pads[h])
                          for h in range(N_HEADS)], axis=0)
    if not with_output:
        return dec_col, kv, None
    q_t = (q * jnp.exp(b) * (HEAD_K ** -0.5)).astype(BF16)
    k_t = (k * jnp.exp(-b)).astype(BF16)
    qm = jnp.concatenate([jnp.where(m, q_t, jnp.zeros_like(q_t)) for m in head_masks], axis=0)
    kt_pad = jnp.concatenate([k_t, jnp.zeros_like(k_t)], axis=0)
    attn = _dot_nt(qm, kt_pad)
    p_bf = jnp.where(keep, attn, 0.0).astype(BF16)
    return dec_col, kv, (qm, p_bf, v_pads)


def _gla_out(ops, s_enter):
    qm, p_bf, v_pads = ops
    o_inter = _dot(qm, s_enter.astype(BF16))
    outs = []
    for h in range(N_HEADS):
        rows = slice(h * CHUNK, (h + 1) * CHUNK)
        outs.append(_dot(p_bf[rows], v_pads[h]) + o_inter[rows])
    return jnp.concatenate(outs, axis=1)


def _log_decay(r, w_a2, b_a):
    z = _dot(r.astype(BF16), w_a2.astype(BF16)) + b_a
    return _log_sigmoid(z) * (1.0 / GLA_TAU)


def _chunk_rows(ci):
    return slice(ci * CHUNK, (ci + 1) * CHUNK)


def _mod_kernel(ct_ref, w_ref, b_ref, o_ref, s_ref):
    s = _silu(ct_ref[...])
    s_ref[...] = s
    w = w_ref[...]
    for j in range(2):
        o_ref[j:j + 1, :] = jnp.sum(w * s[:, j:j + 1], axis=0, keepdims=True) + b_ref[...]


def _mod_call(ct, w_mod, b_mod):
    tn = 512
    n = 2 * D_MODEL
    return pl.pallas_call(
        _mod_kernel,
        grid=(n // tn,),
        in_specs=[pl.BlockSpec((D_MODEL, 2), lambda j: (0, 0)),
                  pl.BlockSpec((D_MODEL, tn), lambda j: (0, j)),
                  pl.BlockSpec((1, tn), lambda j: (0, j))],
        out_specs=(pl.BlockSpec((2, tn), lambda j: (0, j)),
                   pl.BlockSpec((D_MODEL, 2), lambda j: (0, 0))),
        out_shape=(jax.ShapeDtypeStruct((2, n), F32), jax.ShapeDtypeStruct((D_MODEL, 2), F32)),
        compiler_params=pltpu.CompilerParams(dimension_semantics=("arbitrary",),
                                             vmem_limit_bytes=VMEM_LIMIT),
        name="adaln_mod",
    )(ct, w_mod, b_mod)


def _ctx_kernel(ctx_ref, mod_ref, n1g_ref, w_in_t_ref, wa2f_ref, baf_ref, wa2b_ref, bab_ref,
                sf_ref, sb_ref, w_in_ref):
    n_full = D_IN // MXU_N
    for j in range(n_full):
        cols = slice(j * MXU_N, (j + 1) * MXU_N)
        w_in_ref[:, cols] = w_in_t_ref[cols, :].T.astype(BF16)
    tail = w_in_t_ref[D_IN - LANES:D_IN, :].T
    w_in_ref[:, n_full * MXU_N:D_IN] = tail[:, LANES - (D_IN - n_full * MXU_N):].astype(BF16)
    n_chunks = ctx_ref.shape[0] // CHUNK
    sh = mod_ref[1:2, 0:D_MODEL]
    sc = mod_ref[1:2, D_MODEL:2 * D_MODEL]
    h = (_rms_rows(ctx_ref[...], n1g_ref[...]) * (1.0 + sc) + sh).astype(BF16)
    kv = _dot(h, w_in_ref[:, C_K:C_G])
    r = _dot(h, w_in_ref[:, C_R:D_IN])
    k = kv[:, :D_GLA_K]
    v_bf = kv[:, D_GLA_K:].astype(BF16)
    la_f = _log_decay(r[:, :GLA_RANK], wa2f_ref[...], baf_ref[...])
    la_b = _log_decay(r[:, GLA_RANK:], wa2b_ref[...], bab_ref[...])
    for forward, la, out_ref in ((True, la_f, sf_ref), (False, la_b, sb_ref)):
        consts = _gla_consts(forward)
        order = list(range(n_chunks)) if forward else list(range(n_chunks - 1, -1, -1))
        b = {ci: _gla_cumsum(la[_chunk_rows(ci)], consts) for ci in order}
        s = jnp.zeros((N_HEADS * HEAD_K, HEAD_V), F32)
        for ci in order:
            rows = _chunk_rows(ci)
            dec_col, kv_c, _ = _gla_prep(None, k[rows], v_bf[rows], b[ci], consts, forward, False)
            s = dec_col * s + kv_c
        out_ref[...] = s


def _ctx_call(ctx2d, mod, n1g, w_in_t, wa2f, baf, wa2b, bab):
    st = jax.ShapeDtypeStruct((N_HEADS * HEAD_K, HEAD_V), F32)
    return pl.pallas_call(
        _ctx_kernel,
        out_shape=(st, st, jax.ShapeDtypeStruct(w_in_t.shape[::-1], BF16)),
        compiler_params=pltpu.CompilerParams(vmem_limit_bytes=VMEM_LIMIT),
        name="context_states",
    )(ctx2d, mod, n1g, w_in_t, wa2f, baf, wa2b, bab)


def _pass1_rows(r0, state, x_ref, mod_ref, n1g_ref, w_in_ref, wa2f_ref, baf_ref, wa2b_ref, bab_ref,
                qk_ref, laf_ref, v_ref, g_ref, glu_ref, ob_ref, carry_w, carry_r):
    rows = slice(r0, r0 + SUB)
    proj_blocks, gla_stages = [], []

    if carry_r is not None:
        qk_r, v_r, lab_r = carry_r
        first = r0 // CHUNK
        order = list(range(first + SUB // CHUNK - 1, first - 1, -1))
        consts = _gla_consts(False)
        b = {ci: _gla_cumsum(lab_r[_chunk_rows(ci), :], consts) for ci in order}

    if carry_w is not None:
        qk_w, v_w, lab_w = carry_w
        sh1 = mod_ref[0:1, 0:D_MODEL]
        sc1 = mod_ref[0:1, D_MODEL:2 * D_MODEL]
        h = (_rms_rows(x_ref[rows, :], n1g_ref[...]) * (1.0 + sc1) + sh1).astype(BF16)

        def proj(lo, hi):
            return _dot(h, w_in_ref[:, lo:hi])

        def glu_block(j):
            lo, hi = j * MXU_N, (j + 1) * MXU_N
            glu_ref[rows, lo:hi] = proj(lo, hi) * _sigmoid(proj(D_CONV + lo, D_CONV + hi))

        def qk_block(j):
            lo, hi = j * MXU_N, (j + 1) * MXU_N
            blk = proj(C_Q + lo, C_Q + hi)
            qk_ref[rows, lo:hi] = blk
            qk_w[rows, lo:hi] = blk

        def v_block(j):
            lo, hi = j * MXU_N, (j + 1) * MXU_N
            blk = proj(C_V + lo, C_V + hi).astype(BF16)
            v_ref[rows, lo:hi] = blk
            v_w[rows, lo:hi] = blk

        def g_block(j):
            lo, hi = j * MXU_N, (j + 1) * MXU_N
            g_ref[rows, lo:hi] = proj(C_G + lo, C_G + hi).astype(BF16)

        low_rank = {}

        def rank_block():
            low_rank["r"] = proj(C_R, D_IN)

        def decay_block():
            r = low_rank.pop("r")
            laf_ref[rows, :] = _log_decay(r[:, :GLA_RANK], wa2f_ref[...], baf_ref[...])
            lab_w[rows, :] = _log_decay(r[:, GLA_RANK:], wa2b_ref[...], bab_ref[...])

        proj_blocks = ([rank_block]
                       + [functools.partial(glu_block, j) for j in range(D_CONV // MXU_N)]
                       + [decay_block]
                       + [functools.partial(qk_block, j) for j in range(2 * D_GLA_K // MXU_N)]
                       + [functools.partial(v_block, j) for j in range(D_GLA_V // MXU_N)]
                       + [functools.partial(g_block, j) for j in range(D_GLA_V // MXU_N)])

    if carry_r is not None:
        prep, s_enter = {}, {}

        def gla_prep(ci):
            crows = _chunk_rows(ci)
            qk_c = qk_r[crows, :]
            dec_col, kv, ops = _gla_prep(qk_c[:, :D_GLA_K], qk_c[:, D_GLA_K:], v_r[crows, :], b[ci],
                                         consts, False, True)
            prep[ci] = ops
            s_enter[ci] = state["s"]
            state["s"] = dec_col * state["s"] + kv

        def gla_out(ci):
            ob_ref[_chunk_rows(ci), :] = _gla_out(prep[ci], s_enter[ci]).astype(BF16)

        gla_stages = ([functools.partial(gla_prep, ci) for ci in order]
                      + [functools.partial(gla_out, ci) for ci in order])

    if proj_blocks and gla_stages:
        _interleave(proj_blocks, gla_stages)
    else:
        for thunk in proj_blocks + gla_stages:
            thunk()


def _pass1_step(x_ref, mod_ref, n1g_ref, w_in_ref, wa2f_ref, baf_ref, wa2b_ref, bab_ref,
                qk_ref, laf_ref, v_ref, g_ref, glu_ref, ob_ref, s_ref, carry_w, carry_r):
    state = {"s": s_ref[...]} if carry_r is not None else None
    for r0 in range(x_ref.shape[0] - SUB, -1, -SUB):
        _pass1_rows(r0, state, x_ref, mod_ref, n1g_ref, w_in_ref, wa2f_ref, baf_ref, wa2b_ref,
                    bab_ref, qk_ref, laf_ref, v_ref, g_ref, glu_ref, ob_ref, carry_w, carry_r)
    if carry_r is not None:
        s_ref[...] = state["s"]


def _pass1_kernel(n_steps, n_mod2, n_cast,
                  x_ref, mod_ref, n1g_ref, w_in_ref, wa2f_ref, baf_ref, wa2b_ref, bab_ref, sb0_ref,
                  wo_f32_ref, wg_f32_ref, wu_f32_ref, wd_f32_ref, silu_ref, w_mod_ref, b_mod_ref,
                  qk_ref, laf_ref, v_ref, g_ref, glu_ref, ob_ref,
                  wo_bf_ref, wg_bf_ref, wu_bf_ref, wd_bf_ref, mod2_ref,
                  s_ref, qk_a, v_a, lab_a, qk_b, v_b, lab_b):
    step = pl.program_id(0)
    last = n_steps - 1

    @pl.when(step < n_mod2)
    def _():
        mod2_ref[...] = (jnp.sum(w_mod_ref[...] * silu_ref[:, 0:1], axis=0, keepdims=True)
                         + b_mod_ref[...])

    @pl.when(step < n_cast)
    def _():
        for src, dst in ((wo_f32_ref, wo_bf_ref), (wg_f32_ref, wg_bf_ref), (wu_f32_ref, wu_bf_ref),
                         (wd_f32_ref, wd_bf_ref)):
            dst[...] = src[...].astype(BF16)
    bufs = ((qk_a, v_a, lab_a), (qk_b, v_b, lab_b))

    run = functools.partial(_pass1_step, x_ref, mod_ref, n1g_ref, w_in_ref, wa2f_ref, baf_ref,
                            wa2b_ref, bab_ref, qk_ref, laf_ref, v_ref, g_ref, glu_ref, ob_ref, s_ref)

    @pl.when(step == 0)
    def _():
        s_ref[...] = sb0_ref[...]
        run(bufs[0], None)

    for parity in (0, 1):
        @pl.when((step > 0) & (step < last) & (step % 2 == parity))
        def _():
            run(bufs[parity], bufs[1 - parity])

    @pl.when(step == last)
    def _():
        run(None, bufs[(last - 1) % 2])


def _slabs(w, n_slabs):
    rows, cols = w.shape
    assert rows % (n_slabs * 16) == 0
    return w.reshape(n_slabs, rows // n_slabs, cols)


def _pass1_call(x2d, mod, n1g, w_in_bf, wa2f, baf, wa2b, bab, sb0, w_out, w_gate, w_up, w_down,
                silu_ct, w_mod, b_mod):
    n_tok = x2d.shape[0]
    n_tiles = n_tok // TILE1
    mod2_lo = mod.shape[1] // D_MODEL
    mod2_blocks = w_mod.shape[1] // D_MODEL - mod2_lo
    assert mod2_blocks <= n_tiles
    mod2_in = lambda s: (0, mod2_lo + jnp.minimum(s, mod2_blocks - 1))
    mod2_out = lambda s: (0, jnp.minimum(s, mod2_blocks - 1))
    cast_in = [_slabs(w, CAST_SLABS) for w in (w_out, w_gate, w_up, w_down)]
    assert all(w.shape[0] <= n_tiles for w in cast_in)
    cast_specs = [pl.BlockSpec((1,) + w.shape[1:],
                               functools.partial(lambda n, s: (jnp.minimum(s, n - 1), 0, 0), w.shape[0]))
                  for w in cast_in]
    proj = lambda s: (n_tiles - 1 - jnp.minimum(s, n_tiles - 1), 0)
    gla = lambda s: (n_tiles - 1 - jnp.maximum(s - 1, 0), 0)
    const = lambda s: (0, 0)
    full = lambda a: pl.BlockSpec(a.shape, const)
    out_shape = (
        jax.ShapeDtypeStruct((n_tok, 2 * D_GLA_K), F32),
        jax.ShapeDtypeStruct((n_tok, D_GLA_K), F32),
        jax.ShapeDtypeStruct((n_tok, D_GLA_V), BF16),
        jax.ShapeDtypeStruct((n_tok, D_GLA_V), BF16),
        jax.ShapeDtypeStruct((n_tok, D_CONV), F32),
        jax.ShapeDtypeStruct((n_tok, D_GLA_V), BF16),
    )
    carry = [pltpu.VMEM((TILE1, 2 * D_GLA_K), F32), pltpu.VMEM((TILE1, D_GLA_V), BF16),
             pltpu.VMEM((TILE1, D_GLA_K), F32)]
    return pl.pallas_call(
        functools.partial(_pass1_kernel, n_tiles + 1, mod2_blocks, CAST_SLABS),
        grid=(n_tiles + 1,),
        in_specs=[pl.BlockSpec((TILE1, D_MODEL), proj), full(mod), full(n1g), full(w_in_bf),
                  full(wa2f), full(baf), full(wa2b), full(bab), full(sb0)] + cast_specs
        + [full(silu_ct), pl.BlockSpec((D_MODEL, D_MODEL), mod2_in), pl.BlockSpec((1, D_MODEL), mod2_in)],
        out_specs=(pl.BlockSpec((TILE1, 2 * D_GLA_K), proj), pl.BlockSpec((TILE1, D_GLA_K), proj),
                   pl.BlockSpec((TILE1, D_GLA_V), proj), pl.BlockSpec((TILE1, D_GLA_V), proj),
                   pl.BlockSpec((TILE1, D_CONV), proj), pl.BlockSpec((TILE1, D_GLA_V), gla))
        + tuple(cast_specs) + (pl.BlockSpec((1, D_MODEL), mod2_out),),
        out_shape=out_shape + tuple(jax.ShapeDtypeStruct(w.shape, BF16) for w in cast_in)
        + (jax.ShapeDtypeStruct((1, mod2_blocks * D_MODEL), F32),),
        scratch_shapes=[pltpu.VMEM((N_HEADS * HEAD_K, HEAD_V), F32)] + carry + carry,
        compiler_params=pltpu.CompilerParams(dimension_semantics=("arbitrary",),
                                             vmem_limit_bytes=VMEM_LIMIT),
        name="pass1_project_gla_bwd",
    )(x2d, mod, n1g, w_in_bf, wa2f, baf, wa2b, bab, sb0, *cast_in, silu_ct, w_mod, b_mod)


def _conv_stage_in(glu_ref, cbuf_ref):
    n_seg = glu_ref.shape[0] // CHUNK
    for j in range(D_CONV // LANES):
        for s in range(n_seg):
            base = s * SEG_PITCH + SEG_LEAD
            cbuf_ref[j, base:base + CHUNK, :] = glu_ref[_chunk_rows(s), j * LANES:(j + 1) * LANES]


def _conv_taps(segs, j, cbuf_ref, cw_ref, zero):
    accs = [zero for _ in segs]
    for t in range(CONV_WIDTH):
        w = cw_ref[t:t + 1, j * LANES:(j + 1) * LANES]
        for i, s in enumerate(segs):
            start = s * SEG_PITCH + SEG_LEAD - CONV_PAD + t
            accs[i] = accs[i] + cbuf_ref[j, start:start + CHUNK, :] * w
    return accs


def _conv_norm(cols, cb_ref, lng_ref, lnb_ref):
    y = jnp.concatenate(cols, axis=1) + cb_ref[...]
    mu = jnp.mean(y, axis=-1, keepdims=True)
    yc = y - mu
    var = jnp.mean(yc * yc, axis=-1, keepdims=True)
    return _silu(yc * lax.rsqrt(var + EPS) * lng_ref[...] + lnb_ref[...])


def _pass2_step(do_ffn, do_mix, x_ref, qk_ref, laf_ref, v_ref, g_ref, glu_ref, ob_ref, mod_ref,
                n2g_ref, gng_ref, fg_ref, cw_ref, cb_ref, lng_ref, lnb_ref,
                w_out_ref, wg_ref, wu_ref, wd_ref, never_ref, out_ref, s_ref, mix_ref, cbuf_ref):
    n_seg = x_ref.shape[0] // CHUNK
    n_ff = D_FF // MXU_N
    side, ffn = [], []
    latest = {}

    if do_mix:
        chunks = list(range(n_seg))
        consts = _gla_consts(True)
        gng = gng_ref[...]
        _conv_stage_in(glu_ref, cbuf_ref)
        b = {ci: _gla_cumsum(laf_ref[_chunk_rows(ci), :], consts) for ci in chunks}

    if do_ffn:
        g1 = mod_ref[0:1, 0:D_MODEL]
        sh2 = mod_ref[0:1, D_MODEL:2 * D_MODEL]
        sc2 = mod_ref[0:1, 2 * D_MODEL:3 * D_MODEL]
        g2 = mod_ref[0:1, 3 * D_MODEL:4 * D_MODEL]
        x1 = x_ref[...] + g1 * _dot(mix_ref[...], w_out_ref[...])

    if do_mix:
        state = {"s": s_ref[...]}
        prep, s_enter = {}, {}

        def gla_prep(ci):
            rows = _chunk_rows(ci)
            qk_c = qk_ref[rows, :]
            dec_col, kv, ops = _gla_prep(qk_c[:, :D_GLA_K], qk_c[:, D_GLA_K:], v_ref[rows, :], b[ci],
                                         consts, True, True)
            prep[ci] = ops
            s_enter[ci] = state["s"]
            state["s"] = dec_col * state["s"] + kv

        def gla_out(ci):
            rows = _chunk_rows(ci)
            o = _gla_out(prep[ci], s_enter[ci]) + ob_ref[rows, :].astype(F32)
            gate = _silu(g_ref[rows, :].astype(F32))
            for h in range(N_HEADS):
                cols = slice(h * HEAD_V, (h + 1) * HEAD_V)
                mix_ref[rows, D_CONV + h * HEAD_V:D_CONV + (h + 1) * HEAD_V] = (
                    _rms_rows(o[:, cols], gng) * gate[:, cols]).astype(BF16)

        conv_cols = {}
        never = never_ref[...] != 0

        def conv_taps(segs, j):
            zero = (jnp.where(never, latest["ffn"], 0.0) if do_ffn
                    else jnp.zeros((CHUNK, LANES), F32))
            for s, acc in zip(segs, _conv_taps(segs, j, cbuf_ref, cw_ref, zero)):
                conv_cols[s, j] = acc

        def conv_out(s):
            cols = [conv_cols.pop((s, j)) for j in range(D_CONV // LANES)]
            mix_ref[_chunk_rows(s), 0:D_CONV] = _conv_norm(cols, cb_ref, lng_ref, lnb_ref).astype(BF16)

        side = [functools.partial(gla_prep, ci) for ci in chunks]
        for g0 in range(0, n_seg, CONV_GROUP):
            segs = chunks[g0:g0 + CONV_GROUP]
            side += [functools.partial(gla_out, ci) for ci in segs]
            side += [functools.partial(conv_taps, segs, j) for j in range(D_CONV // LANES)]
            side += [functools.partial(conv_out, ci) for ci in segs]

    n_early = min(2, len(side)) if do_ffn else len(side)
    for thunk in side[:n_early]:
        thunk()

    if do_ffn:
        h2 = (_rms_rows(x1, n2g_ref[...]) * (1.0 + sc2) + sh2).astype(BF16)
        act, acc = {}, {}

        def gate_up(j):
            cols = slice(j * MXU_N, (j + 1) * MXU_N)
            gate = _dot(h2, wg_ref[:, cols])
            latest["ffn"] = gate[0:CHUNK, 0:LANES]
            act[j] = (_silu(gate) * _dot(h2, wu_ref[:, cols])).astype(BF16)

        def down(j):
            part = _dot(act.pop(j), wd_ref[j * MXU_N:(j + 1) * MXU_N, :])
            acc["v"] = part if j == 0 else acc["v"] + part

        ffn = [functools.partial(gate_up, 0)]
        for j in range(1, n_ff):
            ffn += [functools.partial(gate_up, j), functools.partial(down, j - 1)]
        ffn.append(functools.partial(down, n_ff - 1))
        _interleave(ffn, side[n_early:])
    if do_mix:
        s_ref[...] = state["s"]
    if do_ffn:
        x2 = x1 + g2 * acc["v"]
        out_ref[...] = _rms_rows(x2, fg_ref[...])


def _pass2_kernel(n_steps,
                  x_ref, qk_ref, laf_ref, v_ref, g_ref, glu_ref, ob_ref, mod_ref, n2g_ref,
                  gng_ref, fg_ref, cw_ref, cb_ref, lng_ref, lnb_ref,
                  w_out_ref, wg_ref, wu_ref, wd_ref, sf0_ref, never_ref,
                  out_ref, s_ref, mix_ref, cbuf_ref):
    step = pl.program_id(0)
    last = n_steps - 1
    run = functools.partial(_pass2_step, x_ref=x_ref, qk_ref=qk_ref, laf_ref=laf_ref, v_ref=v_ref,
                            g_ref=g_ref, glu_ref=glu_ref, ob_ref=ob_ref, mod_ref=mod_ref,
                            n2g_ref=n2g_ref, gng_ref=gng_ref, fg_ref=fg_ref, cw_ref=cw_ref,
                            cb_ref=cb_ref, lng_ref=lng_ref, lnb_ref=lnb_ref, w_out_ref=w_out_ref,
                            wg_ref=wg_ref, wu_ref=wu_ref, wd_ref=wd_ref, never_ref=never_ref,
                            out_ref=out_ref, s_ref=s_ref, mix_ref=mix_ref, cbuf_ref=cbuf_ref)

    @pl.when(step == 0)
    def _():
        s_ref[...] = sf0_ref[...]
        cbuf_ref[...] = jnp.zeros(cbuf_ref.shape, F32)
        run(False, True)

    @pl.when((step > 0) & (step < last))
    def _():
        run(True, True)

    @pl.when(step == last)
    def _():
        run(True, False)


def _pass2_call(x2d, qk, laf, v, g, glu, ob, mod, n2g, gng, fg, cw, cb, lng, lnb,
                w_out_bf, wg_bf, wu_bf, wd_bf, sf0, never):
    n_tok = x2d.shape[0]
    n_tiles = n_tok // TILE
    mixer = lambda s: (jnp.minimum(s, n_tiles - 1), 0)
    ffn = lambda s: (jnp.maximum(s - 1, 0), 0)
    const = lambda s: (0, 0)
    full = lambda a: pl.BlockSpec(a.shape, const)
    mixer_spec = lambda w: pl.BlockSpec((TILE, w), mixer)
    return pl.pallas_call(
        functools.partial(_pass2_kernel, n_tiles + 1),
        grid=(n_tiles + 1,),
        in_specs=[pl.BlockSpec((TILE, D_MODEL), ffn), mixer_spec(2 * D_GLA_K), mixer_spec(D_GLA_K),
                  mixer_spec(D_GLA_V), mixer_spec(D_GLA_V), mixer_spec(D_CONV), mixer_spec(D_GLA_V),
                  full(mod), full(n2g), full(gng), full(fg), full(cw), full(cb), full(lng), full(lnb),
                  full(w_out_bf), full(wg_bf), full(wu_bf), full(wd_bf), full(sf0),
                  full(never)],
        out_specs=pl.BlockSpec((TILE, D_MODEL), ffn),
        out_shape=jax.ShapeDtypeStruct((n_tok, D_MODEL), F32),
        scratch_shapes=[pltpu.VMEM((N_HEADS * HEAD_K, HEAD_V), F32),
                        pltpu.VMEM((TILE, D_CONV + D_GLA_V), BF16),
                        pltpu.VMEM((D_CONV // LANES, (TILE // CHUNK) * SEG_PITCH, LANES), F32)],
        compiler_params=pltpu.CompilerParams(dimension_semantics=("arbitrary",),
                                             vmem_limit_bytes=VMEM_LIMIT),
        name="pass2_gla_fwd_conv_merge_ffn",
    )(x2d, qk, laf, v, g, glu, ob, mod, n2g, gng, fg, cw, cb, lng, lnb,
      w_out_bf, wg_bf, wu_bf, wd_bf, sf0, never)


def kernel(x, c, ctx, c_ctx, w_mod, b_mod, norm1_g, norm2_g, w_in, conv_w, conv_b, conv_ln_g,
           conv_ln_b, w_a2_f, b_a_f, w_a2_b, b_a_b, gla_norm_g, w_out, w_gate, w_up, w_down, final_g):
    bsz, n_lat, d = x.shape
    assert bsz == 1 and d == D_MODEL and n_lat % TILE == 0 and ctx.shape[1] % CHUNK == 0
    assert n_lat % TILE1 == 0 and TILE1 % SUB == 0
    assert (TILE // CHUNK) % CONV_GROUP == 0
    assert w_mod.shape[0] == 1, "single layer"
    row = lambda a: a.reshape(1, -1)

    ct = jnp.concatenate([c.reshape(D_MODEL, 1), c_ctx.reshape(D_MODEL, 1)], axis=1)
    mod, silu_ct = _mod_call(ct, w_mod[0], row(b_mod[0]))

    wa2f, baf = w_a2_f[0], row(b_a_f[0])
    wa2b, bab = w_a2_b[0], row(b_a_b[0])
    n1g = row(norm1_g[0])

    sf0, sb0, w_in_bf = _ctx_call(ctx[0], mod, n1g, w_in[0].T, wa2f, baf, wa2b, bab)

    x2d = x[0]
    qk, laf, v, g, glu, ob, wo_bf, wg_bf, wu_bf, wd_bf, mod2 = _pass1_call(
        x2d, mod, n1g, w_in_bf, wa2f, baf, wa2b, bab, sb0, w_out[0], w_gate[0], w_up[0], w_down[0],
        silu_ct, w_mod[0], row(b_mod[0]))
    unslab = lambda w: w.reshape(-1, w.shape[-1])

    out = _pass2_call(
        x2d, qk, laf, v, g, glu, ob, mod2, row(norm2_g[0]), row(gla_norm_g[0]), row(final_g),
        conv_w[0], row(conv_b[0]), row(conv_ln_g[0]), row(conv_ln_b[0]),
        unslab(wo_bf), unslab(wg_bf), unslab(wu_bf), unslab(wd_bf), sf0,
        jnp.zeros((CHUNK, LANES), jnp.int32))
    return out.reshape(bsz, n_lat, d)
```

```python
import functools

import jax
import jax.numpy as jnp
from jax import lax
from jax.experimental import pallas as pl
from jax.experimental.pallas import tpu as pltpu

D_MODEL = 1024
D_CONV = 512
CONV_WIDTH = 31
CONV_PAD = 15
N_HEADS = 4
HEAD_K = 64
HEAD_V = 128
D_GLA_K = N_HEADS * HEAD_K
D_GLA_V = N_HEADS * HEAD_V
GLA_RANK = 16
GLA_TAU = 16.0
CHUNK = 64
EPS = 1e-6
D_FF = 2816

C_Q = 2 * D_CONV
C_K = C_Q + D_GLA_K
C_V = C_K + D_GLA_K
C_G = C_V + D_GLA_V
C_R = C_G + D_GLA_V
D_IN = C_R + 2 * GLA_RANK

LANES = 128
MXU_N = 256
SEG_PITCH = 96
SEG_LEAD = 16
CONV_GROUP = 2
SUB = 256
TILE1 = 512
TILE = 256
VMEM_LIMIT = 56 * 1024 * 1024

F32 = jnp.float32
BF16 = jnp.bfloat16


def _dot(a, b):
    return jnp.dot(a, b, preferred_element_type=F32)


def _dot_nt(a, b):
    return lax.dot_general(a, b, (((1,), (1,)), ((), ())), preferred_element_type=F32)


def _sigmoid(x):
    return 1.0 / (1.0 + jnp.exp(-x))


def _silu(x):
    return x * _sigmoid(x)


def _log_sigmoid(z):
    return jnp.minimum(z, 0.0) - jnp.log(1.0 + jnp.exp(-jnp.abs(z)))


def _rms_rows(x, g):
    ms = jnp.mean(x * x, axis=-1, keepdims=True)
    return x * lax.rsqrt(ms + EPS) * g


def _interleave(main, side):
    n_main, n_side = len(main), len(side)
    done = 0
    for i, thunk in enumerate(main):
        thunk()
        want = ((i + 1) * n_side) // n_main
        while done < want:
            side[done]()
            done += 1


def _gla_consts(forward):
    r = lax.broadcasted_iota(jnp.int32, (CHUNK, CHUNK), 0)
    c = lax.broadcasted_iota(jnp.int32, (CHUNK, CHUNK), 1)
    tri = (c <= r) if forward else (c >= r)
    tri_bf = jnp.where(tri, 1.0, 0.0).astype(BF16)
    rr = lax.broadcasted_iota(jnp.int32, (N_HEADS * CHUNK, LANES), 0) % CHUNK
    cc = lax.broadcasted_iota(jnp.int32, (N_HEADS * CHUNK, LANES), 1)
    keep = ((cc <= rr) if forward else (cc >= rr)) & (cc < CHUNK)
    lane = lax.broadcasted_iota(jnp.int32, (CHUNK, D_GLA_K), 1)
    head_masks = [(lane >= h * HEAD_K) & (lane < (h + 1) * HEAD_K) for h in range(N_HEADS)]
    return tri_bf, keep, head_masks


def _gla_cumsum(la, consts):
    tri_bf = consts[0]
    la_hi = la.astype(BF16)
    la_lo = (la - la_hi.astype(F32)).astype(BF16)
    return _dot(tri_bf, la_hi) + _dot(tri_bf, la_lo)


def _gla_prep(q, k, v_bf, b, consts, forward, with_output):
    _, keep, head_masks = consts
    b_last = b[CHUNK - 1:CHUNK, :] if forward else b[0:1, :]
    k_w = k * jnp.exp(b_last - b)
    e_last = jnp.exp(b_last)

    tb = jnp.concatenate(
        [k_w, jnp.broadcast_to(e_last, (8, D_GLA_K)), jnp.zeros((CHUNK - 8, D_GLA_K), F32)], axis=0)
    tbt = tb.T
    dec_col = tbt[:, CHUNK:CHUNK + 1]
    kwt_bf = tbt.astype(BF16)
    zeros_v = jnp.zeros((CHUNK, HEAD_V), BF16)
    v_pads = [jnp.concatenate([v_bf[:, h * HEAD_V:(h + 1) * HEAD_V], zeros_v], axis=0)
              for h in range(N_HEADS)]
    kv = jnp.concatenate([_dot(kwt_bf[h * CHUNK:(h + 1) * CHUNK], v_pads[h])
                          for h in range(N_HEADS)], axis=0)
    if not with_output:
        return dec_col, kv, None
    q_t = (q * jnp.exp(b) * (HEAD_K ** -0.5)).astype(BF16)
    k_t = (k * jnp.exp(-b)).astype(BF16)
    qm = jnp.concatenate([jnp.where(m, q_t, jnp.zeros_like(q_t)) for m in head_masks], axis=0)
    kt_pad = jnp.concatenate([k_t, jnp.zeros_like(k_t)], axis=0)
    attn = _dot_nt(qm, kt_pad)
    p_bf = jnp.where(keep, attn, 0.0).astype(BF16)
    return dec_col, kv, (qm, p_bf, v_pads)


def _gla_out(ops, s_enter):
    qm, p_bf, v_pads = ops
    o_inter = _dot(qm, s_enter.astype(BF16))
    outs = []
    for h in range(N_HEADS):
        rows = slice(h * CHUNK, (h + 1) * CHUNK)
        outs.append(_dot(p_bf[rows], v_pads[h]) + o_inter[rows])
    return jnp.concatenate(outs, axis=1)


def _log_decay(r, w_a2, b_a):
    z = _dot(r.astype(BF16), w_a2.astype(BF16)) + b_a
    return _log_sigmoid(z) * (1.0 / GLA_TAU)


def _chunk_rows(ci):
    return slice(ci * CHUNK, (ci + 1) * CHUNK)


def _mod_kernel(ct_ref, w_ref, b_ref, o_ref, s_ref):
    s = _silu(ct_ref[...])
    s_ref[...] = s
    w = w_ref[...]
    for j in range(2):
        o_ref[j:j + 1, :] = jnp.sum(w * s[:, j:j + 1], axis=0, keepdims=True) + b_ref[...]


def _mod_call(ct, w_mod, b_mod):
    tn = 512
    n = 2 * D_MODEL
    return pl.pallas_call(
        _mod_kernel,
        grid=(n // tn,),
        in_specs=[pl.BlockSpec((D_MODEL, 2), lambda j: (0, 0)),
                  pl.BlockSpec((D_MODEL, tn), lambda j: (0, j)),
                  pl.BlockSpec((1, tn), lambda j: (0, j))],
        out_specs=(pl.BlockSpec((2, tn), lambda j: (0, j)),
                   pl.BlockSpec((D_MODEL, 2), lambda j: (0, 0))),
        out_shape=(jax.ShapeDtypeStruct((2, n), F32), jax.ShapeDtypeStruct((D_MODEL, 2), F32)),
        compiler_params=pltpu.CompilerParams(dimension_semantics=("arbitrary",),
                                             vmem_limit_bytes=VMEM_LIMIT),
        name="adaln_mod",
    )(ct, w_mod, b_mod)


def _ctx_kernel(ctx_ref, mod_ref, n1g_ref, w_in_t_ref, wa2f_ref, baf_ref, wa2b_ref, bab_ref,
                sf_ref, sb_ref, w_in_ref):
    n_full = D_IN // MXU_N
    for j in range(n_full):
        cols = slice(j * MXU_N, (j + 1) * MXU_N)
        w_in_ref[:, cols] = w_in_t_ref[cols, :].T.astype(BF16)
    tail = w_in_t_ref[D_IN - LANES:D_IN, :].T
    w_in_ref[:, n_full * MXU_N:D_IN] = tail[:, LANES - (D_IN - n_full * MXU_N):].astype(BF16)
    n_chunks = ctx_ref.shape[0] // CHUNK
    sh = mod_ref[1:2, 0:D_MODEL]
    sc = mod_ref[1:2, D_MODEL:2 * D_MODEL]
    h = (_rms_rows(ctx_ref[...], n1g_ref[...]) * (1.0 + sc) + sh).astype(BF16)
    kv = _dot(h, w_in_ref[:, C_K:C_G])
    r = _dot(h, w_in_ref[:, C_R:D_IN])
    k = kv[:, :D_GLA_K]
    v_bf = kv[:, D_GLA_K:].astype(BF16)
    la_f = _log_decay(r[:, :GLA_RANK], wa2f_ref[...], baf_ref[...])
    la_b = _log_decay(r[:, GLA_RANK:], wa2b_ref[...], bab_ref[...])
    for forward, la, out_ref in ((True, la_f, sf_ref), (False, la_b, sb_ref)):
        consts = _gla_consts(forward)
        order = list(range(n_chunks)) if forward else list(range(n_chunks - 1, -1, -1))
        b = {ci: _gla_cumsum(la[_chunk_rows(ci)], consts) for ci in order}
        s = jnp.zeros((N_HEADS * HEAD_K, HEAD_V), F32)
        for ci in order:
            rows = _chunk_rows(ci)
            dec_col, kv_c, _ = _gla_prep(None, k[rows], v_bf[rows], b[ci], consts, forward, False)
            s = dec_col * s + kv_c
        out_ref[...] = s


def _ctx_call(ctx2d, mod, n1g, w_in_t, wa2f, baf, wa2b, bab):
    st = jax.ShapeDtypeStruct((N_HEADS * HEAD_K, HEAD_V), F32)
    return pl.pallas_call(
        _ctx_kernel,
        out_shape=(st, st, jax.ShapeDtypeStruct(w_in_t.shape[::-1], BF16)),
        compiler_params=pltpu.CompilerParams(vmem_limit_bytes=VMEM_LIMIT),
        name="context_states",
    )(ctx2d, mod, n1g, w_in_t, wa2f, baf, wa2b, bab)


def _pass1_rows(r0, state, x_ref, mod_ref, n1g_ref, w_in_ref, wa2f_ref, baf_ref, wa2b_ref, bab_ref,
                qk_ref, laf_ref, v_ref, g_ref, glu_ref, ob_ref, carry_w, carry_r):
    rows = slice(r0, r0 + SUB)
    proj_blocks, gla_stages = [], []

    if carry_r is not None:
        qk_r, v_r, lab_r = carry_r
        first = r0 // CHUNK
        order = list(range(first + SUB // CHUNK - 1, first - 1, -1))
        consts = _gla_consts(False)
        b = {ci: _gla_cumsum(lab_r[_chunk_rows(ci), :], consts) for ci in order}

    if carry_w is not None:
        qk_w, v_w, lab_w = carry_w
        sh1 = mod_ref[0:1, 0:D_MODEL]
        sc1 = mod_ref[0:1, D_MODEL:2 * D_MODEL]
        h = (_rms_rows(x_ref[rows, :], n1g_ref[...]) * (1.0 + sc1) + sh1).astype(BF16)

        def proj(lo, hi):
            return _dot(h, w_in_ref[:, lo:hi])

        def glu_block(j):
            lo, hi = j * MXU_N, (j + 1) * MXU_N
            glu_ref[rows, lo:hi] = proj(lo, hi) * _sigmoid(proj(D_CONV + lo, D_CONV + hi))

        def qk_block(j):
            lo, hi = j * MXU_N, (j + 1) * MXU_N
            blk = proj(C_Q + lo, C_Q + hi)
            qk_ref[rows, lo:hi] = blk
            qk_w[rows, lo:hi] = blk

        def v_block(j):
            lo, hi = j * MXU_N, (j + 1) * MXU_N
            blk = proj(C_V + lo, C_V + hi).astype(BF16)
            v_ref[rows, lo:hi] = blk
            v_w[rows, lo:hi] = blk

        def g_block(j):
            lo, hi = j * MXU_N, (j + 1) * MXU_N
            g_ref[rows, lo:hi] = proj(C_G + lo, C_G + hi).astype(BF16)

        low_rank = {}

        def rank_block():
            low_rank["r"] = proj(C_R, D_IN)

        def decay_block():
            r = low_rank.pop("r")
            laf_ref[rows, :] = _log_decay(r[:, :GLA_RANK], wa2f_ref[...], baf_ref[...])
            lab_w[rows, :] = _log_decay(r[:, GLA_RANK:], wa2b_ref[...], bab_ref[...])

        proj_blocks = ([rank_block]
                       + [functools.partial(glu_block, j) for j in range(D_CONV // MXU_N)]
                       + [decay_block]
                       + [functools.partial(qk_block, j) for j in range(2 * D_GLA_K // MXU_N)]
                       + [functools.partial(v_block, j) for j in range(D_GLA_V // MXU_N)]
                       + [functools.partial(g_block, j) for j in range(D_GLA_V // MXU_N)])

    if carry_r is not None:
        prep, s_enter = {}, {}

        def gla_prep(ci):
            crows = _chunk_rows(ci)
            qk_c = qk_r[crows, :]
            dec_col, kv, ops = _gla_prep(qk_c[:, :D_GLA_K], qk_c[:, D_GLA_K:], v_r[crows, :], b[ci],
                                         consts, False, True)
            prep[ci] = ops
            s_enter[ci] = state["s"]
            state["s"] = dec_col * state["s"] + kv

        def gla_out(ci):
            ob_ref[_chunk_rows(ci), :] = _gla_out(prep[ci], s_enter[ci]).astype(BF16)

        gla_stages = ([functools.partial(gla_prep, ci) for ci in order]
                      + [functools.partial(gla_out, ci) for ci in order])

    if proj_blocks and gla_stages:
        _interleave(proj_blocks, gla_stages)
    else:
        for thunk in proj_blocks + gla_stages:
            thunk()


def _pass1_step(x_ref, mod_ref, n1g_ref, w_in_ref, wa2f_ref, baf_ref, wa2b_ref, bab_ref,
                qk_ref, laf_ref, v_ref, g_ref, glu_ref, ob_ref, s_ref, carry_w, carry_r):
    state = {"s": s_ref[...]} if carry_r is not None else None
    for r0 in range(x_ref.shape[0] - SUB, -1, -SUB):
        _pass1_rows(r0, state, x_ref, mod_ref, n1g_ref, w_in_ref, wa2f_ref, baf_ref, wa2b_ref,
                    bab_ref, qk_ref, laf_ref, v_ref, g_ref, glu_ref, ob_ref, carry_w, carry_r)
    if carry_r is not None:
        s_ref[...] = state["s"]


def _pass1_kernel(n_steps, n_mod2,
                  x_ref, mod_ref, n1g_ref, w_in_ref, wa2f_ref, baf_ref, wa2b_ref, bab_ref, sb0_ref,
                  wo_f32_ref, wg_f32_ref, wu_f32_ref, wd_f32_ref, silu_ref, w_mod_ref, b_mod_ref,
                  qk_ref, laf_ref, v_ref, g_ref, glu_ref, ob_ref,
                  wo_bf_ref, wg_bf_ref, wu_bf_ref, wd_bf_ref, mod2_ref,
                  s_ref, qk_a, v_a, lab_a, qk_b, v_b, lab_b):
    step = pl.program_id(0)
    last = n_steps - 1

    @pl.when(step < n_mod2)
    def _():
        mod2_ref[...] = (jnp.sum(w_mod_ref[...] * silu_ref[:, 0:1], axis=0, keepdims=True)
                         + b_mod_ref[...])

    for src, dst in ((wo_f32_ref, wo_bf_ref), (wg_f32_ref, wg_bf_ref), (wu_f32_ref, wu_bf_ref),
                     (wd_f32_ref, wd_bf_ref)):
        dst[...] = src[...].astype(BF16)
    bufs = ((qk_a, v_a, lab_a), (qk_b, v_b, lab_b))

    run = functools.partial(_pass1_step, x_ref, mod_ref, n1g_ref, w_in_ref, wa2f_ref, baf_ref,
                            wa2b_ref, bab_ref, qk_ref, laf_ref, v_ref, g_ref, glu_ref, ob_ref, s_ref)

    @pl.when(step == 0)
    def _():
        s_ref[...] = sb0_ref[...]
        run(bufs[0], None)

    for parity in (0, 1):
        @pl.when((step > 0) & (step < last) & (step % 2 == parity))
        def _():
            run(bufs[parity], bufs[1 - parity])

    @pl.when(step == last)
    def _():
        run(None, bufs[(last - 1) % 2])


def _slabs(w, n_slabs):
    rows, cols = w.shape
    assert rows % (n_slabs * 16) == 0
    return w.reshape(n_slabs, rows // n_slabs, cols)


def _pass1_call(x2d, mod, n1g, w_in_bf, wa2f, baf, wa2b, bab, sb0, w_out, w_gate, w_up, w_down,
                silu_ct, w_mod, b_mod):
    n_tok = x2d.shape[0]
    n_tiles = n_tok // TILE1
    mod2_lo = mod.shape[1] // D_MODEL
    mod2_blocks = w_mod.shape[1] // D_MODEL - mod2_lo
    assert mod2_blocks <= n_tiles
    mod2_in = lambda s: (0, mod2_lo + jnp.minimum(s, mod2_blocks - 1))
    mod2_out = lambda s: (0, jnp.minimum(s, mod2_blocks - 1))
    cast_in = [_slabs(w, n) for w, n in ((w_out, 32), (w_gate, 32), (w_up, 32), (w_down, 22))]
    assert all(w.shape[0] <= n_tiles for w in cast_in)
    cast_specs = [pl.BlockSpec((1,) + w.shape[1:],
                               functools.partial(lambda n, s: (jnp.minimum(s, n - 1), 0, 0), w.shape[0]))
                  for w in cast_in]
    proj = lambda s: (n_tiles - 1 - jnp.minimum(s, n_tiles - 1), 0)
    gla = lambda s: (n_tiles - 1 - jnp.maximum(s - 1, 0), 0)
    const = lambda s: (0, 0)
    full = lambda a: pl.BlockSpec(a.shape, const)
    out_shape = (
        jax.ShapeDtypeStruct((n_tok, 2 * D_GLA_K), F32),
        jax.ShapeDtypeStruct((n_tok, D_GLA_K), F32),
        jax.ShapeDtypeStruct((n_tok, D_GLA_V), BF16),
        jax.ShapeDtypeStruct((n_tok, D_GLA_V), BF16),
        jax.ShapeDtypeStruct((n_tok, D_CONV), F32),
        jax.ShapeDtypeStruct((n_tok, D_GLA_V), BF16),
    )
    carry = [pltpu.VMEM((TILE1, 2 * D_GLA_K), F32), pltpu.VMEM((TILE1, D_GLA_V), BF16),
             pltpu.VMEM((TILE1, D_GLA_K), F32)]
    return pl.pallas_call(
        functools.partial(_pass1_kernel, n_tiles + 1, mod2_blocks),
        grid=(n_tiles + 1,),
        in_specs=[pl.BlockSpec((TILE1, D_MODEL), proj), full(mod), full(n1g), full(w_in_bf),
                  full(wa2f), full(baf), full(wa2b), full(bab), full(sb0)] + cast_specs
        + [full(silu_ct), pl.BlockSpec((D_MODEL, D_MODEL), mod2_in), pl.BlockSpec((1, D_MODEL), mod2_in)],
        out_specs=(pl.BlockSpec((TILE1, 2 * D_GLA_K), proj), pl.BlockSpec((TILE1, D_GLA_K), proj),
                   pl.BlockSpec((TILE1, D_GLA_V), proj), pl.BlockSpec((TILE1, D_GLA_V), proj),
                   pl.BlockSpec((TILE1, D_CONV), proj), pl.BlockSpec((TILE1, D_GLA_V), gla))
        + tuple(cast_specs) + (pl.BlockSpec((1, D_MODEL), mod2_out),),
        out_shape=out_shape + tuple(jax.ShapeDtypeStruct(w.shape, BF16) for w in cast_in)
        + (jax.ShapeDtypeStruct((1, mod2_blocks * D_MODEL), F32),),
        scratch_shapes=[pltpu.VMEM((N_HEADS * HEAD_K, HEAD_V), F32)] + carry + carry,
        compiler_params=pltpu.CompilerParams(dimension_semantics=("arbitrary",),
                                             vmem_limit_bytes=VMEM_LIMIT),
        name="pass1_project_gla_bwd",
    )(x2d, mod, n1g, w_in_bf, wa2f, baf, wa2b, bab, sb0, *cast_in, silu_ct, w_mod, b_mod)


def _conv_stage_in(glu_ref, cbuf_ref):
    n_seg = glu_ref.shape[0] // CHUNK
    for j in range(D_CONV // LANES):
        for s in range(n_seg):
            base = s * SEG_PITCH + SEG_LEAD
            cbuf_ref[j, base:base + CHUNK, :] = glu_ref[_chunk_rows(s), j * LANES:(j + 1) * LANES]


def _conv_taps(segs, j, cbuf_ref, cw_ref, zero):
    accs = [zero for _ in segs]
    for t in range(CONV_WIDTH):
        w = cw_ref[t:t + 1, j * LANES:(j + 1) * LANES]
        for i, s in enumerate(segs):
            start = s * SEG_PITCH + SEG_LEAD - CONV_PAD + t
            accs[i] = accs[i] + cbuf_ref[j, start:start + CHUNK, :] * w
    return accs


def _conv_norm(cols, cb_ref, lng_ref, lnb_ref):
    y = jnp.concatenate(cols, axis=1) + cb_ref[...]
    mu = jnp.mean(y, axis=-1, keepdims=True)
    yc = y - mu
    var = jnp.mean(yc * yc, axis=-1, keepdims=True)
    return _silu(yc * lax.rsqrt(var + EPS) * lng_ref[...] + lnb_ref[...])


def _pass2_step(do_ffn, do_mix, x_ref, qk_ref, laf_ref, v_ref, g_ref, glu_ref, ob_ref, mod_ref,
                n2g_ref, gng_ref, fg_ref, cw_ref, cb_ref, lng_ref, lnb_ref,
                w_out_ref, wg_ref, wu_ref, wd_ref, out_ref, s_ref, mix_ref, cbuf_ref):
    n_seg = x_ref.shape[0] // CHUNK
    n_ff = D_FF // MXU_N
    side, ffn = [], []
    latest = {}

    if do_mix:
        chunks = list(range(n_seg))
        consts = _gla_consts(True)
        gng = gng_ref[...]
        _conv_stage_in(glu_ref, cbuf_ref)
        b = {ci: _gla_cumsum(laf_ref[_chunk_rows(ci), :], consts) for ci in chunks}

    if do_ffn:
        g1 = mod_ref[0:1, 0:D_MODEL]
        sh2 = mod_ref[0:1, D_MODEL:2 * D_MODEL]
        sc2 = mod_ref[0:1, 2 * D_MODEL:3 * D_MODEL]
        g2 = mod_ref[0:1, 3 * D_MODEL:4 * D_MODEL]
        x1 = x_ref[...] + g1 * _dot(mix_ref[...], w_out_ref[...])

    if do_mix:
        state = {"s": s_ref[...]}
        prep, s_enter = {}, {}

        def gla_prep(ci):
            rows = _chunk_rows(ci)
            qk_c = qk_ref[rows, :]
            dec_col, kv, ops = _gla_prep(qk_c[:, :D_GLA_K], qk_c[:, D_GLA_K:], v_ref[rows, :], b[ci],
                                         consts, True, True)
            prep[ci] = ops
            s_enter[ci] = state["s"]
            state["s"] = dec_col * state["s"] + kv

        def gla_out(ci):
            rows = _chunk_rows(ci)
            o = _gla_out(prep[ci], s_enter[ci]) + ob_ref[rows, :].astype(F32)
            gate = _silu(g_ref[rows, :].astype(F32))
            for h in range(N_HEADS):
                cols = slice(h * HEAD_V, (h + 1) * HEAD_V)
                mix_ref[rows, D_CONV + h * HEAD_V:D_CONV + (h + 1) * HEAD_V] = (
                    _rms_rows(o[:, cols], gng) * gate[:, cols]).astype(BF16)

        conv_cols = {}
        never = laf_ref[0:CHUNK, 0:LANES] > 0.0

        def conv_taps(segs, j):
            zero = (jnp.where(never, latest["ffn"], 0.0) if do_ffn
                    else jnp.zeros((CHUNK, LANES), F32))
            for s, acc in zip(segs, _conv_taps(segs, j, cbuf_ref, cw_ref, zero)):
                conv_cols[s, j] = acc

        def conv_out(s):
            cols = [conv_cols.pop((s, j)) for j in range(D_CONV // LANES)]
            mix_ref[_chunk_rows(s), 0:D_CONV] = _conv_norm(cols, cb_ref, lng_ref, lnb_ref).astype(BF16)

        side = [functools.partial(gla_prep, ci) for ci in chunks]
        for g0 in range(0, n_seg, CONV_GROUP):
            segs = chunks[g0:g0 + CONV_GROUP]
            side += [functools.partial(gla_out, ci) for ci in segs]
            side += [functools.partial(conv_taps, segs, j) for j in range(D_CONV // LANES)]
            side += [functools.partial(conv_out, ci) for ci in segs]

    n_early = min(2, len(side)) if do_ffn else len(side)
    for thunk in side[:n_early]:
        thunk()

    if do_ffn:
        h2 = (_rms_rows(x1, n2g_ref[...]) * (1.0 + sc2) + sh2).astype(BF16)
        act, acc = {}, {}

        def gate_up(j):
            cols = slice(j * MXU_N, (j + 1) * MXU_N)
            gate = _dot(h2, wg_ref[:, cols])
            latest["ffn"] = gate[0:CHUNK, 0:LANES]
            act[j] = (_silu(gate) * _dot(h2, wu_ref[:, cols])).astype(BF16)

        def down(j):
            part = _dot(act.pop(j), wd_ref[j * MXU_N:(j + 1) * MXU_N, :])
            acc["v"] = part if j == 0 else acc["v"] + part

        ffn = [functools.partial(gate_up, 0)]
        for j in range(1, n_ff):
            ffn += [functools.partial(gate_up, j), functools.partial(down, j - 1)]
        ffn.append(functools.partial(down, n_ff - 1))
        _interleave(ffn, side[n_early:])
    if do_mix:
        s_ref[...] = state["s"]
    if do_ffn:
        x2 = x1 + g2 * acc["v"]
        out_ref[...] = _rms_rows(x2, fg_ref[...])


def _pass2_kernel(n_steps,
                  x_ref, qk_ref, laf_ref, v_ref, g_ref, glu_ref, ob_ref, mod_ref, n2g_ref,
                  gng_ref, fg_ref, cw_ref, cb_ref, lng_ref, lnb_ref,
                  w_out_ref, wg_ref, wu_ref, wd_ref, sf0_ref,
                  out_ref, s_ref, mix_ref, cbuf_ref):
    step = pl.program_id(0)
    last = n_steps - 1
    run = functools.partial(_pass2_step, x_ref=x_ref, qk_ref=qk_ref, laf_ref=laf_ref, v_ref=v_ref,
                            g_ref=g_ref, glu_ref=glu_ref, ob_ref=ob_ref, mod_ref=mod_ref,
                            n2g_ref=n2g_ref, gng_ref=gng_ref, fg_ref=fg_ref, cw_ref=cw_ref,
                            cb_ref=cb_ref, lng_ref=lng_ref, lnb_ref=lnb_ref, w_out_ref=w_out_ref,
                            wg_ref=wg_ref, wu_ref=wu_ref, wd_ref=wd_ref,
                            out_ref=out_ref, s_ref=s_ref, mix_ref=mix_ref, cbuf_ref=cbuf_ref)

    @pl.when(step == 0)
    def _():
        s_ref[...] = sf0_ref[...]
        cbuf_ref[...] = jnp.zeros(cbuf_ref.shape, F32)
        run(False, True)

    @pl.when((step > 0) & (step < last))
    def _():
        run(True, True)

    @pl.when(step == last)
    def _():
        run(True, False)


def _pass2_call(x2d, qk, laf, v, g, glu, ob, mod, n2g, gng, fg, cw, cb, lng, lnb,
                w_out_bf, wg_bf, wu_bf, wd_bf, sf0):
    n_tok = x2d.shape[0]
    n_tiles = n_tok // TILE
    mixer = lambda s: (jnp.minimum(s, n_tiles - 1), 0)
    ffn = lambda s: (jnp.maximum(s - 1, 0), 0)
    const = lambda s: (0, 0)
    full = lambda a: pl.BlockSpec(a.shape, const)
    mixer_spec = lambda w: pl.BlockSpec((TILE, w), mixer)
    return pl.pallas_call(
        functools.partial(_pass2_kernel, n_tiles + 1),
        grid=(n_tiles + 1,),
        in_specs=[pl.BlockSpec((TILE, D_MODEL), ffn), mixer_spec(2 * D_GLA_K), mixer_spec(D_GLA_K),
                  mixer_spec(D_GLA_V), mixer_spec(D_GLA_V), mixer_spec(D_CONV), mixer_spec(D_GLA_V),
                  full(mod), full(n2g), full(gng), full(fg), full(cw), full(cb), full(lng), full(lnb),
                  full(w_out_bf), full(wg_bf), full(wu_bf), full(wd_bf), full(sf0)],
        out_specs=pl.BlockSpec((TILE, D_MODEL), ffn),
        out_shape=jax.ShapeDtypeStruct((n_tok, D_MODEL), F32),
        scratch_shapes=[pltpu.VMEM((N_HEADS * HEAD_K, HEAD_V), F32),
                        pltpu.VMEM((TILE, D_CONV + D_GLA_V), BF16),
                        pltpu.VMEM((D_CONV // LANES, (TILE // CHUNK) * SEG_PITCH, LANES), F32)],
        compiler_params=pltpu.CompilerParams(dimension_semantics=("arbitrary",),
                                             vmem_limit_bytes=VMEM_LIMIT),
        name="pass2_gla_fwd_conv_merge_ffn",
    )(x2d, qk, laf, v, g, glu, ob, mod, n2g, gng, fg, cw, cb, lng, lnb,
      w_out_bf, wg_bf, wu_bf, wd_bf, sf0)


def kernel(x, c, ctx, c_ctx, w_mod, b_mod, norm1_g, norm2_g, w_in, conv_w, conv_b, conv_ln_g,
           conv_ln_b, w_a2_f, b_a_f, w_a2_b, b_a_b, gla_norm_g, w_out, w_gate, w_up, w_down, final_g):
    bsz, n_lat, d = x.shape
    assert bsz == 1 and d == D_MODEL and n_lat % TILE == 0 and ctx.shape[1] % CHUNK == 0
    assert n_lat % TILE1 == 0 and TILE1 % SUB == 0
    assert (TILE // CHUNK) % CONV_GROUP == 0
    assert w_mod.shape[0] == 1, "single layer"
    row = lambda a: a.reshape(1, -1)

    ct = jnp.concatenate([c.reshape(D_MODEL, 1), c_ctx.reshape(D_MODEL, 1)], axis=1)
    mod, silu_ct = _mod_call(ct, w_mod[0], row(b_mod[0]))

    wa2f, baf = w_a2_f[0], row(b_a_f[0])
    wa2b, bab = w_a2_b[0], row(b_a_b[0])
    n1g = row(norm1_g[0])

    sf0, sb0, w_in_bf = _ctx_call(ctx[0], mod, n1g, w_in[0].T, wa2f, baf, wa2b, bab)

    x2d = x[0]
    qk, laf, v, g, glu, ob, wo_bf, wg_bf, wu_bf, wd_bf, mod2 = _pass1_call(
        x2d, mod, n1g, w_in_bf, wa2f, baf, wa2b, bab, sb0, w_out[0], w_gate[0], w_up[0], w_down[0],
        silu_ct, w_mod[0], row(b_mod[0]))
    unslab = lambda w: w.reshape(-1, w.shape[-1])

    out = _pass2_call(
        x2d, qk, laf, v, g, glu, ob, mod2, row(norm2_g[0]), row(gla_norm_g[0]), row(final_g),
        conv_w[0], row(conv_b[0]), row(conv_ln_g[0]), row(conv_ln_b[0]),
        unslab(wo_bf), unslab(wg_bf), unslab(wu_bf), unslab(wd_bf), sf0)
    return out.reshape(bsz, n_lat, d)
```

```python
import functools

import jax
import jax.numpy as jnp
from jax import lax
from jax.experimental import pallas as pl
from jax.experimental.pallas import tpu as pltpu

D_MODEL = 1024
D_CONV = 512
CONV_WIDTH = 31
CONV_PAD = 15
N_HEADS = 4
HEAD_K = 64
HEAD_V = 128
D_GLA_K = N_HEADS * HEAD_K
D_GLA_V = N_HEADS * HEAD_V
GLA_RANK = 16
GLA_TAU = 16.0
CHUNK = 64
EPS = 1e-6
D_FF = 2816

C_Q = 2 * D_CONV
C_K = C_Q + D_GLA_K
C_V = C_K + D_GLA_K
C_G = C_V + D_GLA_V
C_R = C_G + D_GLA_V
D_IN = C_R + 2 * GLA_RANK

LANES = 128
MXU_N = 256
SEG_PITCH = 96
SEG_LEAD = 16
CONV_GROUP = 2
SUB = 256
TILE1 = 512
TILE = 256
VMEM_LIMIT = 56 * 1024 * 1024

F32 = jnp.float32
BF16 = jnp.bfloat16


def _dot(a, b):
    return jnp.dot(a, b, preferred_element_type=F32)


def _dot_nt(a, b):
    return lax.dot_general(a, b, (((1,), (1,)), ((), ())), preferred_element_type=F32)


def _sigmoid(x):
    return 1.0 / (1.0 + jnp.exp(-x))


def _silu(x):
    return x * _sigmoid(x)


def _log_sigmoid(z):
    return jnp.minimum(z, 0.0) - jnp.log(1.0 + jnp.exp(-jnp.abs(z)))


def _rms_rows(x, g):
    ms = jnp.mean(x * x, axis=-1, keepdims=True)
    return x * lax.rsqrt(ms + EPS) * g


def _interleave(main, side):
    n_main, n_side = len(main), len(side)
    done = 0
    for i, thunk in enumerate(main):
        thunk()
        want = ((i + 1) * n_side) // n_main
        while done < want:
            side[done]()
            done += 1


def _gla_consts(forward):
    r = lax.broadcasted_iota(jnp.int32, (CHUNK, CHUNK), 0)
    c = lax.broadcasted_iota(jnp.int32, (CHUNK, CHUNK), 1)
    tri = (c <= r) if forward else (c >= r)
    tri_bf = jnp.where(tri, 1.0, 0.0).astype(BF16)
    rr = lax.broadcasted_iota(jnp.int32, (N_HEADS * CHUNK, LANES), 0) % CHUNK
    cc = lax.broadcasted_iota(jnp.int32, (N_HEADS * CHUNK, LANES), 1)
    keep = ((cc <= rr) if forward else (cc >= rr)) & (cc < CHUNK)
    lane = lax.broadcasted_iota(jnp.int32, (CHUNK, D_GLA_K), 1)
    head_masks = [(lane >= h * HEAD_K) & (lane < (h + 1) * HEAD_K) for h in range(N_HEADS)]
    return tri_bf, keep, head_masks


def _gla_cumsum(la, consts):
    tri_bf = consts[0]
    la_hi = la.astype(BF16)
    la_lo = (la - la_hi.astype(F32)).astype(BF16)
    return _dot(tri_bf, la_hi) + _dot(tri_bf, la_lo)


def _gla_prep(q, k, v_bf, b, consts, forward, with_output):
    _, keep, head_masks = consts
    b_last = b[CHUNK - 1:CHUNK, :] if forward else b[0:1, :]
    k_w = k * jnp.exp(b_last - b)
    e_last = jnp.exp(b_last)

    tb = jnp.concatenate(
        [k_w, jnp.broadcast_to(e_last, (8, D_GLA_K)), jnp.zeros((CHUNK - 8, D_GLA_K), F32)], axis=0)
    tbt = tb.T
    dec_col = tbt[:, CHUNK:CHUNK + 1]
    kwt_bf = tbt.astype(BF16)
    zeros_v = jnp.zeros((CHUNK, HEAD_V), BF16)
    v_pads = [jnp.concatenate([v_bf[:, h * HEAD_V:(h + 1) * HEAD_V], zeros_v], axis=0)
              for h in range(N_HEADS)]
    kv = jnp.concatenate([_dot(kwt_bf[h * CHUNK:(h + 1) * CHUNK], v_pads[h])
                          for h in range(N_HEADS)], axis=0)
    if not with_output:
        return dec_col, kv, None
    q_t = (q * jnp.exp(b) * (HEAD_K ** -0.5)).astype(BF16)
    k_t = (k * jnp.exp(-b)).astype(BF16)
    qm = jnp.concatenate([jnp.where(m, q_t, jnp.zeros_like(q_t)) for m in head_masks], axis=0)
    kt_pad = jnp.concatenate([k_t, jnp.zeros_like(k_t)], axis=0)
    attn = _dot_nt(qm, kt_pad)
    p_bf = jnp.where(keep, attn, 0.0).astype(BF16)
    return dec_col, kv, (qm, p_bf, v_pads)


def _gla_out(ops, s_enter):
    qm, p_bf, v_pads = ops
    o_inter = _dot(qm, s_enter.astype(BF16))
    outs = []
    for h in range(N_HEADS):
        rows = slice(h * CHUNK, (h + 1) * CHUNK)
        outs.append(_dot(p_bf[rows], v_pads[h]) + o_inter[rows])
    return jnp.concatenate(outs, axis=1)


def _log_decay(r, w_a2, b_a):
    z = _dot(r.astype(BF16), w_a2.astype(BF16)) + b_a
    return _log_sigmoid(z) * (1.0 / GLA_TAU)


def _chunk_rows(ci):
    return slice(ci * CHUNK, (ci + 1) * CHUNK)


def _mod_kernel(ct_ref, w_ref, b_ref, o_ref, s_ref):
    s = _silu(ct_ref[...])
    s_ref[...] = s
    w = w_ref[...]
    for j in range(2):
        o_ref[j:j + 1, :] = jnp.sum(w * s[:, j:j + 1], axis=0, keepdims=True) + b_ref[...]


def _mod_call(ct, w_mod, b_mod):
    tn = 512
    n = 2 * D_MODEL
    return pl.pallas_call(
        _mod_kernel,
        grid=(n // tn,),
        in_specs=[pl.BlockSpec((D_MODEL, 2), lambda j: (0, 0)),
                  pl.BlockSpec((D_MODEL, tn), lambda j: (0, j)),
                  pl.BlockSpec((1, tn), lambda j: (0, j))],
        out_specs=(pl.BlockSpec((2, tn), lambda j: (0, j)),
                   pl.BlockSpec((D_MODEL, 2), lambda j: (0, 0))),
        out_shape=(jax.ShapeDtypeStruct((2, n), F32), jax.ShapeDtypeStruct((D_MODEL, 2), F32)),
        compiler_params=pltpu.CompilerParams(dimension_semantics=("arbitrary",),
                                             vmem_limit_bytes=VMEM_LIMIT),
        name="adaln_mod",
    )(ct, w_mod, b_mod)


def _ctx_kernel(ctx_ref, mod_ref, n1g_ref, w_in_t_ref, wa2f_ref, baf_ref, wa2b_ref, bab_ref,
                sf_ref, sb_ref, w_in_ref):
    n_full = D_IN // MXU_N
    for j in range(n_full):
        cols = slice(j * MXU_N, (j + 1) * MXU_N)
        w_in_ref[:, cols] = w_in_t_ref[cols, :].T.astype(BF16)
    tail = w_in_t_ref[D_IN - LANES:D_IN, :].T
    w_in_ref[:, n_full * MXU_N:D_IN] = tail[:, LANES - (D_IN - n_full * MXU_N):].astype(BF16)
    n_chunks = ctx_ref.shape[0] // CHUNK
    sh = mod_ref[1:2, 0:D_MODEL]
    sc = mod_ref[1:2, D_MODEL:2 * D_MODEL]
    h = (_rms_rows(ctx_ref[...], n1g_ref[...]) * (1.0 + sc) + sh).astype(BF16)
    kv = _dot(h, w_in_ref[:, C_K:C_G])
    r = _dot(h, w_in_ref[:, C_R:D_IN])
    k = kv[:, :D_GLA_K]
    v_bf = kv[:, D_GLA_K:].astype(BF16)
    la_f = _log_decay(r[:, :GLA_RANK], wa2f_ref[...], baf_ref[...])
    la_b = _log_decay(r[:, GLA_RANK:], wa2b_ref[...], bab_ref[...])
    for forward, la, out_ref in ((True, la_f, sf_ref), (False, la_b, sb_ref)):
        consts = _gla_consts(forward)
        order = list(range(n_chunks)) if forward else list(range(n_chunks - 1, -1, -1))
        b = {ci: _gla_cumsum(la[_chunk_rows(ci)], consts) for ci in order}
        s = jnp.zeros((N_HEADS * HEAD_K, HEAD_V), F32)
        for ci in order:
            rows = _chunk_rows(ci)
            dec_col, kv_c, _ = _gla_prep(None, k[rows], v_bf[rows], b[ci], consts, forward, False)
            s = dec_col * s + kv_c
        out_ref[...] = s


def _ctx_call(ctx2d, mod, n1g, w_in_t, wa2f, baf, wa2b, bab):
    st = jax.ShapeDtypeStruct((N_HEADS * HEAD_K, HEAD_V), F32)
    return pl.pallas_call(
        _ctx_kernel,
        out_shape=(st, st, jax.ShapeDtypeStruct(w_in_t.shape[::-1], BF16)),
        compiler_params=pltpu.CompilerParams(vmem_limit_bytes=VMEM_LIMIT),
        name="context_states",
    )(ctx2d, mod, n1g, w_in_t, wa2f, baf, wa2b, bab)


def _pass1_rows(r0, state, x_ref, mod_ref, n1g_ref, w_in_ref, wa2f_ref, baf_ref, wa2b_ref, bab_ref,
                qk_ref, laf_ref, v_ref, g_ref, glu_ref, ob_ref, carry_w, carry_r):
    rows = slice(r0, r0 + SUB)
    proj_blocks, gla_stages = [], []

    if carry_r is not None:
        qk_r, v_r, lab_r = carry_r
        first = r0 // CHUNK
        order = list(range(first + SUB // CHUNK - 1, first - 1, -1))
        consts = _gla_consts(False)
        b = {ci: _gla_cumsum(lab_r[_chunk_rows(ci), :], consts) for ci in order}

    if carry_w is not None:
        qk_w, v_w, lab_w = carry_w
        sh1 = mod_ref[0:1, 0:D_MODEL]
        sc1 = mod_ref[0:1, D_MODEL:2 * D_MODEL]
        h = (_rms_rows(x_ref[rows, :], n1g_ref[...]) * (1.0 + sc1) + sh1).astype(BF16)

        def proj(lo, hi):
            return _dot(h, w_in_ref[:, lo:hi])

        def glu_block(j):
            lo, hi = j * MXU_N, (j + 1) * MXU_N
            glu_ref[rows, lo:hi] = proj(lo, hi) * _sigmoid(proj(D_CONV + lo, D_CONV + hi))

        def qk_block(j):
            lo, hi = j * MXU_N, (j + 1) * MXU_N
            blk = proj(C_Q + lo, C_Q + hi)
            qk_ref[rows, lo:hi] = blk
            qk_w[rows, lo:hi] = blk

        def v_block(j):
            lo, hi = j * MXU_N, (j + 1) * MXU_N
            blk = proj(C_V + lo, C_V + hi).astype(BF16)
            v_ref[rows, lo:hi] = blk
            v_w[rows, lo:hi] = blk

        def g_block(j):
            lo, hi = j * MXU_N, (j + 1) * MXU_N
            g_ref[rows, lo:hi] = proj(C_G + lo, C_G + hi).astype(BF16)

        low_rank = {}

        def rank_block():
            low_rank["r"] = proj(C_R, D_IN)

        def decay_block():
            r = low_rank.pop("r")
            laf_ref[rows, :] = _log_decay(r[:, :GLA_RANK], wa2f_ref[...], baf_ref[...])
            lab_w[rows, :] = _log_decay(r[:, GLA_RANK:], wa2b_ref[...], bab_ref[...])

        proj_blocks = ([rank_block]
                       + [functools.partial(glu_block, j) for j in range(D_CONV // MXU_N)]
                       + [decay_block]
                       + [functools.partial(qk_block, j) for j in range(2 * D_GLA_K // MXU_N)]
                       + [functools.partial(v_block, j) for j in range(D_GLA_V // MXU_N)]
                       + [functools.partial(g_block, j) for j in range(D_GLA_V // MXU_N)])

    if carry_r is not None:
        prep, s_enter = {}, {}

        def gla_prep(ci):
            crows = _chunk_rows(ci)
            qk_c = qk_r[crows, :]
            dec_col, kv, ops = _gla_prep(qk_c[:, :D_GLA_K], qk_c[:, D_GLA_K:], v_r[crows, :], b[ci],
                                         consts, False, True)
            prep[ci] = ops
            s_enter[ci] = state["s"]
            state["s"] = dec_col * state["s"] + kv

        def gla_out(ci):
            ob_ref[_chunk_rows(ci), :] = _gla_out(prep[ci], s_enter[ci]).astype(BF16)

        gla_stages = ([functools.partial(gla_prep, ci) for ci in order]
                      + [functools.partial(gla_out, ci) for ci in order])

    if proj_blocks and gla_stages:
        _interleave(proj_blocks, gla_stages)
    else:
        for thunk in proj_blocks + gla_stages:
            thunk()


def _pass1_step(x_ref, mod_ref, n1g_ref, w_in_ref, wa2f_ref, baf_ref, wa2b_ref, bab_ref,
                qk_ref, laf_ref, v_ref, g_ref, glu_ref, ob_ref, s_ref, carry_w, carry_r):
    state = {"s": s_ref[...]} if carry_r is not None else None
    for r0 in range(x_ref.shape[0] - SUB, -1, -SUB):
        _pass1_rows(r0, state, x_ref, mod_ref, n1g_ref, w_in_ref, wa2f_ref, baf_ref, wa2b_ref,
                    bab_ref, qk_ref, laf_ref, v_ref, g_ref, glu_ref, ob_ref, carry_w, carry_r)
    if carry_r is not None:
        s_ref[...] = state["s"]


def _pass1_kernel(n_steps, n_mod2,
                  x_ref, mod_ref, n1g_ref, w_in_ref, wa2f_ref, baf_ref, wa2b_ref, bab_ref, sb0_ref,
                  wo_f32_ref, wg_f32_ref, wu_f32_ref, wd_f32_ref, silu_ref, w_mod_ref, b_mod_ref,
                  qk_ref, laf_ref, v_ref, g_ref, glu_ref, ob_ref,
                  wo_bf_ref, wg_bf_ref, wu_bf_ref, wd_bf_ref, mod2_ref,
                  s_ref, qk_a, v_a, lab_a, qk_b, v_b, lab_b):
    step = pl.program_id(0)
    last = n_steps - 1

    @pl.when(step < n_mod2)
    def _():
        mod2_ref[...] = (jnp.sum(w_mod_ref[...] * silu_ref[:, 0:1], axis=0, keepdims=True)
                         + b_mod_ref[...])

    for src, dst in ((wo_f32_ref, wo_bf_ref), (wg_f32_ref, wg_bf_ref), (wu_f32_ref, wu_bf_ref),
                     (wd_f32_ref, wd_bf_ref)):
        dst[...] = src[...].astype(BF16)
    bufs = ((qk_a, v_a, lab_a), (qk_b, v_b, lab_b))

    run = functools.partial(_pass1_step, x_ref, mod_ref, n1g_ref, w_in_ref, wa2f_ref, baf_ref,
                            wa2b_ref, bab_ref, qk_ref, laf_ref, v_ref, g_ref, glu_ref, ob_ref, s_ref)

    @pl.when(step == 0)
    def _():
        s_ref[...] = sb0_ref[...]
        run(bufs[0], None)

    for parity in (0, 1):
        @pl.when((step > 0) & (step < last) & (step % 2 == parity))
        def _():
            run(bufs[parity], bufs[1 - parity])

    @pl.when(step == last)
    def _():
        run(None, bufs[(last - 1) % 2])


def _slabs(w, n_slabs):
    rows, cols = w.shape
    assert rows % (n_slabs * 16) == 0
    return w.reshape(n_slabs, rows // n_slabs, cols)


def _pass1_call(x2d, mod, n1g, w_in_bf, wa2f, baf, wa2b, bab, sb0, w_out, w_gate, w_up, w_down,
                silu_ct, w_mod, b_mod):
    n_tok = x2d.shape[0]
    n_tiles = n_tok // TILE1
    mod2_lo = mod.shape[1] // D_MODEL
    mod2_blocks = w_mod.shape[1] // D_MODEL - mod2_lo
    assert mod2_blocks <= n_tiles
    mod2_in = lambda s: (0, mod2_lo + jnp.minimum(s, mod2_blocks - 1))
    mod2_out = lambda s: (0, jnp.minimum(s, mod2_blocks - 1))
    cast_in = [_slabs(w, n) for w, n in ((w_out, 32), (w_gate, 32), (w_up, 32), (w_down, 22))]
    assert all(w.shape[0] <= n_tiles for w in cast_in)
    cast_specs = [pl.BlockSpec((1,) + w.shape[1:],
                               functools.partial(lambda n, s: (jnp.minimum(s, n - 1), 0, 0), w.shape[0]))
                  for w in cast_in]
    proj = lambda s: (n_tiles - 1 - jnp.minimum(s, n_tiles - 1), 0)
    gla = lambda s: (n_tiles - 1 - jnp.maximum(s - 1, 0), 0)
    const = lambda s: (0, 0)
    full = lambda a: pl.BlockSpec(a.shape, const)
    out_shape = (
        jax.ShapeDtypeStruct((n_tok, 2 * D_GLA_K), F32),
        jax.ShapeDtypeStruct((n_tok, D_GLA_K), F32),
        jax.ShapeDtypeStruct((n_tok, D_GLA_V), BF16),
        jax.ShapeDtypeStruct((n_tok, D_GLA_V), BF16),
        jax.ShapeDtypeStruct((n_tok, D_CONV), F32),
        jax.ShapeDtypeStruct((n_tok, D_GLA_V), BF16),
    )
    carry = [pltpu.VMEM((TILE1, 2 * D_GLA_K), F32), pltpu.VMEM((TILE1, D_GLA_V), BF16),
             pltpu.VMEM((TILE1, D_GLA_K), F32)]
    return pl.pallas_call(
        functools.partial(_pass1_kernel, n_tiles + 1, mod2_blocks),
        grid=(n_tiles + 1,),
        in_specs=[pl.BlockSpec((TILE1, D_MODEL), proj), full(mod), full(n1g), full(w_in_bf),
                  full(wa2f), full(baf), full(wa2b), full(bab), full(sb0)] + cast_specs
        + [full(silu_ct), pl.BlockSpec((D_MODEL, D_MODEL), mod2_in), pl.BlockSpec((1, D_MODEL), mod2_in)],
        out_specs=(pl.BlockSpec((TILE1, 2 * D_GLA_K), proj), pl.BlockSpec((TILE1, D_GLA_K), proj),
                   pl.BlockSpec((TILE1, D_GLA_V), proj), pl.BlockSpec((TILE1, D_GLA_V), proj),
                   pl.BlockSpec((TILE1, D_CONV), proj), pl.BlockSpec((TILE1, D_GLA_V), gla))
        + tuple(cast_specs) + (pl.BlockSpec((1, D_MODEL), mod2_out),),
        out_shape=out_shape + tuple(jax.ShapeDtypeStruct(w.shape, BF16) for w in cast_in)
        + (jax.ShapeDtypeStruct((1, mod2_blocks * D_MODEL), F32),),
        scratch_shapes=[pltpu.VMEM((N_HEADS * HEAD_K, HEAD_V), F32)] + carry + carry,
        compiler_params=pltpu.CompilerParams(dimension_semantics=("arbitrary",),
                                             vmem_limit_bytes=VMEM_LIMIT),
        name="pass1_project_gla_bwd",
    )(x2d, mod, n1g, w_in_bf, wa2f, baf, wa2b, bab, sb0, *cast_in, silu_ct, w_mod, b_mod)


def _conv_stage_in(glu_ref, cbuf_ref):
    n_seg = glu_ref.shape[0] // CHUNK
    for j in range(D_CONV // LANES):
        for s in range(n_seg):
            base = s * SEG_PITCH + SEG_LEAD
            cbuf_ref[j, base:base + CHUNK, :] = glu_ref[_chunk_rows(s), j * LANES:(j + 1) * LANES]


def _conv_taps(segs, j, cbuf_ref, cw_ref, zero):
    accs = [zero for _ in segs]
    for t in range(CONV_WIDTH):
        w = cw_ref[0, t:t + 1, j * LANES:(j + 1) * LANES]
        for i, s in enumerate(segs):
            start = s * SEG_PITCH + SEG_LEAD - CONV_PAD + t
            accs[i] = accs[i] + cbuf_ref[j, start:start + CHUNK, :] * w
    return accs


def _conv_norm(cols, cb_ref, lng_ref, lnb_ref):
    y = jnp.concatenate(cols, axis=1) + cb_ref[...]
    mu = jnp.mean(y, axis=-1, keepdims=True)
    yc = y - mu
    var = jnp.mean(yc * yc, axis=-1, keepdims=True)
    return _silu(yc * lax.rsqrt(var + EPS) * lng_ref[...] + lnb_ref[...])


def _pass2_step(do_ffn, do_mix, x_ref, qk_ref, laf_ref, v_ref, g_ref, glu_ref, ob_ref, mod_ref,
                n2g_ref, gng_ref, fg_ref, cw_ref, cb_ref, lng_ref, lnb_ref,
                w_out_ref, wg_ref, wu_ref, wd_ref, out_ref, s_ref, mix_ref, cbuf_ref):
    n_seg = x_ref.shape[0] // CHUNK
    n_ff = D_FF // MXU_N
    side, ffn = [], []
    latest = {}

    if do_mix:
        chunks = list(range(n_seg))
        consts = _gla_consts(True)
        gng = gng_ref[...]
        _conv_stage_in(glu_ref, cbuf_ref)
        b = {ci: _gla_cumsum(laf_ref[_chunk_rows(ci), :], consts) for ci in chunks}

    if do_ffn:
        g1 = mod_ref[0:1, 0:D_MODEL]
        sh2 = mod_ref[0:1, D_MODEL:2 * D_MODEL]
        sc2 = mod_ref[0:1, 2 * D_MODEL:3 * D_MODEL]
        g2 = mod_ref[0:1, 3 * D_MODEL:4 * D_MODEL]
        x1 = x_ref[...] + g1 * _dot(mix_ref[...], w_out_ref[...])

    if do_mix:
        state = {"s": s_ref[...]}
        prep, s_enter = {}, {}

        def gla_prep(ci):
            rows = _chunk_rows(ci)
            qk_c = qk_ref[rows, :]
            dec_col, kv, ops = _gla_prep(qk_c[:, :D_GLA_K], qk_c[:, D_GLA_K:], v_ref[rows, :], b[ci],
                                         consts, True, True)
            prep[ci] = ops
            s_enter[ci] = state["s"]
            state["s"] = dec_col * state["s"] + kv

        def gla_out(ci):
            rows = _chunk_rows(ci)
            o = _gla_out(prep[ci], s_enter[ci]) + ob_ref[rows, :].astype(F32)
            gate = _silu(g_ref[rows, :].astype(F32))
            for h in range(N_HEADS):
                cols = slice(h * HEAD_V, (h + 1) * HEAD_V)
                mix_ref[rows, D_CONV + h * HEAD_V:D_CONV + (h + 1) * HEAD_V] = (
                    _rms_rows(o[:, cols], gng) * gate[:, cols]).astype(BF16)

        conv_cols = {}
        never = laf_ref[0:CHUNK, 0:LANES] > 0.0

        def conv_taps(segs, j):
            zero = (jnp.where(never, latest["ffn"], 0.0) if do_ffn
                    else jnp.zeros((CHUNK, LANES), F32))
            for s, acc in zip(segs, _conv_taps(segs, j, cbuf_ref, cw_ref, zero)):
                conv_cols[s, j] = acc

        def conv_out(s):
            cols = [conv_cols.pop((s, j)) for j in range(D_CONV // LANES)]
            mix_ref[_chunk_rows(s), 0:D_CONV] = _conv_norm(cols, cb_ref, lng_ref, lnb_ref).astype(BF16)

        side = [functools.partial(gla_prep, ci) for ci in chunks]
        for g0 in range(0, n_seg, CONV_GROUP):
            segs = chunks[g0:g0 + CONV_GROUP]
            side += [functools.partial(gla_out, ci) for ci in segs]
            side += [functools.partial(conv_taps, segs, j) for j in range(D_CONV // LANES)]
            side += [functools.partial(conv_out, ci) for ci in segs]

    n_early = min(2, len(side)) if do_ffn else len(side)
    for thunk in side[:n_early]:
        thunk()

    if do_ffn:
        h2 = (_rms_rows(x1, n2g_ref[...]) * (1.0 + sc2) + sh2).astype(BF16)
        act, acc = {}, {}

        def gate_up(j):
            cols = slice(j * MXU_N, (j + 1) * MXU_N)
            gate = _dot(h2, wg_ref[:, cols])
            latest["ffn"] = gate[0:CHUNK, 0:LANES]
            act[j] = (_silu(gate) * _dot(h2, wu_ref[:, cols])).astype(BF16)

        def down(j):
            part = _dot(act.pop(j), wd_ref[j * MXU_N:(j + 1) * MXU_N, :])
            acc["v"] = part if j == 0 else acc["v"] + part

        ffn = [functools.partial(gate_up, 0)]
        for j in range(1, n_ff):
            ffn += [functools.partial(gate_up, j), functools.partial(down, j - 1)]
        ffn.append(functools.partial(down, n_ff - 1))
        _interleave(ffn, side[n_early:])
    if do_mix:
        s_ref[...] = state["s"]
    if do_ffn:
        x2 = x1 + g2 * acc["v"]
        out_ref[...] = _rms_rows(x2, fg_ref[...])


def _pass2_kernel(n_steps,
                  x_ref, qk_ref, laf_ref, v_ref, g_ref, glu_ref, ob_ref, mod_ref, n2g_ref,
                  gng_ref, fg_ref, cw_ref, cb_ref, lng_ref, lnb_ref,
                  w_out_ref, wg_ref, wu_ref, wd_ref, sf0_ref,
                  out_ref, s_ref, mix_ref, cbuf_ref):
    step = pl.program_id(0)
    last = n_steps - 1
    run = functools.partial(_pass2_step, x_ref=x_ref, qk_ref=qk_ref, laf_ref=laf_ref, v_ref=v_ref,
                            g_ref=g_ref, glu_ref=glu_ref, ob_ref=ob_ref, mod_ref=mod_ref,
                            n2g_ref=n2g_ref, gng_ref=gng_ref, fg_ref=fg_ref, cw_ref=cw_ref,
                            cb_ref=cb_ref, lng_ref=lng_ref, lnb_ref=lnb_ref, w_out_ref=w_out_ref,
                            wg_ref=wg_ref, wu_ref=wu_ref, wd_ref=wd_ref,
                            out_ref=out_ref, s_ref=s_ref, mix_ref=mix_ref, cbuf_ref=cbuf_ref)

    @pl.when(step == 0)
    def _():
        s_ref[...] = sf0_ref[...]
        cbuf_ref[...] = jnp.zeros(cbuf_ref.shape, F32)
        run(False, True)

    @pl.when((step > 0) & (step < last))
    def _():
        run(True, True)

    @pl.when(step == last)
    def _():
        run(True, False)


def _pass2_call(x2d, qk, laf, v, g, glu, ob, mod, n2g, gng, fg, cw, cb, lng, lnb,
                w_out_bf, wg_bf, wu_bf, wd_bf, sf0):
    n_tok = x2d.shape[0]
    n_tiles = n_tok // TILE
    mixer = lambda s: (jnp.minimum(s, n_tiles - 1), 0)
    ffn = lambda s: (jnp.maximum(s - 1, 0), 0)
    full = lambda a: pl.BlockSpec(a.shape, lambda s: (0,) * a.ndim)
    mixer_spec = lambda w: pl.BlockSpec((TILE, w), mixer)
    return pl.pallas_call(
        functools.partial(_pass2_kernel, n_tiles + 1),
        grid=(n_tiles + 1,),
        in_specs=[pl.BlockSpec((TILE, D_MODEL), ffn), mixer_spec(2 * D_GLA_K), mixer_spec(D_GLA_K),
                  mixer_spec(D_GLA_V), mixer_spec(D_GLA_V), mixer_spec(D_CONV), mixer_spec(D_GLA_V),
                  full(mod), full(n2g), full(gng), full(fg), full(cw), full(cb), full(lng), full(lnb),
                  full(w_out_bf), full(wg_bf), full(wu_bf), full(wd_bf), full(sf0)],
        out_specs=pl.BlockSpec((TILE, D_MODEL), ffn),
        out_shape=jax.ShapeDtypeStruct((n_tok, D_MODEL), F32),
        scratch_shapes=[pltpu.VMEM((N_HEADS * HEAD_K, HEAD_V), F32),
                        pltpu.VMEM((TILE, D_CONV + D_GLA_V), BF16),
                        pltpu.VMEM((D_CONV // LANES, (TILE // CHUNK) * SEG_PITCH, LANES), F32)],
        compiler_params=pltpu.CompilerParams(dimension_semantics=("arbitrary",),
                                             vmem_limit_bytes=VMEM_LIMIT),
        name="pass2_gla_fwd_conv_merge_ffn",
    )(x2d, qk, laf, v, g, glu, ob, mod, n2g, gng, fg, cw, cb, lng, lnb,
      w_out_bf, wg_bf, wu_bf, wd_bf, sf0)


def kernel(x, c, ctx, c_ctx, w_mod, b_mod, norm1_g, norm2_g, w_in, conv_w, conv_b, conv_ln_g,
           conv_ln_b, w_a2_f, b_a_f, w_a2_b, b_a_b, gla_norm_g, w_out, w_gate, w_up, w_down, final_g):
    bsz, n_lat, d = x.shape
    assert bsz == 1 and d == D_MODEL and n_lat % TILE == 0 and ctx.shape[1] % CHUNK == 0
    assert n_lat % TILE1 == 0 and TILE1 % SUB == 0
    assert (TILE // CHUNK) % CONV_GROUP == 0
    assert w_mod.shape[0] == 1, "single layer"
    row = lambda a: a.reshape(1, -1)

    ct = jnp.concatenate([c.reshape(D_MODEL, 1), c_ctx.reshape(D_MODEL, 1)], axis=1)
    mod, silu_ct = _mod_call(ct, w_mod[0], row(b_mod[0]))

    wa2f, baf = w_a2_f[0], row(b_a_f[0])
    wa2b, bab = w_a2_b[0], row(b_a_b[0])
    n1g = row(norm1_g[0])

    sf0, sb0, w_in_bf = _ctx_call(ctx[0], mod, n1g, w_in[0].T, wa2f, baf, wa2b, bab)

    x2d = x[0]
    qk, laf, v, g, glu, ob, wo_bf, wg_bf, wu_bf, wd_bf, mod2 = _pass1_call(
        x2d, mod, n1g, w_in_bf, wa2f, baf, wa2b, bab, sb0, w_out[0], w_gate[0], w_up[0], w_down[0],
        silu_ct, w_mod[0], row(b_mod[0]))
    unslab = lambda w: w.reshape(-1, w.shape[-1])

    out = _pass2_call(
        x2d, qk, laf, v, g, glu, ob, mod2, row(norm2_g[0]), row(gla_norm_g[0]), row(final_g),
        conv_w, row(conv_b[0]), row(conv_ln_g[0]), row(conv_ln_b[0]),
        unslab(wo_bf), unslab(wg_bf), unslab(wu_bf), unslab(wd_bf), sf0)
    return out.reshape(bsz, n_lat, d)
```

```python
import functools

import jax
import jax.numpy as jnp
from jax import lax
from jax.experimental import pallas as pl
from jax.experimental.pallas import tpu as pltpu

D_MODEL = 1024
D_CONV = 512
CONV_WIDTH = 31
CONV_PAD = 15
N_HEADS = 4
HEAD_K = 64
HEAD_V = 128
D_GLA_K = N_HEADS * HEAD_K
D_GLA_V = N_HEADS * HEAD_V
GLA_RANK = 16
GLA_TAU = 16.0
CHUNK = 64
EPS = 1e-6
D_FF = 2816

C_Q = 2 * D_CONV
C_K = C_Q + D_GLA_K
C_V = C_K + D_GLA_K
C_G = C_V + D_GLA_V
C_R = C_G + D_GLA_V
D_IN = C_R + 2 * GLA_RANK

LANES = 128
MXU_N = 256
SEG_PITCH = 96
SEG_LEAD = 16
CONV_GROUP = 2
SUB = 256
TILE1 = 512
TILE = 256
VMEM_LIMIT = 56 * 1024 * 1024

F32 = jnp.float32
BF16 = jnp.bfloat16


def _dot(a, b):
    return jnp.dot(a, b, preferred_element_type=F32)


def _dot_nt(a, b):
    return lax.dot_general(a, b, (((1,), (1,)), ((), ())), preferred_element_type=F32)


def _sigmoid(x):
    return 1.0 / (1.0 + jnp.exp(-x))


def _silu(x):
    return x * _sigmoid(x)


def _log_sigmoid(z):
    return jnp.minimum(z, 0.0) - jnp.log(1.0 + jnp.exp(-jnp.abs(z)))


def _rms_rows(x, g):
    ms = jnp.mean(x * x, axis=-1, keepdims=True)
    return x * lax.rsqrt(ms + EPS) * g


def _interleave(main, side):
    n_main, n_side = len(main), len(side)
    done = 0
    for i, thunk in enumerate(main):
        thunk()
        want = ((i + 1) * n_side) // n_main
        while done < want:
            side[done]()
            done += 1


def _gla_consts(forward):
    r = lax.broadcasted_iota(jnp.int32, (CHUNK, CHUNK), 0)
    c = lax.broadcasted_iota(jnp.int32, (CHUNK, CHUNK), 1)
    tri = (c <= r) if forward else (c >= r)
    tri_bf = jnp.where(tri, 1.0, 0.0).astype(BF16)
    rr = lax.broadcasted_iota(jnp.int32, (N_HEADS * CHUNK, LANES), 0) % CHUNK
    cc = lax.broadcasted_iota(jnp.int32, (N_HEADS * CHUNK, LANES), 1)
    keep = ((cc <= rr) if forward else (cc >= rr)) & (cc < CHUNK)
    lane = lax.broadcasted_iota(jnp.int32, (CHUNK, D_GLA_K), 1)
    head_masks = [(lane >= h * HEAD_K) & (lane < (h + 1) * HEAD_K) for h in range(N_HEADS)]
    return tri_bf, keep, head_masks


def _gla_cumsum(la, consts):
    tri_bf = consts[0]
    la_hi = la.astype(BF16)
    la_lo = (la - la_hi.astype(F32)).astype(BF16)
    return _dot(tri_bf, la_hi) + _dot(tri_bf, la_lo)


def _gla_prep(q, k, v_bf, b, consts, forward, with_output):
    _, keep, head_masks = consts
    b_last = b[CHUNK - 1:CHUNK, :] if forward else b[0:1, :]
    k_w = k * jnp.exp(b_last - b)
    e_last = jnp.exp(b_last)

    tb = jnp.concatenate(
        [k_w, jnp.broadcast_to(e_last, (8, D_GLA_K)), jnp.zeros((CHUNK - 8, D_GLA_K), F32)], axis=0)
    tbt = tb.T
    dec_col = tbt[:, CHUNK:CHUNK + 1]
    kwt_bf = tbt.astype(BF16)
    zeros_v = jnp.zeros((CHUNK, HEAD_V), BF16)
    v_pads = [jnp.concatenate([v_bf[:, h * HEAD_V:(h + 1) * HEAD_V], zeros_v], axis=0)
              for h in range(N_HEADS)]
    kv = jnp.concatenate([_dot(kwt_bf[h * CHUNK:(h + 1) * CHUNK], v_pads[h])
                          for h in range(N_HEADS)], axis=0)
    if not with_output:
        return dec_col, kv, None
    q_t = (q * jnp.exp(b) * (HEAD_K ** -0.5)).astype(BF16)
    k_t = (k * jnp.exp(-b)).astype(BF16)
    qm = jnp.concatenate([jnp.where(m, q_t, jnp.zeros_like(q_t)) for m in head_masks], axis=0)
    kt_pad = jnp.concatenate([k_t, jnp.zeros_like(k_t)], axis=0)
    attn = _dot_nt(qm, kt_pad)
    p_bf = jnp.where(keep, attn, 0.0).astype(BF16)
    return dec_col, kv, (qm, p_bf, v_pads)


def _gla_out(ops, s_enter):
    qm, p_bf, v_pads = ops
    o_inter = _dot(qm, s_enter.astype(BF16))
    outs = []
    for h in range(N_HEADS):
        rows = slice(h * CHUNK, (h + 1) * CHUNK)
        outs.append(_dot(p_bf[rows], v_pads[h]) + o_inter[rows])
    return jnp.concatenate(outs, axis=1)


def _log_decay(r, w_a2, b_a):
    z = _dot(r.astype(BF16), w_a2.astype(BF16)) + b_a
    return _log_sigmoid(z) * (1.0 / GLA_TAU)


def _chunk_rows(ci):
    return slice(ci * CHUNK, (ci + 1) * CHUNK)


def _mod_kernel(ct_ref, w_ref, b_ref, o_ref, s_ref):
    s = _silu(ct_ref[...])
    s_ref[...] = s
    w = w_ref[...]
    for j in range(2):
        o_ref[j:j + 1, :] = jnp.sum(w * s[:, j:j + 1], axis=0, keepdims=True) + b_ref[...]


def _mod_call(ct, w_mod, b_mod):
    tn = 512
    n = 2 * D_MODEL
    return pl.pallas_call(
        _mod_kernel,
        grid=(n // tn,),
        in_specs=[pl.BlockSpec((D_MODEL, 2), lambda j: (0, 0)),
                  pl.BlockSpec((D_MODEL, tn), lambda j: (0, j)),
                  pl.BlockSpec((1, tn), lambda j: (0, j))],
        out_specs=(pl.BlockSpec((2, tn), lambda j: (0, j)),
                   pl.BlockSpec((D_MODEL, 2), lambda j: (0, 0))),
        out_shape=(jax.ShapeDtypeStruct((2, n), F32), jax.ShapeDtypeStruct((D_MODEL, 2), F32)),
        compiler_params=pltpu.CompilerParams(dimension_semantics=("arbitrary",),
                                             vmem_limit_bytes=VMEM_LIMIT),
        name="adaln_mod",
    )(ct, w_mod, b_mod)


def _ctx_kernel(ctx_ref, mod_ref, n1g_ref, w_in_t_ref, wa2f_ref, baf_ref, wa2b_ref, bab_ref,
                sf_ref, sb_ref, w_in_ref):
    n_full = D_IN // MXU_N
    for j in range(n_full):
        cols = slice(j * MXU_N, (j + 1) * MXU_N)
        w_in_ref[:, cols] = w_in_t_ref[cols, :].T.astype(BF16)
    tail = w_in_t_ref[D_IN - LANES:D_IN, :].T
    w_in_ref[:, n_full * MXU_N:D_IN] = tail[:, LANES - (D_IN - n_full * MXU_N):].astype(BF16)
    n_chunks = ctx_ref.shape[0] // CHUNK
    sh = mod_ref[1:2, 0:D_MODEL]
    sc = mod_ref[1:2, D_MODEL:2 * D_MODEL]
    h = (_rms_rows(ctx_ref[...], n1g_ref[...]) * (1.0 + sc) + sh).astype(BF16)
    kv = _dot(h, w_in_ref[:, C_K:C_G])
    r = _dot(h, w_in_ref[:, C_R:D_IN])
    k = kv[:, :D_GLA_K]
    v_bf = kv[:, D_GLA_K:].astype(BF16)
    la_f = _log_decay(r[:, :GLA_RANK], wa2f_ref[...], baf_ref[...])
    la_b = _log_decay(r[:, GLA_RANK:], wa2b_ref[...], bab_ref[...])
    for forward, la, out_ref in ((True, la_f, sf_ref), (False, la_b, sb_ref)):
        consts = _gla_consts(forward)
        order = list(range(n_chunks)) if forward else list(range(n_chunks - 1, -1, -1))
        b = {ci: _gla_cumsum(la[_chunk_rows(ci)], consts) for ci in order}
        s = jnp.zeros((N_HEADS * HEAD_K, HEAD_V), F32)
        for ci in order:
            rows = _chunk_rows(ci)
            dec_col, kv_c, _ = _gla_prep(None, k[rows], v_bf[rows], b[ci], consts, forward, False)
            s = dec_col * s + kv_c
        out_ref[...] = s


def _ctx_call(ctx2d, mod, n1g, w_in_t, wa2f, baf, wa2b, bab):
    st = jax.ShapeDtypeStruct((N_HEADS * HEAD_K, HEAD_V), F32)
    return pl.pallas_call(
        _ctx_kernel,
        out_shape=(st, st, jax.ShapeDtypeStruct(w_in_t.shape[::-1], BF16)),
        compiler_params=pltpu.CompilerParams(vmem_limit_bytes=VMEM_LIMIT),
        name="context_states",
    )(ctx2d, mod, n1g, w_in_t, wa2f, baf, wa2b, bab)


def _pass1_rows(r0, state, x_ref, mod_ref, n1g_ref, w_in_ref, wa2f_ref, baf_ref, wa2b_ref, bab_ref,
                qk_ref, laf_ref, v_ref, g_ref, glu_ref, ob_ref, carry_w, carry_r):
    rows = slice(r0, r0 + SUB)
    proj_blocks, gla_stages = [], []

    if carry_r is not None:
        qk_r, v_r, lab_r = carry_r
        first = r0 // CHUNK
        order = list(range(first + SUB // CHUNK - 1, first - 1, -1))
        consts = _gla_consts(False)
        b = {ci: _gla_cumsum(lab_r[_chunk_rows(ci), :], consts) for ci in order}

    if carry_w is not None:
        qk_w, v_w, lab_w = carry_w
        sh1 = mod_ref[0:1, 0:D_MODEL]
        sc1 = mod_ref[0:1, D_MODEL:2 * D_MODEL]
        h = (_rms_rows(x_ref[rows, :], n1g_ref[...]) * (1.0 + sc1) + sh1).astype(BF16)

        def proj(lo, hi):
            return _dot(h, w_in_ref[:, lo:hi])

        def glu_block(j):
            lo, hi = j * MXU_N, (j + 1) * MXU_N
            glu_ref[rows, lo:hi] = proj(lo, hi) * _sigmoid(proj(D_CONV + lo, D_CONV + hi))

        def qk_block(j):
            lo, hi = j * MXU_N, (j + 1) * MXU_N
            blk = proj(C_Q + lo, C_Q + hi)
            qk_ref[rows, lo:hi] = blk
            qk_w[rows, lo:hi] = blk

        def v_block(j):
            lo, hi = j * MXU_N, (j + 1) * MXU_N
            blk = proj(C_V + lo, C_V + hi).astype(BF16)
            v_ref[rows, lo:hi] = blk
            v_w[rows, lo:hi] = blk

        def g_block(j):
            lo, hi = j * MXU_N, (j + 1) * MXU_N
            g_ref[rows, lo:hi] = proj(C_G + lo, C_G + hi).astype(BF16)

        low_rank = {}

        def rank_block():
            low_rank["r"] = proj(C_R, D_IN)

        def decay_block():
            r = low_rank.pop("r")
            laf_ref[rows, :] = _log_decay(r[:, :GLA_RANK], wa2f_ref[...], baf_ref[...])
            lab_w[rows, :] = _log_decay(r[:, GLA_RANK:], wa2b_ref[...], bab_ref[...])

        proj_blocks = ([rank_block]
                       + [functools.partial(glu_block, j) for j in range(D_CONV // MXU_N)]
                       + [decay_block]
                       + [functools.partial(qk_block, j) for j in range(2 * D_GLA_K // MXU_N)]
                       + [functools.partial(v_block, j) for j in range(D_GLA_V // MXU_N)]
                       + [functools.partial(g_block, j) for j in range(D_GLA_V // MXU_N)])

    if carry_r is not None:
        prep, s_enter = {}, {}

        def gla_prep(ci):
            crows = _chunk_rows(ci)
            qk_c = qk_r[crows, :]
            dec_col, kv, ops = _gla_prep(qk_c[:, :D_GLA_K], qk_c[:, D_GLA_K:], v_r[crows, :], b[ci],
                                         consts, False, True)
            prep[ci] = ops
            s_enter[ci] = state["s"]
            state["s"] = dec_col * state["s"] + kv

        def gla_out(ci):
            ob_ref[_chunk_rows(ci), :] = _gla_out(prep[ci], s_enter[ci]).astype(BF16)

        gla_stages = ([functools.partial(gla_prep, ci) for ci in order]
                      + [functools.partial(gla_out, ci) for ci in order])

    if proj_blocks and gla_stages:
        _interleave(proj_blocks, gla_stages)
    else:
        for thunk in proj_blocks + gla_stages:
            thunk()


def _pass1_step(x_ref, mod_ref, n1g_ref, w_in_ref, wa2f_ref, baf_ref, wa2b_ref, bab_ref,
                qk_ref, laf_ref, v_ref, g_ref, glu_ref, ob_ref, s_ref, carry_w, carry_r):
    state = {"s": s_ref[...]} if carry_r is not None else None
    for r0 in range(x_ref.shape[0] - SUB, -1, -SUB):
        _pass1_rows(r0, state, x_ref, mod_ref, n1g_ref, w_in_ref, wa2f_ref, baf_ref, wa2b_ref,
                    bab_ref, qk_ref, laf_ref, v_ref, g_ref, glu_ref, ob_ref, carry_w, carry_r)
    if carry_r is not None:
        s_ref[...] = state["s"]


def _pass1_kernel(n_steps, n_mod2,
                  x_ref, mod_ref, n1g_ref, w_in_ref, wa2f_ref, baf_ref, wa2b_ref, bab_ref, sb0_ref,
                  wo_f32_ref, wg_f32_ref, wu_f32_ref, wd_f32_ref, silu_ref, w_mod_ref, b_mod_ref,
                  qk_ref, laf_ref, v_ref, g_ref, glu_ref, ob_ref,
                  wo_bf_ref, wg_bf_ref, wu_bf_ref, wd_bf_ref, mod2_ref,
                  s_ref, qk_a, v_a, lab_a, qk_b, v_b, lab_b):
    step = pl.program_id(0)
    last = n_steps - 1

    @pl.when(step < n_mod2)
    def _():
        mod2_ref[...] = (jnp.sum(w_mod_ref[...] * silu_ref[:, 0:1], axis=0, keepdims=True)
                         + b_mod_ref[...])

    for src, dst in ((wo_f32_ref, wo_bf_ref), (wg_f32_ref, wg_bf_ref), (wu_f32_ref, wu_bf_ref),
                     (wd_f32_ref, wd_bf_ref)):
        dst[...] = src[...].astype(BF16)
    bufs = ((qk_a, v_a, lab_a), (qk_b, v_b, lab_b))

    run = functools.partial(_pass1_step, x_ref, mod_ref, n1g_ref, w_in_ref, wa2f_ref, baf_ref,
                            wa2b_ref, bab_ref, qk_ref, laf_ref, v_ref, g_ref, glu_ref, ob_ref, s_ref)

    @pl.when(step == 0)
    def _():
        s_ref[...] = sb0_ref[...]
        run(bufs[0], None)

    for parity in (0, 1):
        @pl.when((step > 0) & (step < last) & (step % 2 == parity))
        def _():
            run(bufs[parity], bufs[1 - parity])

    @pl.when(step == last)
    def _():
        run(None, bufs[(last - 1) % 2])


def _slabs(w, n_slabs):
    rows, cols = w.shape
    assert rows % (n_slabs * 16) == 0
    return w.reshape(n_slabs, rows // n_slabs, cols)


def _pass1_call(x2d, mod, n1g, w_in_bf, wa2f, baf, wa2b, bab, sb0, w_out, w_gate, w_up, w_down,
                silu_ct, w_mod, b_mod):
    n_tok = x2d.shape[0]
    n_tiles = n_tok // TILE1
    mod2_lo = mod.shape[1] // D_MODEL
    mod2_blocks = w_mod.shape[1] // D_MODEL - mod2_lo
    assert mod2_blocks <= n_tiles
    mod2_in = lambda s: (0, mod2_lo + jnp.minimum(s, mod2_blocks - 1))
    mod2_out = lambda s: (0, jnp.minimum(s, mod2_blocks - 1))
    cast_in = [_slabs(w, n) for w, n in ((w_out, 32), (w_gate, 32), (w_up, 32), (w_down, 22))]
    assert all(w.shape[0] <= n_tiles for w in cast_in)
    cast_specs = [pl.BlockSpec((1,) + w.shape[1:],
                               functools.partial(lambda n, s: (jnp.minimum(s, n - 1), 0, 0), w.shape[0]))
                  for w in cast_in]
    proj = lambda s: (n_tiles - 1 - jnp.minimum(s, n_tiles - 1), 0)
    gla = lambda s: (n_tiles - 1 - jnp.maximum(s - 1, 0), 0)
    const = lambda s: (0, 0)
    full = lambda a: pl.BlockSpec(a.shape, const)
    out_shape = (
        jax.ShapeDtypeStruct((n_tok, 2 * D_GLA_K), F32),
        jax.ShapeDtypeStruct((n_tok, D_GLA_K), F32),
        jax.ShapeDtypeStruct((n_tok, D_GLA_V), BF16),
        jax.ShapeDtypeStruct((n_tok, D_GLA_V), BF16),
        jax.ShapeDtypeStruct((n_tok, D_CONV), F32),
        jax.ShapeDtypeStruct((n_tok, D_GLA_V), BF16),
    )
    carry = [pltpu.VMEM((TILE1, 2 * D_GLA_K), F32), pltpu.VMEM((TILE1, D_GLA_V), BF16),
             pltpu.VMEM((TILE1, D_GLA_K), F32)]
    return pl.pallas_call(
        functools.partial(_pass1_kernel, n_tiles + 1, mod2_blocks),
        grid=(n_tiles + 1,),
        in_specs=[pl.BlockSpec((TILE1, D_MODEL), proj), full(mod), full(n1g), full(w_in_bf),
                  full(wa2f), full(baf), full(wa2b), full(bab), full(sb0)] + cast_specs
        + [full(silu_ct), pl.BlockSpec((D_MODEL, D_MODEL), mod2_in), pl.BlockSpec((1, D_MODEL), mod2_in)],
        out_specs=(pl.BlockSpec((TILE1, 2 * D_GLA_K), proj), pl.BlockSpec((TILE1, D_GLA_K), proj),
                   pl.BlockSpec((TILE1, D_GLA_V), proj), pl.BlockSpec((TILE1, D_GLA_V), proj),
                   pl.BlockSpec((TILE1, D_CONV), proj), pl.BlockSpec((TILE1, D_GLA_V), gla))
        + tuple(cast_specs) + (pl.BlockSpec((1, D_MODEL), mod2_out),),
        out_shape=out_shape + tuple(jax.ShapeDtypeStruct(w.shape, BF16) for w in cast_in)
        + (jax.ShapeDtypeStruct((1, mod2_blocks * D_MODEL), F32),),
        scratch_shapes=[pltpu.VMEM((N_HEADS * HEAD_K, HEAD_V), F32)] + carry + carry,
        compiler_params=pltpu.CompilerParams(dimension_semantics=("arbitrary",),
                                             vmem_limit_bytes=VMEM_LIMIT),
        name="pass1_project_gla_bwd",
    )(x2d, mod, n1g, w_in_bf, wa2f, baf, wa2b, bab, sb0, *cast_in, silu_ct, w_mod, b_mod)


def _conv_stage_in(glu_ref, cbuf_ref):
    n_seg = glu_ref.shape[0] // CHUNK
    for j in range(D_CONV // LANES):
        for s in range(n_seg):
            base = s * SEG_PITCH + SEG_LEAD
            cbuf_ref[j, base:base + CHUNK, :] = glu_ref[_chunk_rows(s), j * LANES:(j + 1) * LANES]


def _conv_taps(segs, j, cbuf_ref, cw_ref, zero):
    accs = [zero for _ in segs]
    for t in range(CONV_WIDTH):
        w = cw_ref[t:t + 1, j * LANES:(j + 1) * LANES]
        for i, s in enumerate(segs):
            start = s * SEG_PITCH + SEG_LEAD - CONV_PAD + t
            accs[i] = accs[i] + cbuf_ref[j, start:start + CHUNK, :] * w
    return accs


def _conv_norm(cols, cb_ref, lng_ref, lnb_ref):
    y = jnp.concatenate(cols, axis=1) + cb_ref[...]
    mu = jnp.mean(y, axis=-1, keepdims=True)
    yc = y - mu
    var = jnp.mean(yc * yc, axis=-1, keepdims=True)
    return _silu(yc * lax.rsqrt(var + EPS) * lng_ref[...] + lnb_ref[...])


def _pass2_step(do_ffn, do_mix, x_ref, qk_ref, laf_ref, v_ref, g_ref, glu_ref, ob_ref, mod_ref,
                n2g_ref, gng_ref, fg_ref, cw_ref, cb_ref, lng_ref, lnb_ref,
                w_out_ref, wg_ref, wu_ref, wd_ref, out_ref, s_ref, mix_ref, cbuf_ref):
    n_seg = x_ref.shape[0] // CHUNK
    n_ff = D_FF // MXU_N
    side, ffn = [], []
    latest = {}

    if do_mix:
        chunks = list(range(n_seg))
        consts = _gla_consts(True)
        gng = gng_ref[...]
        _conv_stage_in(glu_ref, cbuf_ref)
        b = {ci: _gla_cumsum(laf_ref[_chunk_rows(ci), :], consts) for ci in chunks}

    if do_ffn:
        g1 = mod_ref[0:1, 0:D_MODEL]
        sh2 = mod_ref[0:1, D_MODEL:2 * D_MODEL]
        sc2 = mod_ref[0:1, 2 * D_MODEL:3 * D_MODEL]
        g2 = mod_ref[0:1, 3 * D_MODEL:4 * D_MODEL]
        x1 = x_ref[...] + g1 * _dot(mix_ref[...], w_out_ref[...])

    if do_mix:
        state = {"s": s_ref[...]}
        prep, s_enter = {}, {}

        def gla_prep(ci):
            rows = _chunk_rows(ci)
            qk_c = qk_ref[rows, :]
            dec_col, kv, ops = _gla_prep(qk_c[:, :D_GLA_K], qk_c[:, D_GLA_K:], v_ref[rows, :], b[ci],
                                         consts, True, True)
            prep[ci] = ops
            s_enter[ci] = state["s"]
            state["s"] = dec_col * state["s"] + kv

        def gla_out(ci):
            rows = _chunk_rows(ci)
            o = _gla_out(prep[ci], s_enter[ci]) + ob_ref[rows, :].astype(F32)
            gate = _silu(g_ref[rows, :].astype(F32))
            for h in range(N_HEADS):
                cols = slice(h * HEAD_V, (h + 1) * HEAD_V)
                mix_ref[rows, D_CONV + h * HEAD_V:D_CONV + (h + 1) * HEAD_V] = (
                    _rms_rows(o[:, cols], gng) * gate[:, cols]).astype(BF16)

        conv_cols = {}
        never = laf_ref[0:CHUNK, 0:LANES] > 0.0

        def conv_taps(segs, j):
            zero = (jnp.where(never, latest["ffn"], 0.0) if do_ffn
                    else jnp.zeros((CHUNK, LANES), F32))
            for s, acc in zip(segs, _conv_taps(segs, j, cbuf_ref, cw_ref, zero)):
                conv_cols[s, j] = acc

        def conv_out(s):
            cols = [conv_cols.pop((s, j)) for j in range(D_CONV // LANES)]
            mix_ref[_chunk_rows(s), 0:D_CONV] = _conv_norm(cols, cb_ref, lng_ref, lnb_ref).astype(BF16)

        side = [functools.partial(gla_prep, ci) for ci in chunks]
        for g0 in range(0, n_seg, CONV_GROUP):
            segs = chunks[g0:g0 + CONV_GROUP]
            side += [functools.partial(gla_out, ci) for ci in segs]
            side += [functools.partial(conv_taps, segs, j) for j in range(D_CONV // LANES)]
            side += [functools.partial(conv_out, ci) for ci in segs]

    n_early = n_seg if do_ffn else len(side)
    for thunk in side[:n_early]:
        thunk()

    if do_ffn:
        h2 = (_rms_rows(x1, n2g_ref[...]) * (1.0 + sc2) + sh2).astype(BF16)
        act, acc = {}, {}

        def gate_up(j):
            cols = slice(j * MXU_N, (j + 1) * MXU_N)
            gate = _dot(h2, wg_ref[:, cols])
            latest["ffn"] = gate[0:CHUNK, 0:LANES]
            act[j] = (_silu(gate) * _dot(h2, wu_ref[:, cols])).astype(BF16)

        def down(j):
            part = _dot(act.pop(j), wd_ref[j * MXU_N:(j + 1) * MXU_N, :])
            acc["v"] = part if j == 0 else acc["v"] + part

        ffn = [functools.partial(gate_up, 0)]
        for j in range(1, n_ff):
            ffn += [functools.partial(gate_up, j), functools.partial(down, j - 1)]
        ffn.append(functools.partial(down, n_ff - 1))
        _interleave(ffn, side[n_early:])
    if do_mix:
        s_ref[...] = state["s"]
    if do_ffn:
        x2 = x1 + g2 * acc["v"]
        out_ref[...] = _rms_rows(x2, fg_ref[...])


def _pass2_kernel(n_steps,
                  x_ref, qk_ref, laf_ref, v_ref, g_ref, glu_ref, ob_ref, mod_ref, n2g_ref,
                  gng_ref, fg_ref, cw_ref, cb_ref, lng_ref, lnb_ref,
                  w_out_ref, wg_ref, wu_ref, wd_ref, sf0_ref,
                  out_ref, s_ref, mix_ref, cbuf_ref):
    step = pl.program_id(0)
    last = n_steps - 1
    run = functools.partial(_pass2_step, x_ref=x_ref, qk_ref=qk_ref, laf_ref=laf_ref, v_ref=v_ref,
                            g_ref=g_ref, glu_ref=glu_ref, ob_ref=ob_ref, mod_ref=mod_ref,
                            n2g_ref=n2g_ref, gng_ref=gng_ref, fg_ref=fg_ref, cw_ref=cw_ref,
                            cb_ref=cb_ref, lng_ref=lng_ref, lnb_ref=lnb_ref, w_out_ref=w_out_ref,
                            wg_ref=wg_ref, wu_ref=wu_ref, wd_ref=wd_ref,
                            out_ref=out_ref, s_ref=s_ref, mix_ref=mix_ref, cbuf_ref=cbuf_ref)

    @pl.when(step == 0)
    def _():
        s_ref[...] = sf0_ref[...]
        cbuf_ref[...] = jnp.zeros(cbuf_ref.shape, F32)
        run(False, True)

    @pl.when((step > 0) & (step < last))
    def _():
        run(True, True)

    @pl.when(step == last)
    def _():
        run(True, False)


def _pass2_call(x2d, qk, laf, v, g, glu, ob, mod, n2g, gng, fg, cw, cb, lng, lnb,
                w_out_bf, wg_bf, wu_bf, wd_bf, sf0):
    n_tok = x2d.shape[0]
    n_tiles = n_tok // TILE
    mixer = lambda s: (jnp.minimum(s, n_tiles - 1), 0)
    ffn = lambda s: (jnp.maximum(s - 1, 0), 0)
    const = lambda s: (0, 0)
    full = lambda a: pl.BlockSpec(a.shape, const)
    mixer_spec = lambda w: pl.BlockSpec((TILE, w), mixer)
    return pl.pallas_call(
        functools.partial(_pass2_kernel, n_tiles + 1),
        grid=(n_tiles + 1,),
        in_specs=[pl.BlockSpec((TILE, D_MODEL), ffn), mixer_spec(2 * D_GLA_K), mixer_spec(D_GLA_K),
                  mixer_spec(D_GLA_V), mixer_spec(D_GLA_V), mixer_spec(D_CONV), mixer_spec(D_GLA_V),
                  full(mod), full(n2g), full(gng), full(fg), full(cw), full(cb), full(lng), full(lnb),
                  full(w_out_bf), full(wg_bf), full(wu_bf), full(wd_bf), full(sf0)],
        out_specs=pl.BlockSpec((TILE, D_MODEL), ffn),
        out_shape=jax.ShapeDtypeStruct((n_tok, D_MODEL), F32),
        scratch_shapes=[pltpu.VMEM((N_HEADS * HEAD_K, HEAD_V), F32),
                        pltpu.VMEM((TILE, D_CONV + D_GLA_V), BF16),
                        pltpu.VMEM((D_CONV // LANES, (TILE // CHUNK) * SEG_PITCH, LANES), F32)],
        compiler_params=pltpu.CompilerParams(dimension_semantics=("arbitrary",),
                                             vmem_limit_bytes=VMEM_LIMIT),
        name="pass2_gla_fwd_conv_merge_ffn",
    )(x2d, qk, laf, v, g, glu, ob, mod, n2g, gng, fg, cw, cb, lng, lnb,
      w_out_bf, wg_bf, wu_bf, wd_bf, sf0)


def kernel(x, c, ctx, c_ctx, w_mod, b_mod, norm1_g, norm2_g, w_in, conv_w, conv_b, conv_ln_g,
           conv_ln_b, w_a2_f, b_a_f, w_a2_b, b_a_b, gla_norm_g, w_out, w_gate, w_up, w_down, final_g):
    bsz, n_lat, d = x.shape
    assert bsz == 1 and d == D_MODEL and n_lat % TILE == 0 and ctx.shape[1] % CHUNK == 0
    assert n_lat % TILE1 == 0 and TILE1 % SUB == 0
    assert (TILE // CHUNK) % CONV_GROUP == 0
    assert w_mod.shape[0] == 1, "single layer"
    row = lambda a: a.reshape(1, -1)

    ct = jnp.concatenate([c.reshape(D_MODEL, 1), c_ctx.reshape(D_MODEL, 1)], axis=1)
    mod, silu_ct = _mod_call(ct, w_mod[0], row(b_mod[0]))

    wa2f, baf = w_a2_f[0], row(b_a_f[0])
    wa2b, bab = w_a2_b[0], row(b_a_b[0])
    n1g = row(norm1_g[0])

    sf0, sb0, w_in_bf = _ctx_call(ctx[0], mod, n1g, w_in[0].T, wa2f, baf, wa2b, bab)

    x2d = x[0]
    qk, laf, v, g, glu, ob, wo_bf, wg_bf, wu_bf, wd_bf, mod2 = _pass1_call(
        x2d, mod, n1g, w_in_bf, wa2f, baf, wa2b, bab, sb0, w_out[0], w_gate[0], w_up[0], w_down[0],
        silu_ct, w_mod[0], row(b_mod[0]))
    unslab = lambda w: w.reshape(-1, w.shape[-1])

    out = _pass2_call(
        x2d, qk, laf, v, g, glu, ob, mod2, row(norm2_g[0]), row(gla_norm_g[0]), row(final_g),
        conv_w[0], row(conv_b[0]), row(conv_ln_g[0]), row(conv_ln_b[0]),
        unslab(wo_bf), unslab(wg_bf), unslab(wu_bf), unslab(wd_bf), sf0)
    return out.reshape(bsz, n_lat, d)
```
